```python
import jax, jax.numpy as jnp
from jax import lax
import numpy as np

D_MODEL = 2048
BATCH = 4
SEQ = 2048
DEPTH = 1
DEC_BATCH = 128
DEC_SEQ = 1
PAST_LEN = 16384
PAGE_SIZE = 128

D_CONV = D_MODEL // 2
CONV_WIDTH = 3
D_HGRN = D_MODEL // 2
HGRN_KDIM = 128
HGRN_HEADS = D_HGRN // HGRN_KDIM
HGRN_VDIM = D_HGRN // HGRN_HEADS
CHUNK = 64
D_FF = 4 * D_MODEL
EPS = 1e-6
IN_SIZES = (D_CONV, D_CONV, D_CONV, D_HGRN, D_HGRN, D_HGRN, D_HGRN, D_MODEL, D_MODEL)
N_IN = sum(IN_SIZES)

kernel_name = "hybrid_shortconv_hgrn2_gated_merge_step"


def rmsnorm(x, g):
    xf = x.astype(jnp.float32)
    y = xf * lax.rsqrt(jnp.mean(xf * xf, axis=-1, keepdims=True) + EPS)
    return (y * g.astype(jnp.float32)).astype(x.dtype)


def hgrn2_chunked(q, k, v, log_f, s0):
    bn, t, h, _ = q.shape
    dv = v.shape[-1]
    L = min(CHUNK, t)
    pad = (-t) % L

    def prep(a):
        a = jnp.pad(a.astype(jnp.float32), ((0, 0), (0, pad), (0, 0), (0, 0)))
        return a.reshape(bn, -1, L, h, a.shape[-1]).transpose(0, 3, 1, 2, 4)

    qc, kc, vc, lf = prep(q), prep(k), prep(v), prep(log_f)
    b = jnp.cumsum(lf, axis=3)
    q_in = qc * jnp.exp(b)
    k_in = kc * jnp.exp(-b)
    scores = jnp.einsum('bhnld,bhnsd->bhnls', q_in, k_in)
    causal = jnp.tril(jnp.ones((L, L), dtype=bool))
    scores = jnp.where(causal, scores, 0.0)
    o_intra = jnp.einsum('bhnls,bhnse->bhnle', scores, vc)
    b_last = b[:, :, :, -1:, :]
    k_end = kc * jnp.exp(b_last - b)
    delta = jnp.einsum('bhnld,bhnle->nbhde', k_end, vc)
    decay = jnp.exp(b_last[:, :, :, 0, :]).transpose(2, 0, 1, 3)

    def step(S, inp):
        dec, dl = inp
        return dec[..., None] * S + dl, S

    s_fin, s_starts = lax.scan(step, s0.astype(jnp.float32), (decay, delta))
    o_inter = jnp.einsum('bhnld,nbhde->bhnle', q_in, s_starts)
    o = (o_intra + o_inter).transpose(0, 2, 3, 1, 4).reshape(bn, -1, h, dv)[:, :t]
    return o, s_fin


def layer(x, conv_state, hgrn_state, lb, norm_mix, w_in, conv_w, onorm_g,
          w_branch_a, w_branch_b, w_out, norm_ffn, w_up, w_down):
    bn, t, _ = x.shape
    hx = rmsnorm(x, norm_mix)
    proj = hx @ w_in
    cuts = [int(c) for c in np.cumsum(IN_SIZES)[:-1]]
    hc, bg, cg, q, fz, iv, og, ga, gb = jnp.split(proj, cuts, axis=-1)

    u = cg * hc
    buf = jnp.concatenate([conv_state.astype(u.dtype), u], axis=1)
    conv = conv_w[0] * buf[:, :t] + conv_w[1] * buf[:, 1:t + 1] + conv_w[2] * buf[:, 2:t + 2]
    a = bg * conv
    new_conv = buf[:, t:]

    f = lb + (1.0 - lb) * jax.nn.sigmoid(fz.astype(jnp.float32))
    log_f = jnp.log(f)
    kk = 1.0 - f
    shp = (bn, t, HGRN_HEADS, HGRN_KDIM)
    o, s_new = hgrn2_chunked(jax.nn.silu(q).reshape(shp), kk.reshape(shp),
                             iv.reshape(bn, t, HGRN_HEADS, HGRN_VDIM), log_f.reshape(shp),
                             hgrn_state)
    o = rmsnorm(o.astype(x.dtype), onorm_g) * jax.nn.silu(og.reshape(bn, t, HGRN_HEADS, HGRN_VDIM))
    o = o.reshape(bn, t, D_HGRN)

    mix = jax.nn.sigmoid(ga) * (a @ w_branch_a) + jax.nn.sigmoid(gb) * (o @ w_branch_b)
    x = x + mix @ w_out

    h2 = rmsnorm(x, norm_ffn)
    x = x + jnp.square(jax.nn.relu(h2 @ w_up)) @ w_down
    return x, new_conv, s_new.astype(hgrn_state.dtype)


def setup_inputs(seed: int = 0) -> dict:
    key = jax.random.key(seed)
    ks = jax.random.split(key, 16)
    f32 = jnp.float32
    nrm = lambda k, s, sc: jax.random.normal(k, s, f32) * sc
    return {
        "x_prompt": nrm(ks[0], (BATCH, SEQ, D_MODEL), 1.0),
        "x_sample": nrm(ks[1], (DEC_BATCH, DEC_SEQ, D_MODEL), 1.0),
        "state_conv": nrm(ks[2], (DEPTH, DEC_BATCH, CONV_WIDTH - 1, D_CONV), 1.0),
        "state_hgrn": nrm(ks[3], (DEPTH, DEC_BATCH, HGRN_HEADS, HGRN_KDIM, HGRN_VDIM), 1.0),
        "norm_mix": 1.0 + nrm(ks[4], (DEPTH, D_MODEL), 0.02),
        "w_in": nrm(ks[5], (DEPTH, D_MODEL, N_IN), D_MODEL ** -0.5),
        "conv_w": nrm(ks[6], (DEPTH, CONV_WIDTH, D_CONV), CONV_WIDTH ** -0.5),
        "lb_logits": nrm(ks[7], (DEPTH + 1, D_HGRN), 0.1),
        "onorm_g": 1.0 + nrm(ks[8], (DEPTH, HGRN_VDIM), 0.02),
        "w_branch_a": nrm(ks[9], (DEPTH, D_CONV, D_MODEL), D_CONV ** -0.5),
        "w_branch_b": nrm(ks[10], (DEPTH, D_HGRN, D_MODEL), D_HGRN ** -0.5),
        "w_out": nrm(ks[11], (DEPTH, D_MODEL, D_MODEL), D_MODEL ** -0.5),
        "norm_ffn": 1.0 + nrm(ks[12], (DEPTH, D_MODEL), 0.02),
        "w_up": nrm(ks[13], (DEPTH, D_MODEL, D_FF), D_MODEL ** -0.5),
        "w_down": nrm(ks[14], (DEPTH, D_FF, D_MODEL), D_FF ** -0.5),
        "norm_final": 1.0 + nrm(ks[15], (D_MODEL,), 0.02),
    }


def reference(x_prompt, x_sample, state_conv, state_hgrn, norm_mix, w_in, conv_w,
              lb_logits, onorm_g, w_branch_a, w_branch_b, w_out, norm_ffn, w_up,
              w_down, norm_final):
    lb_all = jnp.cumsum(jax.nn.softmax(lb_logits.astype(jnp.float32), axis=0), axis=0)
    yp, ys = x_prompt, x_sample
    conv_p, hgrn_p, conv_s, hgrn_s = [], [], [], []
    for l in range(DEPTH):
        params = (lb_all[l], norm_mix[l], w_in[l], conv_w[l], onorm_g[l], w_branch_a[l],
                  w_branch_b[l], w_out[l], norm_ffn[l], w_up[l], w_down[l])
        zc = jnp.zeros((yp.shape[0], CONV_WIDTH - 1, D_CONV), yp.dtype)
        zh = jnp.zeros((yp.shape[0], HGRN_HEADS, HGRN_KDIM, HGRN_VDIM), yp.dtype)
        yp, c, s = layer(yp, zc, zh, *params)
        conv_p.append(c)
        hgrn_p.append(s)
        ys, c, s = layer(ys, state_conv[l], state_hgrn[l], *params)
        conv_s.append(c)
        hgrn_s.append(s)
    yp = rmsnorm(yp, norm_final)
    ys = rmsnorm(ys, norm_final)
    return (yp, ys, jnp.stack(conv_p), jnp.stack(hgrn_p), jnp.stack(conv_s), jnp.stack(hgrn_s))
```

```python
import functools

import jax
import jax.numpy as jnp
from jax import lax
from jax.experimental import pallas as pl
from jax.experimental.pallas import tpu as pltpu

EPS = 1e-6
CHUNK = 64
HEAD_DIM = 128
V7X_VMEM_BYTES = 64 * 1024 * 1024
VMEM_LIMIT = V7X_VMEM_BYTES * 7 // 8

BF16 = jnp.bfloat16
F32 = jnp.float32


def _params(semantics):
    return pltpu.CompilerParams(dimension_semantics=semantics, vmem_limit_bytes=VMEM_LIMIT)


def _dot(a, b):
    return jnp.dot(a, b, preferred_element_type=F32)


def _dot_nt(a, b):
    return lax.dot_general(a, b, (((1,), (1,)), ((), ())), preferred_element_type=F32)


def _dot_tn(a, b):
    return lax.dot_general(a, b, (((0,), (0,)), ((), ())), preferred_element_type=F32)


def _sigmoid(x):
    return jax.nn.sigmoid(x)


def _silu(x):
    return x * jax.nn.sigmoid(x)


def _rmsnorm_kernel(x_ref, g_ref, o_ref):
    x = x_ref[...]
    ms = jnp.mean(x * x, axis=-1, keepdims=True)
    o_ref[...] = (x * lax.rsqrt(ms + EPS) * g_ref[...]).astype(o_ref.dtype)


def _rmsnorm(x, g, tm):
    m, d = x.shape
    return pl.pallas_call(
        _rmsnorm_kernel,
        grid=(m // tm,),
        in_specs=[pl.BlockSpec((tm, d), lambda i: (i, 0)),
                  pl.BlockSpec((1, d), lambda i: (0, 0))],
        out_specs=pl.BlockSpec((tm, d), lambda i: (i, 0)),
        out_shape=jax.ShapeDtypeStruct((m, d), BF16),
        compiler_params=_params(("parallel",)),
        name="rmsnorm",
    )(x, g.reshape(1, d))


def _conv_prompt_kernel(hx_ref, whc_ref, wbg_ref, wcg_ref, cw_ref, a_ref, nc_ref, carry_ref):
    t = pl.program_id(2)
    tm = hx_ref.shape[0]

    @pl.when(t == 0)
    def _():
        carry_ref[...] = jnp.zeros_like(carry_ref)

    hx = hx_ref[...]
    hc = _dot(hx, whc_ref[...])
    bg = _dot(hx, wbg_ref[...])
    cg = _dot(hx, wcg_ref[...])
    u = cg * hc
    c0 = carry_ref[0:1, :]
    c1 = carry_ref[1:2, :]
    row = lax.broadcasted_iota(jnp.int32, u.shape, 0)
    u1 = jnp.where(row == 0, c1, pltpu.roll(u, 1, 0))
    u2 = jnp.where(row == 0, c0, jnp.where(row == 1, c1, pltpu.roll(u, 2, 0)))
    cw = cw_ref[...]
    conv = cw[0:1, :] * u2 + cw[1:2, :] * u1 + cw[2:3, :] * u
    a_ref[...] = (bg * conv).astype(a_ref.dtype)
    tail = u[tm - 2:tm, :]
    carry_ref[0:2, :] = tail

    @pl.when(t == pl.num_programs(2) - 1)
    def _():
        nc_ref[0] = tail


def _conv_prompt(hx, w_in, conv_w, batch, seq, d_conv, tm, tc):
    m, d = hx.shape
    n_c = d_conv // tc
    n_t = seq // tm
    return pl.pallas_call(
        _conv_prompt_kernel,
        grid=(n_c, batch, n_t),
        in_specs=[pl.BlockSpec((tm, d), lambda c, b, t: (b * n_t + t, 0)),
                  pl.BlockSpec((d, tc), lambda c, b, t: (0, c)),
                  pl.BlockSpec((d, tc), lambda c, b, t: (0, n_c + c)),
                  pl.BlockSpec((d, tc), lambda c, b, t: (0, 2 * n_c + c)),
                  pl.BlockSpec((3, tc), lambda c, b, t: (0, c))],
        out_specs=[pl.BlockSpec((tm, tc), lambda c, b, t: (b * n_t + t, c)),
                   pl.BlockSpec((1, 2, tc), lambda c, b, t: (b, 0, c))],
        out_shape=[jax.ShapeDtypeStruct((m, d_conv), BF16),
                   jax.ShapeDtypeStruct((batch, 2, d_conv), F32)],
        scratch_shapes=[pltpu.VMEM((8, tc), F32)],
        compiler_params=_params(("arbitrary", "arbitrary", "arbitrary")),
        name="conv_prompt",
    )(hx, w_in, w_in, w_in, conv_w)


def _conv_sample_kernel(hx_ref, whc_ref, wbg_ref, wcg_ref, cw_ref, s0_ref, s1_ref, a_ref, u_ref):
    hx = hx_ref[...]
    hc = _dot(hx, whc_ref[...])
    bg = _dot(hx, wbg_ref[...])
    cg = _dot(hx, wcg_ref[...])
    u = cg * hc
    cw = cw_ref[...]
    conv = cw[0:1, :] * s0_ref[...] + cw[1:2, :] * s1_ref[...] + cw[2:3, :] * u
    a_ref[...] = (bg * conv).astype(a_ref.dtype)
    u_ref[...] = u


def _conv_sample(hx, w_in, conv_w, state2d, d_conv, tc):
    m, d = hx.shape
    n_c = d_conv // tc
    return pl.pallas_call(
        _conv_sample_kernel,
        grid=(n_c,),
        in_specs=[pl.BlockSpec((m, d), lambda c: (0, 0)),
                  pl.BlockSpec((d, tc), lambda c: (0, c)),
                  pl.BlockSpec((d, tc), lambda c: (0, n_c + c)),
                  pl.BlockSpec((d, tc), lambda c: (0, 2 * n_c + c)),
                  pl.BlockSpec((3, tc), lambda c: (0, c)),
                  pl.BlockSpec((m, tc), lambda c: (0, c)),
                  pl.BlockSpec((m, tc), lambda c: (0, n_c + c))],
        out_specs=[pl.BlockSpec((m, tc), lambda c: (0, c)),
                   pl.BlockSpec((m, tc), lambda c: (0, c))],
        out_shape=[jax.ShapeDtypeStruct((m, d_conv), BF16),
                   jax.ShapeDtypeStruct((m, d_conv), F32)],
        compiler_params=_params(("parallel",)),
        name="conv_sample",
    )(hx, w_in, w_in, w_in, conv_w, state2d, state2d)


def _proj_kernel(x_ref, w_ref, o_ref):
    o_ref[...] = _dot(x_ref[...], w_ref[...])


def _proj(x, w, col0, n, tm, tn):
    m, d = x.shape
    return pl.pallas_call(
        _proj_kernel,
        grid=(n // tn, m // tm),
        in_specs=[pl.BlockSpec((tm, d), lambda j, i: (i, 0)),
                  pl.BlockSpec((d, tn), lambda j, i: (0, col0 // tn + j))],
        out_specs=pl.BlockSpec((tm, tn), lambda j, i: (i, j)),
        out_shape=jax.ShapeDtypeStruct((m, n), F32),
        compiler_params=_params(("parallel", "parallel")),
        name="hgrn_proj",
    )(x, w)


def _lower_bound(lbl):
    e = jnp.exp(lbl - jnp.max(lbl, axis=0))
    return e[0] / jnp.sum(e, axis=0)


def _head_norm_gate(o, g, og):
    ms = jnp.mean(o * o, axis=-1, keepdims=True)
    return o * lax.rsqrt(ms + EPS) * g * _silu(og)


def _cumsum_rows(x, tril_bf16):
    hi = x.astype(BF16)
    r1 = x - hi.astype(F32)
    mid = r1.astype(BF16)
    lo = (r1 - mid.astype(F32)).astype(BF16)
    n = x.shape[1]
    parts = _dot(tril_bf16, jnp.concatenate([hi, mid, lo], axis=1))
    return parts[:, 0:n] + parts[:, n:2 * n] + parts[:, 2 * n:3 * n]


def _hgrn_prompt_kernel(q_ref, fz_ref, iv_ref, og_ref, lbl_ref, g_ref, o_ref, s_ref, st_ref):
    t = pl.program_id(2)
    tb = q_ref.shape[0]

    @pl.when(t == 0)
    def _():
        st_ref[...] = jnp.zeros_like(st_ref)

    lb = _lower_bound(lbl_ref[...])
    g = g_ref[...]
    row = lax.broadcasted_iota(jnp.int32, (CHUNK, CHUNK), 0)
    col = lax.broadcasted_iota(jnp.int32, (CHUNK, CHUNK), 1)
    causal = row >= col
    tril = causal.astype(BF16)

    st = st_ref[...]
    for c in range(tb // CHUNK):
        sl = pl.ds(c * CHUNK, CHUNK)
        qs = _silu(q_ref[sl, :])
        f = lb + (1.0 - lb) * _sigmoid(fz_ref[sl, :])
        kk = 1.0 - f
        v = iv_ref[sl, :].astype(BF16)
        b = _cumsum_rows(jnp.log(f), tril)
        b_last = b[CHUNK - 1:CHUNK, :]
        q_in = (qs * jnp.exp(b)).astype(BF16)
        k_in = (kk * jnp.exp(-b)).astype(BF16)
        k_end = (kk * jnp.exp(b_last - b)).astype(BF16)
        scores = jnp.where(causal, _dot_nt(q_in, k_in), 0.0)
        o = _dot(scores.astype(BF16), v) + _dot_nt(q_in, st.astype(BF16))
        st = jnp.exp(b_last) * st + _dot_tn(v, k_end)
        o_ref[sl, :] = _head_norm_gate(o, g, og_ref[sl, :]).astype(o_ref.dtype)
    st_ref[...] = st

    @pl.when(t == pl.num_programs(2) - 1)
    def _():
        s_ref[0, 0] = st.T


def _hgrn_prompt(p, lb_logits, onorm_g, batch, seq, heads, tb):
    m = p.shape[0]
    n_t = seq // tb
    dk = HEAD_DIM

    def col(k):
        return pl.BlockSpec((tb, dk), lambda b, h, t: (b * n_t + t, k * heads + h))

    return pl.pallas_call(
        _hgrn_prompt_kernel,
        grid=(batch, heads, n_t),
        in_specs=[col(0), col(1), col(2), col(3),
                  pl.BlockSpec((2, 1, dk), lambda b, h, t: (0, 0, h)),
                  pl.BlockSpec((1, dk), lambda b, h, t: (0, 0))],
        out_specs=[pl.BlockSpec((tb, dk), lambda b, h, t: (b * n_t + t, h)),
                   pl.BlockSpec((1, 1, dk, dk), lambda b, h, t: (b, h, 0, 0))],
        out_shape=[jax.ShapeDtypeStruct((m, heads * dk), BF16),
                   jax.ShapeDtypeStruct((batch, heads, dk, dk), F32)],
        scratch_shapes=[pltpu.VMEM((dk, dk), F32)],
        compiler_params=_params(("arbitrary", "arbitrary", "arbitrary")),
        name="hgrn_prompt",
    )(p, p, p, p, lb_logits.reshape(2, 1, heads * dk), onorm_g.reshape(1, dk))


def _rows_to_cols(x):
    h, d = x.shape
    return jnp.concatenate([x, jnp.zeros((d - h, d), x.dtype)], axis=0).T


def _hgrn_sample_kernel(p_ref, lbl_ref, g_ref, s0_ref, o_ref, sn_ref):
    heads = s0_ref.shape[1]
    p = p_ref[0]
    qs = _silu(p[0:heads])
    lb = _lower_bound(lbl_ref[...])
    f = lb + (1.0 - lb) * _sigmoid(p[heads:2 * heads])
    kk = 1.0 - f
    v = p[2 * heads:3 * heads]
    og = p[3 * heads:4 * heads]
    o_intra = jnp.sum(qs * kk, axis=-1, keepdims=True) * v
    f_c = _rows_to_cols(f)
    k_c = _rows_to_cols(kk)
    q_c = _rows_to_cols(qs * f)
    o_inter = []
    for h in range(heads):
        s = s0_ref[0, h]
        sn_ref[0, h] = f_c[:, h:h + 1] * s + k_c[:, h:h + 1] * v[h:h + 1, :]
        o_inter.append(jnp.sum(q_c[:, h:h + 1] * s, axis=0, keepdims=True))
    o = o_intra + jnp.concatenate(o_inter, axis=0)
    o_ref[0] = _head_norm_gate(o, g_ref[...], og)


def _hgrn_sample(p3, lbl3, onorm_g, state):
    n, rows, dk = p3.shape
    heads = rows // 4
    return pl.pallas_call(
        _hgrn_sample_kernel,
        grid=(n,),
        in_specs=[pl.BlockSpec((1, rows, dk), lambda b: (b, 0, 0)),
                  pl.BlockSpec((2, heads, dk), lambda b: (0, 0, 0)),
                  pl.BlockSpec((1, dk), lambda b: (0, 0)),
                  pl.BlockSpec((1, heads, dk, dk), lambda b: (b, 0, 0, 0))],
        out_specs=[pl.BlockSpec((1, heads, dk), lambda b: (b, 0, 0)),
                   pl.BlockSpec((1, heads, dk, dk), lambda b: (b, 0, 0, 0))],
        out_shape=[jax.ShapeDtypeStruct((n, heads, dk), F32),
                   jax.ShapeDtypeStruct((n, heads, dk, dk), F32)],
        compiler_params=_params(("parallel",)),
        name="hgrn_sample",
    )(p3, lbl3, onorm_g.reshape(1, dk), state)


def _merge_kernel(hx_ref, a_ref, o_ref, wga_ref, wgb_ref, wa_ref, wb_ref, mix_ref):
    hx = hx_ref[...]
    ga = _sigmoid(_dot(hx, wga_ref[...]))
    gb = _sigmoid(_dot(hx, wgb_ref[...]))
    ya = _dot(a_ref[...].astype(BF16), wa_ref[...])
    yb = _dot(o_ref[...].astype(BF16), wb_ref[...])
    mix_ref[...] = (ga * ya + gb * yb).astype(mix_ref.dtype)


def _merge(hx, a, o, w_in, w_a, w_b, col_ga, tm, tn):
    m, d = hx.shape
    dc = a.shape[1]
    dh = o.shape[1]
    n_n = d // tn
    return pl.pallas_call(
        _merge_kernel,
        grid=(n_n, m // tm),
        in_specs=[pl.BlockSpec((tm, d), lambda j, i: (i, 0)),
                  pl.BlockSpec((tm, dc), lambda j, i: (i, 0)),
                  pl.BlockSpec((tm, dh), lambda j, i: (i, 0)),
                  pl.BlockSpec((d, tn), lambda j, i: (0, col_ga // tn + j)),
                  pl.BlockSpec((d, tn), lambda j, i: (0, col_ga // tn + n_n + j)),
                  pl.BlockSpec((dc, tn), lambda j, i: (0, j)),
                  pl.BlockSpec((dh, tn), lambda j, i: (0, j))],
        out_specs=pl.BlockSpec((tm, tn), lambda j, i: (i, j)),
        out_shape=jax.ShapeDtypeStruct((m, d), BF16),
        compiler_params=_params(("parallel", "parallel")),
        name="merge",
    )(hx, a, o, w_in, w_in, w_a, w_b)


def _outproj_kernel(x_ref, mix_ref, w_ref, g_ref, x1_ref, h2_ref):
    x1 = x_ref[...] + _dot(mix_ref[...], w_ref[...])
    x1_ref[...] = x1
    ms = jnp.mean(x1 * x1, axis=-1, keepdims=True)
    h2_ref[...] = (x1 * lax.rsqrt(ms + EPS) * g_ref[...]).astype(h2_ref.dtype)


def _outproj(x, mix, w_out, g, tm):
    m, d = x.shape
    return pl.pallas_call(
        _outproj_kernel,
        grid=(m // tm,),
        in_specs=[pl.BlockSpec((tm, d), lambda i: (i, 0)),
                  pl.BlockSpec((tm, d), lambda i: (i, 0)),
                  pl.BlockSpec((d, d), lambda i: (0, 0)),
                  pl.BlockSpec((1, d), lambda i: (0, 0))],
        out_specs=[pl.BlockSpec((tm, d), lambda i: (i, 0)),
                   pl.BlockSpec((tm, d), lambda i: (i, 0))],
        out_shape=[jax.ShapeDtypeStruct((m, d), F32),
                   jax.ShapeDtypeStruct((m, d), BF16)],
        compiler_params=_params(("parallel",)),
        name="outproj",
    )(x, mix, w_out, g.reshape(1, d))


def _mlp_kernel(h2_ref, x1_ref, wup_ref, wdn_ref, g_ref, y_ref, acc_ref):
    j = pl.program_id(1)

    @pl.when(j == 0)
    def _():
        acc_ref[...] = jnp.zeros_like(acc_ref)

    h = jnp.maximum(_dot(h2_ref[...], wup_ref[...]), 0.0)
    acc_ref[...] += _dot((h * h).astype(BF16), wdn_ref[...])

    @pl.when(j == pl.num_programs(1) - 1)
    def _():
        y = x1_ref[...] + acc_ref[...]
        ms = jnp.mean(y * y, axis=-1, keepdims=True)
        y_ref[...] = y * lax.rsqrt(ms + EPS) * g_ref[...]


def _mlp(h2, x1, w_up, w_down, g, tm, tf):
    m, d = h2.shape
    dff = w_up.shape[1]
    return pl.pallas_call(
        _mlp_kernel,
        grid=(m // tm, dff // tf),
        in_specs=[pl.BlockSpec((tm, d), lambda i, j: (i, 0)),
                  pl.BlockSpec((tm, d), lambda i, j: (i, 0)),
                  pl.BlockSpec((d, tf), lambda i, j: (0, j)),
                  pl.BlockSpec((tf, d), lambda i, j: (j, 0)),
                  pl.BlockSpec((1, d), lambda i, j: (0, 0))],
        out_specs=pl.BlockSpec((tm, d), lambda i, j: (i, 0)),
        out_shape=jax.ShapeDtypeStruct((m, d), F32),
        scratch_shapes=[pltpu.VMEM((tm, d), F32)],
        compiler_params=_params(("parallel", "arbitrary")),
        name="mlp",
    )(h2, x1, w_up, w_down, g.reshape(1, d))


def kernel(x_prompt, x_sample, state_conv, state_hgrn, norm_mix, w_in, conv_w, lb_logits, onorm_g,
           w_branch_a, w_branch_b, w_out, norm_ffn, w_up, w_down, norm_final):
    batch, seq, d = x_prompt.shape
    n_dec = x_sample.shape[0]
    depth, _, d_conv = conv_w.shape
    heads, dk = state_hgrn.shape[2], state_hgrn.shape[3]
    d_hgrn = heads * dk
    assert depth == 1 and x_sample.shape[1] == 1 and dk == HEAD_DIM and state_hgrn.shape[4] == dk
    assert seq % CHUNK == 0
    col_hgrn = 3 * d_conv
    col_ga = col_hgrn + 4 * d_hgrn

    w_in_b = w_in[0].astype(BF16)
    w_a_b = w_branch_a[0].astype(BF16)
    w_b_b = w_branch_b[0].astype(BF16)
    w_out_b = w_out[0].astype(BF16)
    w_up_b = w_up[0].astype(BF16)
    w_dn_b = w_down[0].astype(BF16)
    cw = conv_w[0]
    lbl = lb_logits

    xp = x_prompt.reshape(batch * seq, d)
    hx = _rmsnorm(xp, norm_mix[0], 512)
    a_p, conv_p = _conv_prompt(hx, w_in_b, cw, batch, seq, d_conv, 512, 512)
    p_p = _proj(hx, w_in_b, col_hgrn, 4 * d_hgrn, 512, 1024)
    o_p, hgrn_p = _hgrn_prompt(p_p, lbl, onorm_g[0], batch, seq, heads, 256)
    mix_p = _merge(hx, a_p, o_p, w_in_b, w_a_b, w_b_b, col_ga, 512, 1024)
    x1_p, h2_p = _outproj(xp, mix_p, w_out_b, norm_ffn[0], 512)
    y_p = _mlp(h2_p, x1_p, w_up_b, w_dn_b, norm_final, 512, 1024)

    xs = x_sample.reshape(n_dec, d)
    hs = _rmsnorm(xs, norm_mix[0], n_dec)
    st2 = state_conv[0].reshape(n_dec, 2 * d_conv)
    a_s, u_s = _conv_sample(hs, w_in_b, cw, st2, d_conv, 512)
    p_s = _proj(hs, w_in_b, col_hgrn, 4 * d_hgrn, n_dec, 1024)
    o_s, hgrn_s = _hgrn_sample(p_s.reshape(n_dec, 4 * heads, dk), lbl.reshape(2, heads, dk),
                               onorm_g[0], state_hgrn[0])
    mix_s = _merge(hs, a_s, o_s.reshape(n_dec, d_hgrn), w_in_b, w_a_b, w_b_b, col_ga, n_dec, 1024)
    x1_s, h2_s = _outproj(xs, mix_s, w_out_b, norm_ffn[0], n_dec)
    y_s = _mlp(h2_s, x1_s, w_up_b, w_dn_b, norm_final, n_dec, 1024)

    conv_s = jnp.stack([state_conv[0, :, 1, :], u_s], axis=1)
    return (y_p.reshape(batch, seq, d), y_s.reshape(n_dec, 1, d),
            conv_p[None], hgrn_p[None], conv_s[None], hgrn_s[None])
```

```python
import functools

import jax
import jax.numpy as jnp
from jax import lax
from jax.experimental import pallas as pl
from jax.experimental.pallas import tpu as pltpu

EPS = 1e-6
CHUNK = 64
HEAD_DIM = 128
V7X_VMEM_BYTES = 64 * 1024 * 1024
VMEM_LIMIT = V7X_VMEM_BYTES * 7 // 8

BF16 = jnp.bfloat16
F32 = jnp.float32


def _params(semantics):
    return pltpu.CompilerParams(dimension_semantics=semantics, vmem_limit_bytes=VMEM_LIMIT)


def _dot(a, b):
    return jnp.dot(a, b, preferred_element_type=F32)


def _dot_nt(a, b):
    return lax.dot_general(a, b, (((1,), (1,)), ((), ())), preferred_element_type=F32)


def _dot_tn(a, b):
    return lax.dot_general(a, b, (((0,), (0,)), ((), ())), preferred_element_type=F32)


def _sigmoid(x):
    return jax.nn.sigmoid(x)


def _silu(x):
    return x * jax.nn.sigmoid(x)


def _rmsnorm_kernel(x_ref, g_ref, o_ref):
    x = x_ref[...]
    ms = jnp.mean(x * x, axis=-1, keepdims=True)
    o_ref[...] = (x * lax.rsqrt(ms + EPS) * g_ref[...]).astype(o_ref.dtype)


def _rmsnorm(x, g, tm):
    m, d = x.shape
    return pl.pallas_call(
        _rmsnorm_kernel,
        grid=(m // tm,),
        in_specs=[pl.BlockSpec((tm, d), lambda i: (i, 0)),
                  pl.BlockSpec((1, d), lambda i: (0, 0))],
        out_specs=pl.BlockSpec((tm, d), lambda i: (i, 0)),
        out_shape=jax.ShapeDtypeStruct((m, d), BF16),
        compiler_params=_params(("parallel",)),
        name="rmsnorm",
    )(x, g.reshape(1, d))


def _conv_prompt_kernel(hx_ref, whc_ref, wbg_ref, wcg_ref, cw_ref, a_ref, nc_ref, carry_ref):
    t = pl.program_id(2)
    tm = hx_ref.shape[0]

    @pl.when(t == 0)
    def _():
        carry_ref[...] = jnp.zeros_like(carry_ref)

    hx = hx_ref[...]
    hc = _dot(hx, whc_ref[...])
    bg = _dot(hx, wbg_ref[...])
    cg = _dot(hx, wcg_ref[...])
    u = cg * hc
    c0 = carry_ref[0:1, :]
    c1 = carry_ref[1:2, :]
    row = lax.broadcasted_iota(jnp.int32, u.shape, 0)
    u1 = jnp.where(row == 0, c1, pltpu.roll(u, 1, 0))
    u2 = jnp.where(row == 0, c0, jnp.where(row == 1, c1, pltpu.roll(u, 2, 0)))
    cw = cw_ref[...]
    conv = cw[0:1, :] * u2 + cw[1:2, :] * u1 + cw[2:3, :] * u
    a_ref[...] = (bg * conv).astype(a_ref.dtype)
    tail = u[tm - 2:tm, :]
    carry_ref[0:2, :] = tail

    @pl.when(t == pl.num_programs(2) - 1)
    def _():
        nc_ref[0] = tail


def _conv_prompt(hx, w_in, conv_w, batch, seq, d_conv, tm, tc):
    m, d = hx.shape
    n_c = d_conv // tc
    n_t = seq // tm
    return pl.pallas_call(
        _conv_prompt_kernel,
        grid=(n_c, batch, n_t),
        in_specs=[pl.BlockSpec((tm, d), lambda c, b, t: (b * n_t + t, 0)),
                  pl.BlockSpec((d, tc), lambda c, b, t: (0, c)),
                  pl.BlockSpec((d, tc), lambda c, b, t: (0, n_c + c)),
                  pl.BlockSpec((d, tc), lambda c, b, t: (0, 2 * n_c + c)),
                  pl.BlockSpec((3, tc), lambda c, b, t: (0, c))],
        out_specs=[pl.BlockSpec((tm, tc), lambda c, b, t: (b * n_t + t, c)),
                   pl.BlockSpec((1, 2, tc), lambda c, b, t: (b, 0, c))],
        out_shape=[jax.ShapeDtypeStruct((m, d_conv), BF16),
                   jax.ShapeDtypeStruct((batch, 2, d_conv), F32)],
        scratch_shapes=[pltpu.VMEM((8, tc), F32)],
        compiler_params=_params(("arbitrary", "arbitrary", "arbitrary")),
        name="conv_prompt",
    )(hx, w_in, w_in, w_in, conv_w)


def _conv_sample_kernel(hx_ref, whc_ref, wbg_ref, wcg_ref, cw_ref, s0_ref, s1_ref, a_ref, u_ref):
    hx = hx_ref[...]
    hc = _dot(hx, whc_ref[...])
    bg = _dot(hx, wbg_ref[...])
    cg = _dot(hx, wcg_ref[...])
    u = cg * hc
    cw = cw_ref[...]
    conv = cw[0:1, :] * s0_ref[...] + cw[1:2, :] * s1_ref[...] + cw[2:3, :] * u
    a_ref[...] = (bg * conv).astype(a_ref.dtype)
    u_ref[...] = u


def _conv_sample(hx, w_in, conv_w, state2d, d_conv, tc):
    m, d = hx.shape
    n_c = d_conv // tc
    return pl.pallas_call(
        _conv_sample_kernel,
        grid=(n_c,),
        in_specs=[pl.BlockSpec((m, d), lambda c: (0, 0)),
                  pl.BlockSpec((d, tc), lambda c: (0, c)),
                  pl.BlockSpec((d, tc), lambda c: (0, n_c + c)),
                  pl.BlockSpec((d, tc), lambda c: (0, 2 * n_c + c)),
                  pl.BlockSpec((3, tc), lambda c: (0, c)),
                  pl.BlockSpec((m, tc), lambda c: (0, c)),
                  pl.BlockSpec((m, tc), lambda c: (0, n_c + c))],
        out_specs=[pl.BlockSpec((m, tc), lambda c: (0, c)),
                   pl.BlockSpec((m, tc), lambda c: (0, c))],
        out_shape=[jax.ShapeDtypeStruct((m, d_conv), BF16),
                   jax.ShapeDtypeStruct((m, d_conv), F32)],
        compiler_params=_params(("parallel",)),
        name="conv_sample",
    )(hx, w_in, w_in, w_in, conv_w, state2d, state2d)


def _proj_kernel(x_ref, w_ref, o_ref):
    o_ref[...] = _dot(x_ref[...], w_ref[...])


def _proj(x, w, col0, n, tm, tn):
    m, d = x.shape
    return pl.pallas_call(
        _proj_kernel,
        grid=(n // tn, m // tm),
        in_specs=[pl.BlockSpec((tm, d), lambda j, i: (i, 0)),
                  pl.BlockSpec((d, tn), lambda j, i: (0, col0 // tn + j))],
        out_specs=pl.BlockSpec((tm, tn), lambda j, i: (i, j)),
        out_shape=jax.ShapeDtypeStruct((m, n), F32),
        compiler_params=_params(("parallel", "parallel")),
        name="hgrn_proj",
    )(x, w)


def _lower_bound(lbl):
    e = jnp.exp(lbl - jnp.max(lbl, axis=0))
    return e[0] / jnp.sum(e, axis=0)


def _head_norm_gate(o, g, og):
    ms = jnp.mean(o * o, axis=-1, keepdims=True)
    return o * lax.rsqrt(ms + EPS) * g * _silu(og)


def _cumsum_rows(x, tril_bf16):
    hi = x.astype(BF16)
    r1 = x - hi.astype(F32)
    mid = r1.astype(BF16)
    lo = (r1 - mid.astype(F32)).astype(BF16)
    n = x.shape[1]
    parts = _dot(tril_bf16, jnp.concatenate([hi, mid, lo], axis=1))
    return parts[:, 0:n] + parts[:, n:2 * n] + parts[:, 2 * n:3 * n]


def _hgrn_prompt_kernel(q_ref, fz_ref, iv_ref, og_ref, lbl_ref, g_ref, o_ref, s_ref, st_ref):
    t = pl.program_id(2)
    tb = q_ref.shape[0]
    heads_per_step = st_ref.shape[0]
    dk = HEAD_DIM
    n_chunks = tb // CHUNK

    @pl.when(t == 0)
    def _():
        st_ref[...] = jnp.zeros_like(st_ref)

    lb_all = _lower_bound(lbl_ref[...])
    g = g_ref[...]
    row = lax.broadcasted_iota(jnp.int32, (tb, tb), 0)
    col = lax.broadcasted_iota(jnp.int32, (tb, tb), 1)
    shift = CHUNK.bit_length() - 1
    causal = (row >= col) & (jnp.right_shift(row, shift) == jnp.right_shift(col, shift))
    tril = causal.astype(BF16)

    for hh in range(heads_per_step):
        cs = slice(hh * dk, (hh + 1) * dk)
        lb = lb_all[:, cs]
        qs = _silu(q_ref[:, cs])
        f = lb + (1.0 - lb) * _sigmoid(fz_ref[:, cs])
        kk = 1.0 - f
        v = iv_ref[:, cs].astype(BF16)
        b = _cumsum_rows(jnp.log(f), tril)
        b3 = b.reshape(n_chunks, CHUNK, dk)
        b_last = b3[:, CHUNK - 1:CHUNK, :]
        b_end = jnp.broadcast_to(b_last, b3.shape).reshape(tb, dk)
        q_in = (qs * jnp.exp(b)).astype(BF16)
        k_in = (kk * jnp.exp(-b)).astype(BF16)
        k_end = (kk * jnp.exp(b_end - b)).astype(BF16)
        scores = jnp.where(causal, _dot_nt(q_in, k_in), 0.0)
        o_intra = _dot(scores.astype(BF16), v)
        st = st_ref[hh]
        o_inter = []
        for c in range(n_chunks):
            rs = slice(c * CHUNK, (c + 1) * CHUNK)
            o_inter.append(_dot_nt(q_in[rs], st.astype(BF16)))
            st = jnp.exp(b_last[c]) * st + _dot_tn(v[rs], k_end[rs])
        st_ref[hh] = st
        o = o_intra + jnp.concatenate(o_inter, axis=0)
        o_ref[:, cs] = _head_norm_gate(o, g, og_ref[:, cs]).astype(o_ref.dtype)

    @pl.when(t == pl.num_programs(2) - 1)
    def _():
        for hh in range(heads_per_step):
            s_ref[0, hh] = st_ref[hh].T


def _hgrn_prompt(p, lb_logits, onorm_g, batch, seq, heads, tb, hp):
    m = p.shape[0]
    n_t = seq // tb
    n_h = heads // hp
    dk = HEAD_DIM
    assert CHUNK & (CHUNK - 1) == 0

    def col(k):
        return pl.BlockSpec((tb, hp * dk), lambda b, h, t: (b * n_t + t, k * n_h + h))

    return pl.pallas_call(
        _hgrn_prompt_kernel,
        grid=(batch, n_h, n_t),
        in_specs=[col(0), col(1), col(2), col(3),
                  pl.BlockSpec((2, 1, hp * dk), lambda b, h, t: (0, 0, h)),
                  pl.BlockSpec((1, dk), lambda b, h, t: (0, 0))],
        out_specs=[pl.BlockSpec((tb, hp * dk), lambda b, h, t: (b * n_t + t, h)),
                   pl.BlockSpec((1, hp, dk, dk), lambda b, h, t: (b, h, 0, 0))],
        out_shape=[jax.ShapeDtypeStruct((m, heads * dk), BF16),
                   jax.ShapeDtypeStruct((batch, heads, dk, dk), F32)],
        scratch_shapes=[pltpu.VMEM((hp, dk, dk), F32)],
        compiler_params=_params(("arbitrary", "arbitrary", "arbitrary")),
        name="hgrn_prompt",
    )(p, p, p, p, lb_logits.reshape(2, 1, heads * dk), onorm_g.reshape(1, dk))


def _hgrn_sample_kernel(p_ref, lbl_ref, g_ref, s0_ref, o_ref, sn_ref):
    bb, heads, dk = s0_ref.shape[0], s0_ref.shape[1], s0_ref.shape[2]
    lb = _lower_bound(lbl_ref[...])
    g = g_ref[...]
    rows, per_seq = [], []
    for i in range(bb):
        p = p_ref[i]
        qs = _silu(p[0:heads])
        f = lb + (1.0 - lb) * _sigmoid(p[heads:2 * heads])
        kk = 1.0 - f
        rows += [f, kk, qs * f]
        per_seq.append((qs, kk, p[2 * heads:3 * heads], p[3 * heads:4 * heads]))
    pad = dk - 3 * heads * bb
    cols = jnp.concatenate(rows + [jnp.zeros((pad, dk), F32)], axis=0).T
    for i in range(bb):
        qs, kk, v, og = per_seq[i]
        o_inter = []
        for h in range(heads):
            c0 = 3 * heads * i + h
            f_c = cols[:, c0:c0 + 1]
            k_c = cols[:, c0 + heads:c0 + heads + 1]
            q_c = cols[:, c0 + 2 * heads:c0 + 2 * heads + 1]
            s = s0_ref[i, h]
            sn_ref[i, h] = f_c * s + k_c * v[h:h + 1, :]
            o_inter.append(jnp.sum(q_c * s, axis=0, keepdims=True))
        o = jnp.sum(qs * kk, axis=-1, keepdims=True) * v + jnp.concatenate(o_inter, axis=0)
        o_ref[i] = _head_norm_gate(o, g, og)


def _hgrn_sample(p3, lbl3, onorm_g, state, bb):
    n, rows, dk = p3.shape
    heads = rows // 4
    assert n % bb == 0 and 3 * heads * bb <= dk
    return pl.pallas_call(
        _hgrn_sample_kernel,
        grid=(n // bb,),
        in_specs=[pl.BlockSpec((bb, rows, dk), lambda b: (b, 0, 0)),
                  pl.BlockSpec((2, heads, dk), lambda b: (0, 0, 0)),
                  pl.BlockSpec((1, dk), lambda b: (0, 0)),
                  pl.BlockSpec((bb, heads, dk, dk), lambda b: (b, 0, 0, 0))],
        out_specs=[pl.BlockSpec((bb, heads, dk), lambda b: (b, 0, 0)),
                   pl.BlockSpec((bb, heads, dk, dk), lambda b: (b, 0, 0, 0))],
        out_shape=[jax.ShapeDtypeStruct((n, heads, dk), F32),
                   jax.ShapeDtypeStruct((n, heads, dk, dk), F32)],
        compiler_params=_params(("parallel",)),
        name="hgrn_sample",
    )(p3, lbl3, onorm_g.reshape(1, dk), state)


def _merge_kernel(hx_ref, a_ref, o_ref, wga_ref, wgb_ref, wa_ref, wb_ref, mix_ref):
    hx = hx_ref[...]
    ga = _sigmoid(_dot(hx, wga_ref[...]))
    gb = _sigmoid(_dot(hx, wgb_ref[...]))
    ya = _dot(a_ref[...].astype(BF16), wa_ref[...])
    yb = _dot(o_ref[...].astype(BF16), wb_ref[...])
    mix_ref[...] = (ga * ya + gb * yb).astype(mix_ref.dtype)


def _merge(hx, a, o, w_in, w_a, w_b, col_ga, tm, tn):
    m, d = hx.shape
    dc = a.shape[1]
    dh = o.shape[1]
    n_n = d // tn
    return pl.pallas_call(
        _merge_kernel,
        grid=(n_n, m // tm),
        in_specs=[pl.BlockSpec((tm, d), lambda j, i: (i, 0)),
                  pl.BlockSpec((tm, dc), lambda j, i: (i, 0)),
                  pl.BlockSpec((tm, dh), lambda j, i: (i, 0)),
                  pl.BlockSpec((d, tn), lambda j, i: (0, col_ga // tn + j)),
                  pl.BlockSpec((d, tn), lambda j, i: (0, col_ga // tn + n_n + j)),
                  pl.BlockSpec((dc, tn), lambda j, i: (0, j)),
                  pl.BlockSpec((dh, tn), lambda j, i: (0, j))],
        out_specs=pl.BlockSpec((tm, tn), lambda j, i: (i, j)),
        out_shape=jax.ShapeDtypeStruct((m, d), BF16),
        compiler_params=_params(("parallel", "parallel")),
        name="merge",
    )(hx, a, o, w_in, w_in, w_a, w_b)


def _outproj_kernel(x_ref, mix_ref, w_ref, g_ref, x1_ref, h2_ref):
    x1 = x_ref[...] + _dot(mix_ref[...], w_ref[...])
    x1_ref[...] = x1
    ms = jnp.mean(x1 * x1, axis=-1, keepdims=True)
    h2_ref[...] = (x1 * lax.rsqrt(ms + EPS) * g_ref[...]).astype(h2_ref.dtype)


def _outproj(x, mix, w_out, g, tm):
    m, d = x.shape
    return pl.pallas_call(
        _outproj_kernel,
        grid=(m // tm,),
        in_specs=[pl.BlockSpec((tm, d), lambda i: (i, 0)),
                  pl.BlockSpec((tm, d), lambda i: (i, 0)),
                  pl.BlockSpec((d, d), lambda i: (0, 0)),
                  pl.BlockSpec((1, d), lambda i: (0, 0))],
        out_specs=[pl.BlockSpec((tm, d), lambda i: (i, 0)),
                   pl.BlockSpec((tm, d), lambda i: (i, 0))],
        out_shape=[jax.ShapeDtypeStruct((m, d), F32),
                   jax.ShapeDtypeStruct((m, d), BF16)],
        compiler_params=_params(("parallel",)),
        name="outproj",
    )(x, mix, w_out, g.reshape(1, d))


def _mlp_kernel(h2_ref, x1_ref, wup_ref, wdn_ref, g_ref, y_ref, acc_ref):
    j = pl.program_id(1)

    @pl.when(j == 0)
    def _():
        acc_ref[...] = jnp.zeros_like(acc_ref)

    h = jnp.maximum(_dot(h2_ref[...], wup_ref[...]), 0.0)
    acc_ref[...] += _dot((h * h).astype(BF16), wdn_ref[...])

    @pl.when(j == pl.num_programs(1) - 1)
    def _():
        y = x1_ref[...] + acc_ref[...]
        ms = jnp.mean(y * y, axis=-1, keepdims=True)
        y_ref[...] = y * lax.rsqrt(ms + EPS) * g_ref[...]


def _mlp(h2, x1, w_up, w_down, g, tm, tf):
    m, d = h2.shape
    dff = w_up.shape[1]
    return pl.pallas_call(
        _mlp_kernel,
        grid=(m // tm, dff // tf),
        in_specs=[pl.BlockSpec((tm, d), lambda i, j: (i, 0)),
                  pl.BlockSpec((tm, d), lambda i, j: (i, 0)),
                  pl.BlockSpec((d, tf), lambda i, j: (0, j)),
                  pl.BlockSpec((tf, d), lambda i, j: (j, 0)),
                  pl.BlockSpec((1, d), lambda i, j: (0, 0))],
        out_specs=pl.BlockSpec((tm, d), lambda i, j: (i, 0)),
        out_shape=jax.ShapeDtypeStruct((m, d), F32),
        scratch_shapes=[pltpu.VMEM((tm, d), F32)],
        compiler_params=_params(("parallel", "arbitrary")),
        name="mlp",
    )(h2, x1, w_up, w_down, g.reshape(1, d))


def kernel(x_prompt, x_sample, state_conv, state_hgrn, norm_mix, w_in, conv_w, lb_logits, onorm_g,
           w_branch_a, w_branch_b, w_out, norm_ffn, w_up, w_down, norm_final):
    batch, seq, d = x_prompt.shape
    n_dec = x_sample.shape[0]
    depth, _, d_conv = conv_w.shape
    heads, dk = state_hgrn.shape[2], state_hgrn.shape[3]
    d_hgrn = heads * dk
    assert depth == 1 and x_sample.shape[1] == 1 and dk == HEAD_DIM and state_hgrn.shape[4] == dk
    assert seq % CHUNK == 0
    col_hgrn = 3 * d_conv
    col_ga = col_hgrn + 4 * d_hgrn

    w_in_b = w_in[0].astype(BF16)
    w_a_b = w_branch_a[0].astype(BF16)
    w_b_b = w_branch_b[0].astype(BF16)
    w_out_b = w_out[0].astype(BF16)
    w_up_b = w_up[0].astype(BF16)
    w_dn_b = w_down[0].astype(BF16)
    cw = conv_w[0]
    lbl = lb_logits

    xp = x_prompt.reshape(batch * seq, d)
    hx = _rmsnorm(xp, norm_mix[0], 512)
    a_p, conv_p = _conv_prompt(hx, w_in_b, cw, batch, seq, d_conv, 512, 512)
    p_p = _proj(hx, w_in_b, col_hgrn, 4 * d_hgrn, 512, 1024)
    o_p, hgrn_p = _hgrn_prompt(p_p, lbl, onorm_g[0], batch, seq, heads, 256, 4)
    mix_p = _merge(hx, a_p, o_p, w_in_b, w_a_b, w_b_b, col_ga, 512, 1024)
    x1_p, h2_p = _outproj(xp, mix_p, w_out_b, norm_ffn[0], 512)
    y_p = _mlp(h2_p, x1_p, w_up_b, w_dn_b, norm_final, 512, 1024)

    xs = x_sample.reshape(n_dec, d)
    hs = _rmsnorm(xs, norm_mix[0], n_dec)
    st2 = state_conv[0].reshape(n_dec, 2 * d_conv)
    a_s, u_s = _conv_sample(hs, w_in_b, cw, st2, d_conv, 512)
    p_s = _proj(hs, w_in_b, col_hgrn, 4 * d_hgrn, n_dec, 1024)
    o_s, hgrn_s = _hgrn_sample(p_s.reshape(n_dec, 4 * heads, dk), lbl.reshape(2, heads, dk),
                               onorm_g[0], state_hgrn[0], 4)
    mix_s = _merge(hs, a_s, o_s.reshape(n_dec, d_hgrn), w_in_b, w_a_b, w_b_b, col_ga, n_dec, 1024)
    x1_s, h2_s = _outproj(xs, mix_s, w_out_b, norm_ffn[0], n_dec)
    y_s = _mlp(h2_s, x1_s, w_up_b, w_dn_b, norm_final, n_dec, 1024)

    conv_s = jnp.stack([state_conv[0, :, 1, :], u_s], axis=1)
    return (y_p.reshape(batch, seq, d), y_s.reshape(n_dec, 1, d),
            conv_p[None], hgrn_p[None], conv_s[None], hgrn_s[None])
```

```python
import functools

import jax
import jax.numpy as jnp
from jax import lax
from jax.experimental import pallas as pl
from jax.experimental.pallas import tpu as pltpu

EPS = 1e-6
CHUNK = 64
HEAD_DIM = 128
V7X_VMEM_BYTES = 64 * 1024 * 1024
VMEM_LIMIT = V7X_VMEM_BYTES * 7 // 8

ROW_BLOCK = 512
MLP_ROW_BLOCK = 1024
MLP_FF_BLOCK = 512
CONV_COL_BLOCK = 512
PROJ_COL_BLOCK = 1024
MERGE_COL_BLOCK = 512
HGRN_ROWS = 256
HGRN_HEADS = 4
HGRN_DECODE_SEQS = 4

BF16 = jnp.bfloat16
F32 = jnp.float32


def _params(semantics):
    return pltpu.CompilerParams(dimension_semantics=semantics, vmem_limit_bytes=VMEM_LIMIT)


def _dot(a, b):
    return jnp.dot(a, b, preferred_element_type=F32)


def _dot_nt(a, b):
    return lax.dot_general(a, b, (((1,), (1,)), ((), ())), preferred_element_type=F32)


def _dot_tn(a, b):
    return lax.dot_general(a, b, (((0,), (0,)), ((), ())), preferred_element_type=F32)


def _sigmoid(x):
    return jax.nn.sigmoid(x)


def _silu(x):
    return x * jax.nn.sigmoid(x)


def _rms_scale(x, g):
    ms = jnp.mean(x * x, axis=-1, keepdims=True)
    return x * lax.rsqrt(ms + EPS) * g


def _row_split(i, n_full, full_rows, tail_rows, body):
    @pl.when(i < n_full)
    def _():
        body(full_rows, False)

    @pl.when(i == n_full)
    def _():
        body(tail_rows, True)


def _rmsnorm_kernel(xp_ref, xs_ref, g_ref, o_ref, *, n_full):
    tm, tail = xp_ref.shape[0], xs_ref.shape[0]

    def body(rows, is_tail):
        x = xs_ref[...] if is_tail else xp_ref[...]
        o_ref[0:rows, :] = _rms_scale(x, g_ref[...]).astype(o_ref.dtype)

    _row_split(pl.program_id(0), n_full, tm, tail, body)


def _rmsnorm(xp, xs, g, tm):
    mp, d = xp.shape
    tail = xs.shape[0]
    n_full = mp // tm
    return pl.pallas_call(
        functools.partial(_rmsnorm_kernel, n_full=n_full),
        grid=(n_full + 1,),
        in_specs=[pl.BlockSpec((tm, d), lambda i: (jnp.minimum(i, n_full - 1), 0)),
                  pl.BlockSpec((tail, d), lambda i: (0, 0)),
                  pl.BlockSpec((1, d), lambda i: (0, 0))],
        out_specs=pl.BlockSpec((tm, d), lambda i: (i, 0)),
        out_shape=jax.ShapeDtypeStruct((mp + tail, d), BF16),
        compiler_params=_params(("arbitrary",)),
        name="rmsnorm",
    )(xp, xs, g.reshape(1, d))


def _conv_prompt_kernel(hx_ref, whc_ref, wbg_ref, wcg_ref, cw_ref, a_ref, nc_ref, wb_ref, carry_ref):
    b, t = pl.program_id(1), pl.program_id(2)
    tm = hx_ref.shape[0]

    @pl.when((b == 0) & (t == 0))
    def _():
        wb_ref[0] = whc_ref[...].astype(BF16)
        wb_ref[1] = wbg_ref[...].astype(BF16)
        wb_ref[2] = wcg_ref[...].astype(BF16)

    @pl.when(t == 0)
    def _():
        carry_ref[...] = jnp.zeros_like(carry_ref)

    hx = hx_ref[...]
    hc = _dot(hx, wb_ref[0])
    bg = _dot(hx, wb_ref[1])
    cg = _dot(hx, wb_ref[2])
    u = cg * hc
    c0 = carry_ref[0:1, :]
    c1 = carry_ref[1:2, :]
    row = lax.broadcasted_iota(jnp.int32, u.shape, 0)
    u1 = jnp.where(row == 0, c1, pltpu.roll(u, 1, 0))
    u2 = jnp.where(row == 0, c0, jnp.where(row == 1, c1, pltpu.roll(u, 2, 0)))
    cw = cw_ref[...]
    conv = cw[0:1, :] * u2 + cw[1:2, :] * u1 + cw[2:3, :] * u
    a_ref[...] = (bg * conv).astype(a_ref.dtype)
    tail = u[tm - 2:tm, :]
    carry_ref[0:2, :] = tail

    @pl.when(t == pl.num_programs(2) - 1)
    def _():
        nc_ref[0] = tail


def _conv_prompt(hx, w_in, conv_w, batch, seq, d_conv, tm, tc):
    d = hx.shape[1]
    n_c = d_conv // tc
    n_t = seq // tm
    return pl.pallas_call(
        _conv_prompt_kernel,
        grid=(n_c, batch, n_t),
        in_specs=[pl.BlockSpec((tm, d), lambda c, b, t: (b * n_t + t, 0)),
                  pl.BlockSpec((d, tc), lambda c, b, t: (0, c)),
                  pl.BlockSpec((d, tc), lambda c, b, t: (0, n_c + c)),
                  pl.BlockSpec((d, tc), lambda c, b, t: (0, 2 * n_c + c)),
                  pl.BlockSpec((3, tc), lambda c, b, t: (0, c))],
        out_specs=[pl.BlockSpec((tm, tc), lambda c, b, t: (b * n_t + t, c)),
                   pl.BlockSpec((1, 2, tc), lambda c, b, t: (b, 0, c))],
        out_shape=[jax.ShapeDtypeStruct((batch * seq, d_conv), BF16),
                   jax.ShapeDtypeStruct((batch, 2, d_conv), F32)],
        scratch_shapes=[pltpu.VMEM((3, d, tc), BF16), pltpu.VMEM((8, tc), F32)],
        compiler_params=_params(("arbitrary", "arbitrary", "arbitrary")),
        name="conv_prompt",
    )(hx, w_in, w_in, w_in, conv_w)


def _conv_sample_kernel(hx_ref, whc_ref, wbg_ref, wcg_ref, cw_ref, s0_ref, s1_ref, a_ref, u_ref):
    hx = hx_ref[...]
    hc = _dot(hx, whc_ref[...].astype(BF16))
    bg = _dot(hx, wbg_ref[...].astype(BF16))
    cg = _dot(hx, wcg_ref[...].astype(BF16))
    u = cg * hc
    cw = cw_ref[...]
    conv = cw[0:1, :] * s0_ref[...] + cw[1:2, :] * s1_ref[...] + cw[2:3, :] * u
    a_ref[...] = (bg * conv).astype(a_ref.dtype)
    u_ref[...] = u


def _conv_sample(hx, row_block, n, w_in, conv_w, state2d, d_conv, tc):
    d = hx.shape[1]
    n_c = d_conv // tc
    return pl.pallas_call(
        _conv_sample_kernel,
        grid=(n_c,),
        in_specs=[pl.BlockSpec((n, d), lambda c: (row_block, 0)),
                  pl.BlockSpec((d, tc), lambda c: (0, c)),
                  pl.BlockSpec((d, tc), lambda c: (0, n_c + c)),
                  pl.BlockSpec((d, tc), lambda c: (0, 2 * n_c + c)),
                  pl.BlockSpec((3, tc), lambda c: (0, c)),
                  pl.BlockSpec((n, tc), lambda c: (0, c)),
                  pl.BlockSpec((n, tc), lambda c: (0, n_c + c))],
        out_specs=[pl.BlockSpec((n, tc), lambda c: (0, c)),
                   pl.BlockSpec((n, tc), lambda c: (0, c))],
        out_shape=[jax.ShapeDtypeStruct((n, d_conv), BF16),
                   jax.ShapeDtypeStruct((n, d_conv), F32)],
        compiler_params=_params(("arbitrary",)),
        name="conv_sample",
    )(hx, w_in, w_in, w_in, conv_w, state2d, state2d)


def _proj_kernel(x_ref, w_ref, o_ref, wb_ref, *, n_full, tail):
    i = pl.program_id(1)

    @pl.when(i == 0)
    def _():
        wb_ref[...] = w_ref[...].astype(BF16)

    def body(rows, is_tail):
        o_ref[0:rows, :] = _dot(x_ref[0:rows, :], wb_ref[...])

    _row_split(i, n_full, x_ref.shape[0], tail, body)


def _proj(x, w, col0, n, tm, tn, tail):
    m, d = x.shape
    n_full = (m - tail) // tm
    return pl.pallas_call(
        functools.partial(_proj_kernel, n_full=n_full, tail=tail),
        grid=(n // tn, n_full + 1),
        in_specs=[pl.BlockSpec((tm, d), lambda j, i: (i, 0)),
                  pl.BlockSpec((d, tn), lambda j, i: (0, col0 // tn + j))],
        out_specs=pl.BlockSpec((tm, tn), lambda j, i: (i, j)),
        out_shape=jax.ShapeDtypeStruct((m, n), F32),
        scratch_shapes=[pltpu.VMEM((d, tn), BF16)],
        compiler_params=_params(("arbitrary", "arbitrary")),
        name="hgrn_proj",
    )(x, w)


def _lower_bound(lbl):
    e = jnp.exp(lbl - jnp.max(lbl, axis=0))
    return e[0] / jnp.sum(e, axis=0)


def _head_norm_gate(o, g, og):
    return _rms_scale(o, g) * _silu(og)


def _cumsum_rows(x, tril_bf16):
    hi = x.astype(BF16)
    r1 = x - hi.astype(F32)
    mid = r1.astype(BF16)
    lo = (r1 - mid.astype(F32)).astype(BF16)
    n = x.shape[1]
    parts = _dot(tril_bf16, jnp.concatenate([hi, mid, lo], axis=1))
    return parts[:, 0:n] + parts[:, n:2 * n] + parts[:, 2 * n:3 * n]


def _hgrn_prompt_kernel(q_ref, fz_ref, iv_ref, og_ref, lbl_ref, g_ref, o_ref, s_ref, st_ref):
    t = pl.program_id(2)
    tb = q_ref.shape[0]
    heads_per_step = st_ref.shape[0]
    dk = HEAD_DIM
    n_chunks = tb // CHUNK

    @pl.when(t == 0)
    def _():
        st_ref[...] = jnp.zeros_like(st_ref)

    lb_all = _lower_bound(lbl_ref[...])
    g = g_ref[...]
    row = lax.broadcasted_iota(jnp.int32, (tb, tb), 0)
    col = lax.broadcasted_iota(jnp.int32, (tb, tb), 1)
    shift = CHUNK.bit_length() - 1
    causal = (row >= col) & (jnp.right_shift(row, shift) == jnp.right_shift(col, shift))
    tril = causal.astype(BF16)

    for hh in range(heads_per_step):
        cs = slice(hh * dk, (hh + 1) * dk)
        lb = lb_all[:, cs]
        qs = _silu(q_ref[:, cs])
        f = lb + (1.0 - lb) * _sigmoid(fz_ref[:, cs])
        kk = 1.0 - f
        v = iv_ref[:, cs].astype(BF16)
        b = _cumsum_rows(jnp.log(f), tril)
        b3 = b.reshape(n_chunks, CHUNK, dk)
        b_last = b3[:, CHUNK - 1:CHUNK, :]
        b_end = jnp.broadcast_to(b_last, b3.shape).reshape(tb, dk)
        q_in = (qs * jnp.exp(b)).astype(BF16)
        k_in = (kk * jnp.exp(-b)).astype(BF16)
        k_end = (kk * jnp.exp(b_end - b)).astype(BF16)
        scores = jnp.where(causal, _dot_nt(q_in, k_in), 0.0)
        o_intra = _dot(scores.astype(BF16), v)
        st = st_ref[hh]
        o_inter = []
        for c in range(n_chunks):
            rs = slice(c * CHUNK, (c + 1) * CHUNK)
            o_inter.append(_dot_nt(q_in[rs], st.astype(BF16)))
            st = jnp.exp(b_last[c]) * st + _dot_tn(v[rs], k_end[rs])
        st_ref[hh] = st
        o = o_intra + jnp.concatenate(o_inter, axis=0)
        o_ref[:, cs] = _head_norm_gate(o, g, og_ref[:, cs]).astype(o_ref.dtype)

    @pl.when(t == pl.num_programs(2) - 1)
    def _():
        for hh in range(heads_per_step):
            s_ref[0, hh] = st_ref[hh].T


def _hgrn_prompt(p, lb_logits, onorm_g, batch, seq, heads, tb, hp):
    n_t = seq // tb
    n_h = heads // hp
    dk = HEAD_DIM
    assert CHUNK & (CHUNK - 1) == 0

    def col(k):
        return pl.BlockSpec((tb, hp * dk), lambda b, h, t: (b * n_t + t, k * n_h + h))

    return pl.pallas_call(
        _hgrn_prompt_kernel,
        grid=(batch, n_h, n_t),
        in_specs=[col(0), col(1), col(2), col(3),
                  pl.BlockSpec((2, 1, hp * dk), lambda b, h, t: (0, 0, h)),
                  pl.BlockSpec((1, dk), lambda b, h, t: (0, 0))],
        out_specs=[pl.BlockSpec((tb, hp * dk), lambda b, h, t: (b * n_t + t, h)),
                   pl.BlockSpec((1, hp, dk, dk), lambda b, h, t: (b, h, 0, 0))],
        out_shape=[jax.ShapeDtypeStruct((batch * seq, heads * dk), BF16),
                   jax.ShapeDtypeStruct((batch, heads, dk, dk), F32)],
        scratch_shapes=[pltpu.VMEM((hp, dk, dk), F32)],
        compiler_params=_params(("arbitrary", "arbitrary", "arbitrary")),
        name="hgrn_prompt",
    )(p, p, p, p, lb_logits.reshape(2, 1, heads * dk), onorm_g.reshape(1, dk))


def _hgrn_sample_kernel(p_ref, lbl_ref, g_ref, s0_ref, o_ref, sn_ref):
    bb, heads, dk = s0_ref.shape[0], s0_ref.shape[1], s0_ref.shape[2]
    lb = _lower_bound(lbl_ref[...])
    g = g_ref[...]
    rows, per_seq = [], []
    for i in range(bb):
        p = p_ref[i]
        qs = _silu(p[0:heads])
        f = lb + (1.0 - lb) * _sigmoid(p[heads:2 * heads])
        kk = 1.0 - f
        rows += [f, kk, qs * f]
        per_seq.append((qs, kk, p[2 * heads:3 * heads], p[3 * heads:4 * heads]))
    pad = dk - 3 * heads * bb
    cols = jnp.concatenate(rows + [jnp.zeros((pad, dk), F32)], axis=0).T
    for i in range(bb):
        qs, kk, v, og = per_seq[i]
        o_inter = []
        for h in range(heads):
            c0 = 3 * heads * i + h
            f_c = cols[:, c0:c0 + 1]
            k_c = cols[:, c0 + heads:c0 + heads + 1]
            q_c = cols[:, c0 + 2 * heads:c0 + 2 * heads + 1]
            s = s0_ref[i, h]
            sn_ref[i, h] = f_c * s + k_c * v[h:h + 1, :]
            o_inter.append(jnp.sum(q_c * s, axis=0, keepdims=True))
        o = jnp.sum(qs * kk, axis=-1, keepdims=True) * v + jnp.concatenate(o_inter, axis=0)
        o_ref[i] = _head_norm_gate(o, g, og)


def _hgrn_sample(p3, lbl3, onorm_g, state, bb):
    n, rows, dk = p3.shape
    heads = rows // 4
    assert n % bb == 0 and 3 * heads * bb <= dk
    return pl.pallas_call(
        _hgrn_sample_kernel,
        grid=(n // bb,),
        in_specs=[pl.BlockSpec((bb, rows, dk), lambda b: (b, 0, 0)),
                  pl.BlockSpec((2, heads, dk), lambda b: (0, 0, 0)),
                  pl.BlockSpec((1, dk), lambda b: (0, 0)),
                  pl.BlockSpec((bb, heads, dk, dk), lambda b: (b, 0, 0, 0))],
        out_specs=[pl.BlockSpec((bb, heads, dk), lambda b: (b, 0, 0)),
                   pl.BlockSpec((bb, heads, dk, dk), lambda b: (b, 0, 0, 0))],
        out_shape=[jax.ShapeDtypeStruct((n, heads, dk), F32),
                   jax.ShapeDtypeStruct((n, heads, dk, dk), F32)],
        compiler_params=_params(("parallel",)),
        name="hgrn_sample",
    )(p3, lbl3, onorm_g.reshape(1, dk), state)


def _merge_kernel(hx_ref, a_ref, o_ref, at_ref, ot_ref, wga_ref, wgb_ref, wa_ref, wb_ref, mix_ref,
                  wgab_ref, wabb_ref, *, n_full):
    i = pl.program_id(1)

    @pl.when(i == 0)
    def _():
        wgab_ref[0] = wga_ref[...].astype(BF16)
        wgab_ref[1] = wgb_ref[...].astype(BF16)
        wabb_ref[0] = wa_ref[...].astype(BF16)
        wabb_ref[1] = wb_ref[...].astype(BF16)

    def body(rows, is_tail):
        hx = hx_ref[0:rows, :]
        a = at_ref[...] if is_tail else a_ref[...]
        o = ot_ref[...].astype(BF16) if is_tail else o_ref[...]
        ga = _sigmoid(_dot(hx, wgab_ref[0]))
        gb = _sigmoid(_dot(hx, wgab_ref[1]))
        mix = ga * _dot(a, wabb_ref[0]) + gb * _dot(o, wabb_ref[1])
        mix_ref[0:rows, :] = mix.astype(mix_ref.dtype)

    _row_split(i, n_full, hx_ref.shape[0], at_ref.shape[0], body)


def _merge(hx, a, o, a_tail, o_tail, w_in, w_a, w_b, col_ga, tm, tn):
    m, d = hx.shape
    dc, dh = a.shape[1], o.shape[1]
    tail = a_tail.shape[0]
    n_full = a.shape[0] // tm
    n_n = d // tn
    clamp = lambda j, i: (jnp.minimum(i, n_full - 1), 0)
    return pl.pallas_call(
        functools.partial(_merge_kernel, n_full=n_full),
        grid=(n_n, n_full + 1),
        in_specs=[pl.BlockSpec((tm, d), lambda j, i: (i, 0)),
                  pl.BlockSpec((tm, dc), clamp),
                  pl.BlockSpec((tm, dh), clamp),
                  pl.BlockSpec((tail, dc), lambda j, i: (0, 0)),
                  pl.BlockSpec((tail, dh), lambda j, i: (0, 0)),
                  pl.BlockSpec((d, tn), lambda j, i: (0, col_ga // tn + j)),
                  pl.BlockSpec((d, tn), lambda j, i: (0, col_ga // tn + n_n + j)),
                  pl.BlockSpec((dc, tn), lambda j, i: (0, j)),
                  pl.BlockSpec((dh, tn), lambda j, i: (0, j))],
        out_specs=pl.BlockSpec((tm, tn), lambda j, i: (i, j)),
        out_shape=jax.ShapeDtypeStruct((m, d), BF16),
        scratch_shapes=[pltpu.VMEM((2, d, tn), BF16), pltpu.VMEM((2, dc, tn), BF16)],
        compiler_params=_params(("arbitrary", "arbitrary")),
        name="merge",
    )(hx, a, o, a_tail, o_tail, w_in, w_in, w_a, w_b)


def _outproj_kernel(xp_ref, xs_ref, mix_ref, w_ref, g_ref, x1_ref, h2_ref, wb_ref, *, n_full):
    i = pl.program_id(0)

    @pl.when(i == 0)
    def _():
        wb_ref[...] = w_ref[...].astype(BF16)

    def body(rows, is_tail):
        x = xs_ref[...] if is_tail else xp_ref[...]
        x1 = x + _dot(mix_ref[0:rows, :], wb_ref[...])
        x1_ref[0:rows, :] = x1
        h2_ref[0:rows, :] = _rms_scale(x1, g_ref[...]).astype(h2_ref.dtype)

    _row_split(i, n_full, xp_ref.shape[0], xs_ref.shape[0], body)


def _outproj(xp, xs, mix, w_out, g, tm):
    mp, d = xp.shape
    tail = xs.shape[0]
    n_full = mp // tm
    return pl.pallas_call(
        functools.partial(_outproj_kernel, n_full=n_full),
        grid=(n_full + 1,),
        in_specs=[pl.BlockSpec((tm, d), lambda i: (jnp.minimum(i, n_full - 1), 0)),
                  pl.BlockSpec((tail, d), lambda i: (0, 0)),
                  pl.BlockSpec((tm, d), lambda i: (i, 0)),
                  pl.BlockSpec((d, d), lambda i: (0, 0), pipeline_mode=pl.Buffered(1)),
                  pl.BlockSpec((1, d), lambda i: (0, 0))],
        out_specs=[pl.BlockSpec((tm, d), lambda i: (i, 0)),
                   pl.BlockSpec((tm, d), lambda i: (i, 0))],
        out_shape=[jax.ShapeDtypeStruct((mp + tail, d), F32),
                   jax.ShapeDtypeStruct((mp + tail, d), BF16)],
        scratch_shapes=[pltpu.VMEM((d, d), BF16)],
        compiler_params=_params(("arbitrary",)),
        name="outproj",
    )(xp, xs, mix, w_out, g.reshape(1, d))


def _mlp_kernel(h2_ref, x1_ref, wup_ref, wdn_ref, g_ref, yp_ref, ys_ref, *, n_full):
    i, j = pl.program_id(0), pl.program_id(1)
    last = pl.num_programs(1) - 1

    def body(rows, is_tail):
        y_ref = ys_ref if is_tail else yp_ref

        @pl.when(j == 0)
        def _():
            y_ref[...] = x1_ref[0:rows, :]

        h = jnp.maximum(_dot(h2_ref[0:rows, :], wup_ref[...].astype(BF16)), 0.0)
        y_ref[...] += _dot((h * h).astype(BF16), wdn_ref[...].astype(BF16))

        @pl.when(j == last)
        def _():
            y_ref[...] = _rms_scale(y_ref[...], g_ref[...])

    _row_split(i, n_full, yp_ref.shape[0], ys_ref.shape[0], body)


def _mlp(h2, x1, w_up, w_down, g, tm, tf, tail):
    m, d = h2.shape
    dff = w_up.shape[1]
    n_full = (m - tail) // tm
    once = pl.Buffered(1)
    return pl.pallas_call(
        functools.partial(_mlp_kernel, n_full=n_full),
        grid=(n_full + 1, dff // tf),
        in_specs=[pl.BlockSpec((tm, d), lambda i, j: (i, 0), pipeline_mode=once),
                  pl.BlockSpec((tm, d), lambda i, j: (i, 0), pipeline_mode=once),
                  pl.BlockSpec((d, tf), lambda i, j: (0, j)),
                  pl.BlockSpec((tf, d), lambda i, j: (j, 0)),
                  pl.BlockSpec((1, d), lambda i, j: (0, 0))],
        out_specs=[pl.BlockSpec((tm, d), lambda i, j: (jnp.minimum(i, n_full - 1), 0)),
                   pl.BlockSpec((tail, d), lambda i, j: (0, 0))],
        out_shape=[jax.ShapeDtypeStruct((m - tail, d), F32),
                   jax.ShapeDtypeStruct((tail, d), F32)],
        compiler_params=_params(("arbitrary", "arbitrary")),
        name="mlp",
    )(h2, x1, w_up, w_down, g.reshape(1, d))


def kernel(x_prompt, x_sample, state_conv, state_hgrn, norm_mix, w_in, conv_w, lb_logits, onorm_g,
           w_branch_a, w_branch_b, w_out, norm_ffn, w_up, w_down, norm_final):
    batch, seq, d = x_prompt.shape
    n_dec = x_sample.shape[0]
    depth, _, d_conv = conv_w.shape
    heads, dk = state_hgrn.shape[2], state_hgrn.shape[3]
    d_hgrn = heads * dk
    mp = batch * seq
    assert depth == 1 and x_sample.shape[1] == 1 and dk == HEAD_DIM and state_hgrn.shape[4] == dk
    assert seq % ROW_BLOCK == 0 and mp % MLP_ROW_BLOCK == 0 and ROW_BLOCK % n_dec == 0
    col_hgrn = 3 * d_conv
    col_ga = col_hgrn + 4 * d_hgrn
    w_in0 = w_in.reshape(w_in.shape[1:])
    w_a0 = w_branch_a.reshape(w_branch_a.shape[1:])
    w_b0 = w_branch_b.reshape(w_branch_b.shape[1:])
    w_out0 = w_out.reshape(w_out.shape[1:])
    w_up0 = w_up.reshape(w_up.shape[1:])
    w_down0 = w_down.reshape(w_down.shape[1:])

    xp = x_prompt.reshape(mp, d)
    xs = x_sample.reshape(n_dec, d)
    hx = _rmsnorm(xp, xs, norm_mix[0], ROW_BLOCK)

    a_p, conv_p = _conv_prompt(hx, w_in0, conv_w[0], batch, seq, d_conv, ROW_BLOCK, CONV_COL_BLOCK)
    st2 = state_conv[0].reshape(n_dec, 2 * d_conv)
    a_s, u_s = _conv_sample(hx, mp // n_dec, n_dec, w_in0, conv_w[0], st2, d_conv, CONV_COL_BLOCK)

    p = _proj(hx, w_in0, col_hgrn, 4 * d_hgrn, ROW_BLOCK, PROJ_COL_BLOCK, n_dec)
    o_p, hgrn_p = _hgrn_prompt(p, lb_logits, onorm_g[0], batch, seq, heads, HGRN_ROWS, HGRN_HEADS)
    o_s, hgrn_s = _hgrn_sample(p[mp:].reshape(n_dec, 4 * heads, dk), lb_logits.reshape(2, heads, dk),
                               onorm_g[0], state_hgrn[0], HGRN_DECODE_SEQS)

    mix = _merge(hx, a_p, o_p, a_s, o_s.reshape(n_dec, d_hgrn), w_in0, w_a0, w_b0,
                 col_ga, ROW_BLOCK, MERGE_COL_BLOCK)
    x1, h2 = _outproj(xp, xs, mix, w_out0, norm_ffn[0], ROW_BLOCK)
    y_p, y_s = _mlp(h2, x1, w_up0, w_down0, norm_final, MLP_ROW_BLOCK, MLP_FF_BLOCK, n_dec)

    conv_s = jnp.stack([state_conv[0, :, 1, :], u_s], axis=1)
    return (y_p.reshape(batch, seq, d), y_s.reshape(n_dec, 1, d),
            conv_p[None], hgrn_p[None], conv_s[None], hgrn_s[None])
```

```python
import functools

import jax
import jax.numpy as jnp
from jax import lax
from jax.experimental import pallas as pl
from jax.experimental.pallas import tpu as pltpu

EPS = 1e-6
CHUNK = 64
HEAD_DIM = 128
V7X_VMEM_BYTES = 64 * 1024 * 1024
VMEM_LIMIT = V7X_VMEM_BYTES * 7 // 8

ROW_BLOCK = 512
MLP_ROW_BLOCK = 1024
MLP_FF_BLOCK = 512
CONV_COL_BLOCK = 512
PROJ_COL_BLOCK = 1024
MERGE_COL_BLOCK = 1024
SIDE_SLABS = 64
HGRN_ROWS = 256
HGRN_HEADS = 4
HGRN_DECODE_SEQS = 4

BF16 = jnp.bfloat16
F32 = jnp.float32


def _params(semantics):
    return pltpu.CompilerParams(dimension_semantics=semantics, vmem_limit_bytes=VMEM_LIMIT)


def _dot(a, b):
    return jnp.dot(a, b, preferred_element_type=F32)


def _dot_nt(a, b):
    return lax.dot_general(a, b, (((1,), (1,)), ((), ())), preferred_element_type=F32)


def _dot_tn(a, b):
    return lax.dot_general(a, b, (((0,), (0,)), ((), ())), preferred_element_type=F32)


def _sigmoid(x):
    return jax.nn.sigmoid(x)


def _silu(x):
    return x * jax.nn.sigmoid(x)


def _rms_scale(x, g):
    ms = jnp.mean(x * x, axis=-1, keepdims=True)
    return x * lax.rsqrt(ms + EPS) * g


def _row_split(i, n_full, full_rows, tail_rows, body):
    @pl.when(i < n_full)
    def _():
        body(full_rows, False)

    @pl.when(i == n_full)
    def _():
        body(tail_rows, True)


def _slab_cast(step, n_slabs, src_ref, dst_ref):
    @pl.when(step < n_slabs)
    def _():
        dst_ref[...] = src_ref[...].astype(dst_ref.dtype)


def _slab_specs(w, n_slabs, step_of):
    r, c = w.shape
    assert r % n_slabs == 0

    def index(*g):
        return (jnp.minimum(step_of(*g), n_slabs - 1), 0)

    spec = pl.BlockSpec((r // n_slabs, c), index)
    return spec, spec, jax.ShapeDtypeStruct((r, c), BF16)


def _rmsnorm_kernel(xp_ref, xs_ref, g_ref, o_ref, *, n_full):
    tm, tail = xp_ref.shape[0], xs_ref.shape[0]

    def body(rows, is_tail):
        x = xs_ref[...] if is_tail else xp_ref[...]
        o_ref[0:rows, :] = _rms_scale(x, g_ref[...]).astype(o_ref.dtype)

    _row_split(pl.program_id(0), n_full, tm, tail, body)


def _rmsnorm(xp, xs, g, tm):
    mp, d = xp.shape
    tail = xs.shape[0]
    n_full = mp // tm
    return pl.pallas_call(
        functools.partial(_rmsnorm_kernel, n_full=n_full),
        grid=(n_full + 1,),
        in_specs=[pl.BlockSpec((tm, d), lambda i: (jnp.minimum(i, n_full - 1), 0)),
                  pl.BlockSpec((tail, d), lambda i: (0, 0)),
                  pl.BlockSpec((1, d), lambda i: (0, 0))],
        out_specs=pl.BlockSpec((tm, d), lambda i: (i, 0)),
        out_shape=jax.ShapeDtypeStruct((mp + tail, d), BF16),
        compiler_params=_params(("arbitrary",)),
        name="rmsnorm",
    )(xp, xs, g.reshape(1, d))


def _conv_prompt_kernel(hx_ref, whc_ref, wbg_ref, wcg_ref, cw_ref, side_ref, a_ref, nc_ref, side_out_ref,
                        wb_ref, carry_ref, *, n_slabs):
    c, b, t = pl.program_id(0), pl.program_id(1), pl.program_id(2)
    tm = hx_ref.shape[0]
    _slab_cast((c * pl.num_programs(1) + b) * pl.num_programs(2) + t, n_slabs, side_ref, side_out_ref)

    @pl.when((b == 0) & (t == 0))
    def _():
        wb_ref[0] = whc_ref[...].astype(BF16)
        wb_ref[1] = wbg_ref[...].astype(BF16)
        wb_ref[2] = wcg_ref[...].astype(BF16)

    @pl.when(t == 0)
    def _():
        carry_ref[...] = jnp.zeros_like(carry_ref)

    hx = hx_ref[...]
    hc = _dot(hx, wb_ref[0])
    bg = _dot(hx, wb_ref[1])
    cg = _dot(hx, wb_ref[2])
    u = cg * hc
    c0 = carry_ref[0:1, :]
    c1 = carry_ref[1:2, :]
    row = lax.broadcasted_iota(jnp.int32, u.shape, 0)
    u1 = jnp.where(row == 0, c1, pltpu.roll(u, 1, 0))
    u2 = jnp.where(row == 0, c0, jnp.where(row == 1, c1, pltpu.roll(u, 2, 0)))
    cw = cw_ref[...]
    conv = cw[0:1, :] * u2 + cw[1:2, :] * u1 + cw[2:3, :] * u
    a_ref[...] = (bg * conv).astype(a_ref.dtype)
    tail = u[tm - 2:tm, :]
    carry_ref[0:2, :] = tail

    @pl.when(t == pl.num_programs(2) - 1)
    def _():
        nc_ref[0] = tail


def _conv_prompt(hx, w_in, conv_w, batch, seq, d_conv, tm, tc):
    d = hx.shape[1]
    n_c = d_conv // tc
    n_t = seq // tm
    n_slabs = n_c * batch * n_t
    side_in, side_out, side_shape = _slab_specs(w_in, n_slabs, lambda c, b, t: (c * batch + b) * n_t + t)
    return pl.pallas_call(
        functools.partial(_conv_prompt_kernel, n_slabs=n_slabs),
        grid=(n_c, batch, n_t),
        in_specs=[pl.BlockSpec((tm, d), lambda c, b, t: (b * n_t + t, 0)),
                  pl.BlockSpec((d, tc), lambda c, b, t: (0, c)),
                  pl.BlockSpec((d, tc), lambda c, b, t: (0, n_c + c)),
                  pl.BlockSpec((d, tc), lambda c, b, t: (0, 2 * n_c + c)),
                  pl.BlockSpec((3, tc), lambda c, b, t: (0, c)),
                  side_in],
        out_specs=[pl.BlockSpec((tm, tc), lambda c, b, t: (b * n_t + t, c)),
                   pl.BlockSpec((1, 2, tc), lambda c, b, t: (b, 0, c)),
                   side_out],
        out_shape=[jax.ShapeDtypeStruct((batch * seq, d_conv), BF16),
                   jax.ShapeDtypeStruct((batch, 2, d_conv), F32),
                   side_shape],
        scratch_shapes=[pltpu.VMEM((3, d, tc), BF16), pltpu.VMEM((8, tc), F32)],
        compiler_params=_params(("arbitrary", "arbitrary", "arbitrary")),
        name="conv_prompt",
    )(hx, w_in, w_in, w_in, conv_w, w_in)


def _conv_sample_kernel(hx_ref, whc_ref, wbg_ref, wcg_ref, cw_ref, s0_ref, s1_ref, a_ref, u_ref):
    hx = hx_ref[...]
    hc = _dot(hx, whc_ref[...])
    bg = _dot(hx, wbg_ref[...])
    cg = _dot(hx, wcg_ref[...])
    u = cg * hc
    cw = cw_ref[...]
    conv = cw[0:1, :] * s0_ref[...] + cw[1:2, :] * s1_ref[...] + cw[2:3, :] * u
    a_ref[...] = (bg * conv).astype(a_ref.dtype)
    u_ref[...] = u


def _conv_sample(hx, row_block, n, w_in, conv_w, state2d, d_conv, tc):
    d = hx.shape[1]
    n_c = d_conv // tc
    return pl.pallas_call(
        _conv_sample_kernel,
        grid=(n_c,),
        in_specs=[pl.BlockSpec((n, d), lambda c: (row_block, 0)),
                  pl.BlockSpec((d, tc), lambda c: (0, c)),
                  pl.BlockSpec((d, tc), lambda c: (0, n_c + c)),
                  pl.BlockSpec((d, tc), lambda c: (0, 2 * n_c + c)),
                  pl.BlockSpec((3, tc), lambda c: (0, c)),
                  pl.BlockSpec((n, tc), lambda c: (0, c)),
                  pl.BlockSpec((n, tc), lambda c: (0, n_c + c))],
        out_specs=[pl.BlockSpec((n, tc), lambda c: (0, c)),
                   pl.BlockSpec((n, tc), lambda c: (0, c))],
        out_shape=[jax.ShapeDtypeStruct((n, d_conv), BF16),
                   jax.ShapeDtypeStruct((n, d_conv), F32)],
        compiler_params=_params(("arbitrary",)),
        name="conv_sample",
    )(hx, w_in, w_in, w_in, conv_w, state2d, state2d)


def _proj_kernel(x_ref, w_ref, side_ref, o_ref, side_out_ref, *, n_full, tail, n_slabs):
    j, i = pl.program_id(0), pl.program_id(1)
    _slab_cast(j * pl.num_programs(1) + i, n_slabs, side_ref, side_out_ref)

    def body(rows, is_tail):
        o_ref[0:rows, :] = _dot(x_ref[0:rows, :], w_ref[...])

    _row_split(i, n_full, x_ref.shape[0], tail, body)


def _proj(x, w, col0, n, tm, tn, tail, side_w, n_slabs):
    m, d = x.shape
    n_full = (m - tail) // tm
    assert n_slabs <= (n // tn) * (n_full + 1)
    side_in, side_out, side_shape = _slab_specs(side_w, n_slabs, lambda j, i: j * (n_full + 1) + i)
    return pl.pallas_call(
        functools.partial(_proj_kernel, n_full=n_full, tail=tail, n_slabs=n_slabs),
        grid=(n // tn, n_full + 1),
        in_specs=[pl.BlockSpec((tm, d), lambda j, i: (i, 0)),
                  pl.BlockSpec((d, tn), lambda j, i: (0, col0 // tn + j)),
                  side_in],
        out_specs=[pl.BlockSpec((tm, tn), lambda j, i: (i, j)), side_out],
        out_shape=[jax.ShapeDtypeStruct((m, n), F32), side_shape],
        compiler_params=_params(("arbitrary", "arbitrary")),
        name="hgrn_proj",
    )(x, w, side_w)


def _lower_bound(lbl):
    e = jnp.exp(lbl - jnp.max(lbl, axis=0))
    return e[0] / jnp.sum(e, axis=0)


def _head_norm_gate(o, g, og):
    return _rms_scale(o, g) * _silu(og)


def _cumsum_rows(x, tril_bf16):
    hi = x.astype(BF16)
    r1 = x - hi.astype(F32)
    mid = r1.astype(BF16)
    lo = (r1 - mid.astype(F32)).astype(BF16)
    n = x.shape[1]
    parts = _dot(tril_bf16, jnp.concatenate([hi, mid, lo], axis=1))
    return parts[:, 0:n] + parts[:, n:2 * n] + parts[:, 2 * n:3 * n]


def _hgrn_prompt_kernel(q_ref, fz_ref, iv_ref, og_ref, lbl_ref, g_ref, side_ref, o_ref, s_ref, side_out_ref,
                        st_ref, *, n_slabs):
    t = pl.program_id(2)
    step = (pl.program_id(0) * pl.num_programs(1) + pl.program_id(1)) * pl.num_programs(2) + t
    _slab_cast(step, n_slabs, side_ref, side_out_ref)
    tb = q_ref.shape[0]
    heads_per_step = st_ref.shape[0]
    dk = HEAD_DIM
    n_chunks = tb // CHUNK

    @pl.when(t == 0)
    def _():
        st_ref[...] = jnp.zeros_like(st_ref)

    lb_all = _lower_bound(lbl_ref[...])
    g = g_ref[...]
    row = lax.broadcasted_iota(jnp.int32, (tb, tb), 0)
    col = lax.broadcasted_iota(jnp.int32, (tb, tb), 1)
    shift = CHUNK.bit_length() - 1
    causal = (row >= col) & (jnp.right_shift(row, shift) == jnp.right_shift(col, shift))
    tril = causal.astype(BF16)

    for hh in range(heads_per_step):
        cs = slice(hh * dk, (hh + 1) * dk)
        lb = lb_all[:, cs]
        qs = _silu(q_ref[:, cs])
        f = lb + (1.0 - lb) * _sigmoid(fz_ref[:, cs])
        kk = 1.0 - f
        v = iv_ref[:, cs].astype(BF16)
        b = _cumsum_rows(jnp.log(f), tril)
        b3 = b.reshape(n_chunks, CHUNK, dk)
        b_last = b3[:, CHUNK - 1:CHUNK, :]
        b_end = jnp.broadcast_to(b_last, b3.shape).reshape(tb, dk)
        q_in = (qs * jnp.exp(b)).astype(BF16)
        k_in = (kk * jnp.exp(-b)).astype(BF16)
        k_end = (kk * jnp.exp(b_end - b)).astype(BF16)
        scores = jnp.where(causal, _dot_nt(q_in, k_in), 0.0)
        o_intra = _dot(scores.astype(BF16), v)
        st = st_ref[hh]
        o_inter = []
        for c in range(n_chunks):
            rs = slice(c * CHUNK, (c + 1) * CHUNK)
            o_inter.append(_dot_nt(q_in[rs], st.astype(BF16)))
            st = jnp.exp(b_last[c]) * st + _dot_tn(v[rs], k_end[rs])
        st_ref[hh] = st
        o = o_intra + jnp.concatenate(o_inter, axis=0)
        o_ref[:, cs] = _head_norm_gate(o, g, og_ref[:, cs]).astype(o_ref.dtype)

    @pl.when(t == pl.num_programs(2) - 1)
    def _():
        for hh in range(heads_per_step):
            s_ref[0, hh] = st_ref[hh].T


def _hgrn_prompt(p, lb_logits, onorm_g, batch, seq, heads, tb, hp, side_w):
    n_t = seq // tb
    n_h = heads // hp
    dk = HEAD_DIM
    assert CHUNK & (CHUNK - 1) == 0
    n_slabs = batch * n_h * n_t
    side_in, side_out, side_shape = _slab_specs(side_w, n_slabs, lambda b, h, t: (b * n_h + h) * n_t + t)

    def col(k):
        return pl.BlockSpec((tb, hp * dk), lambda b, h, t: (b * n_t + t, k * n_h + h))

    return pl.pallas_call(
        functools.partial(_hgrn_prompt_kernel, n_slabs=n_slabs),
        grid=(batch, n_h, n_t),
        in_specs=[col(0), col(1), col(2), col(3),
                  pl.BlockSpec((2, 1, hp * dk), lambda b, h, t: (0, 0, h)),
                  pl.BlockSpec((1, dk), lambda b, h, t: (0, 0)),
                  side_in],
        out_specs=[pl.BlockSpec((tb, hp * dk), lambda b, h, t: (b * n_t + t, h)),
                   pl.BlockSpec((1, hp, dk, dk), lambda b, h, t: (b, h, 0, 0)),
                   side_out],
        out_shape=[jax.ShapeDtypeStruct((batch * seq, heads * dk), BF16),
                   jax.ShapeDtypeStruct((batch, heads, dk, dk), F32),
                   side_shape],
        scratch_shapes=[pltpu.VMEM((hp, dk, dk), F32)],
        compiler_params=_params(("arbitrary", "arbitrary", "arbitrary")),
        name="hgrn_prompt",
    )(p, p, p, p, lb_logits.reshape(2, 1, heads * dk), onorm_g.reshape(1, dk), side_w)


def _hgrn_sample_kernel(p_ref, lbl_ref, g_ref, s0_ref, o_ref, sn_ref):
    bb, heads, dk = s0_ref.shape[0], s0_ref.shape[1], s0_ref.shape[2]
    lb = _lower_bound(lbl_ref[...])
    g = g_ref[...]
    rows, per_seq = [], []
    for i in range(bb):
        p = p_ref[i]
        qs = _silu(p[0:heads])
        f = lb + (1.0 - lb) * _sigmoid(p[heads:2 * heads])
        kk = 1.0 - f
        rows += [f, kk, qs * f]
        per_seq.append((qs, kk, p[2 * heads:3 * heads], p[3 * heads:4 * heads]))
    pad = dk - 3 * heads * bb
    cols = jnp.concatenate(rows + [jnp.zeros((pad, dk), F32)], axis=0).T
    for i in range(bb):
        qs, kk, v, og = per_seq[i]
        o_inter = []
        for h in range(heads):
            c0 = 3 * heads * i + h
            f_c = cols[:, c0:c0 + 1]
            k_c = cols[:, c0 + heads:c0 + heads + 1]
            q_c = cols[:, c0 + 2 * heads:c0 + 2 * heads + 1]
            s = s0_ref[i, h]
            sn_ref[i, h] = f_c * s + k_c * v[h:h + 1, :]
            o_inter.append(jnp.sum(q_c * s, axis=0, keepdims=True))
        o = jnp.sum(qs * kk, axis=-1, keepdims=True) * v + jnp.concatenate(o_inter, axis=0)
        o_ref[i] = _head_norm_gate(o, g, og)


def _hgrn_sample(p3, lbl3, onorm_g, state, bb):
    n, rows, dk = p3.shape
    heads = rows // 4
    assert n % bb == 0 and 3 * heads * bb <= dk
    return pl.pallas_call(
        _hgrn_sample_kernel,
        grid=(n // bb,),
        in_specs=[pl.BlockSpec((bb, rows, dk), lambda b: (b, 0, 0)),
                  pl.BlockSpec((2, heads, dk), lambda b: (0, 0, 0)),
                  pl.BlockSpec((1, dk), lambda b: (0, 0)),
                  pl.BlockSpec((bb, heads, dk, dk), lambda b: (b, 0, 0, 0))],
        out_specs=[pl.BlockSpec((bb, heads, dk), lambda b: (b, 0, 0)),
                   pl.BlockSpec((bb, heads, dk, dk), lambda b: (b, 0, 0, 0))],
        out_shape=[jax.ShapeDtypeStruct((n, heads, dk), F32),
                   jax.ShapeDtypeStruct((n, heads, dk, dk), F32)],
        compiler_params=_params(("parallel",)),
        name="hgrn_sample",
    )(p3, lbl3, onorm_g.reshape(1, dk), state)


def _merge_kernel(hx_ref, a_ref, o_ref, at_ref, ot_ref, wga_ref, wgb_ref, wa_ref, wb_ref, side_ref,
                  mix_ref, side_out_ref, wabb_ref, *, n_full, n_slabs):
    j, i = pl.program_id(0), pl.program_id(1)
    _slab_cast(j * pl.num_programs(1) + i, n_slabs, side_ref, side_out_ref)

    @pl.when(i == 0)
    def _():
        wabb_ref[0] = wa_ref[...].astype(BF16)
        wabb_ref[1] = wb_ref[...].astype(BF16)

    def body(rows, is_tail):
        hx = hx_ref[0:rows, :]
        a = at_ref[...] if is_tail else a_ref[...]
        o = ot_ref[...].astype(BF16) if is_tail else o_ref[...]
        ga = _sigmoid(_dot(hx, wga_ref[...]))
        gb = _sigmoid(_dot(hx, wgb_ref[...]))
        mix = ga * _dot(a, wabb_ref[0]) + gb * _dot(o, wabb_ref[1])
        mix_ref[0:rows, :] = mix.astype(mix_ref.dtype)

    _row_split(i, n_full, hx_ref.shape[0], at_ref.shape[0], body)


def _merge(hx, a, o, a_tail, o_tail, w_in, w_a, w_b, col_ga, tm, tn, side_w, n_slabs):
    m, d = hx.shape
    dc, dh = a.shape[1], o.shape[1]
    tail = a_tail.shape[0]
    n_full = a.shape[0] // tm
    n_n = d // tn
    clamp = lambda j, i: (jnp.minimum(i, n_full - 1), 0)
    assert n_slabs <= n_n * (n_full + 1)
    side_in, side_out, side_shape = _slab_specs(side_w, n_slabs, lambda j, i: j * (n_full + 1) + i)
    return pl.pallas_call(
        functools.partial(_merge_kernel, n_full=n_full, n_slabs=n_slabs),
        grid=(n_n, n_full + 1),
        in_specs=[pl.BlockSpec((tm, d), lambda j, i: (i, 0)),
                  pl.BlockSpec((tm, dc), clamp),
                  pl.BlockSpec((tm, dh), clamp),
                  pl.BlockSpec((tail, dc), lambda j, i: (0, 0)),
                  pl.BlockSpec((tail, dh), lambda j, i: (0, 0)),
                  pl.BlockSpec((d, tn), lambda j, i: (0, col_ga // tn + j)),
                  pl.BlockSpec((d, tn), lambda j, i: (0, col_ga // tn + n_n + j)),
                  pl.BlockSpec((dc, tn), lambda j, i: (0, j)),
                  pl.BlockSpec((dh, tn), lambda j, i: (0, j)),
                  side_in],
        out_specs=[pl.BlockSpec((tm, tn), lambda j, i: (i, j)), side_out],
        out_shape=[jax.ShapeDtypeStruct((m, d), BF16), side_shape],
        scratch_shapes=[pltpu.VMEM((2, dc, tn), BF16)],
        compiler_params=_params(("arbitrary", "arbitrary")),
        name="merge",
    )(hx, a, o, a_tail, o_tail, w_in, w_in, w_a, w_b, side_w)


def _outproj_kernel(xp_ref, xs_ref, mix_ref, w_ref, g_ref, x1_ref, h2_ref, *, n_full):
    i = pl.program_id(0)

    def body(rows, is_tail):
        x = xs_ref[...] if is_tail else xp_ref[...]
        x1 = x + _dot(mix_ref[0:rows, :], w_ref[...])
        x1_ref[0:rows, :] = x1
        h2_ref[0:rows, :] = _rms_scale(x1, g_ref[...]).astype(h2_ref.dtype)

    _row_split(i, n_full, xp_ref.shape[0], xs_ref.shape[0], body)


def _outproj(xp, xs, mix, w_out, g, tm):
    mp, d = xp.shape
    tail = xs.shape[0]
    n_full = mp // tm
    return pl.pallas_call(
        functools.partial(_outproj_kernel, n_full=n_full),
        grid=(n_full + 1,),
        in_specs=[pl.BlockSpec((tm, d), lambda i: (jnp.minimum(i, n_full - 1), 0)),
                  pl.BlockSpec((tail, d), lambda i: (0, 0)),
                  pl.BlockSpec((tm, d), lambda i: (i, 0)),
                  pl.BlockSpec((d, d), lambda i: (0, 0)),
                  pl.BlockSpec((1, d), lambda i: (0, 0))],
        out_specs=[pl.BlockSpec((tm, d), lambda i: (i, 0)),
                   pl.BlockSpec((tm, d), lambda i: (i, 0))],
        out_shape=[jax.ShapeDtypeStruct((mp + tail, d), F32),
                   jax.ShapeDtypeStruct((mp + tail, d), BF16)],
        compiler_params=_params(("arbitrary",)),
        name="outproj",
    )(xp, xs, mix, w_out, g.reshape(1, d))


def _mlp_kernel(h2_ref, x1_ref, wup_ref, wdn_ref, g_ref, yp_ref, ys_ref, *, n_full):
    i, j = pl.program_id(0), pl.program_id(1)
    last = pl.num_programs(1) - 1

    def body(rows, is_tail):
        y_ref = ys_ref if is_tail else yp_ref

        @pl.when(j == 0)
        def _():
            y_ref[...] = x1_ref[0:rows, :]

        h = jnp.maximum(_dot(h2_ref[0:rows, :], wup_ref[...]), 0.0)
        y_ref[...] += _dot((h * h).astype(BF16), wdn_ref[...])

        @pl.when(j == last)
        def _():
            y_ref[...] = _rms_scale(y_ref[...], g_ref[...])

    _row_split(i, n_full, yp_ref.shape[0], ys_ref.shape[0], body)


def _mlp(h2, x1, w_up, w_down, g, tm, tf, tail):
    m, d = h2.shape
    dff = w_up.shape[1]
    n_full = (m - tail) // tm
    return pl.pallas_call(
        functools.partial(_mlp_kernel, n_full=n_full),
        grid=(n_full + 1, dff // tf),
        in_specs=[pl.BlockSpec((tm, d), lambda i, j: (i, 0)),
                  pl.BlockSpec((tm, d), lambda i, j: (i, 0)),
                  pl.BlockSpec((d, tf), lambda i, j: (0, j)),
                  pl.BlockSpec((tf, d), lambda i, j: (j, 0)),
                  pl.BlockSpec((1, d), lambda i, j: (0, 0))],
        out_specs=[pl.BlockSpec((tm, d), lambda i, j: (jnp.minimum(i, n_full - 1), 0)),
                   pl.BlockSpec((tail, d), lambda i, j: (0, 0))],
        out_shape=[jax.ShapeDtypeStruct((m - tail, d), F32),
                   jax.ShapeDtypeStruct((tail, d), F32)],
        compiler_params=_params(("arbitrary", "arbitrary")),
        name="mlp",
    )(h2, x1, w_up, w_down, g.reshape(1, d))


def kernel(x_prompt, x_sample, state_conv, state_hgrn, norm_mix, w_in, conv_w, lb_logits, onorm_g,
           w_branch_a, w_branch_b, w_out, norm_ffn, w_up, w_down, norm_final):
    batch, seq, d = x_prompt.shape
    n_dec = x_sample.shape[0]
    depth, _, d_conv = conv_w.shape
    heads, dk = state_hgrn.shape[2], state_hgrn.shape[3]
    d_hgrn = heads * dk
    mp = batch * seq
    assert depth == 1 and x_sample.shape[1] == 1 and dk == HEAD_DIM and state_hgrn.shape[4] == dk
    assert seq % ROW_BLOCK == 0 and mp % MLP_ROW_BLOCK == 0 and ROW_BLOCK % n_dec == 0
    col_hgrn = 3 * d_conv
    col_ga = col_hgrn + 4 * d_hgrn
    w_in0 = w_in.reshape(w_in.shape[1:])
    w_a0 = w_branch_a.reshape(w_branch_a.shape[1:])
    w_b0 = w_branch_b.reshape(w_branch_b.shape[1:])
    w_out0 = w_out.reshape(w_out.shape[1:])
    w_up0 = w_up.reshape(w_up.shape[1:])
    w_down0 = w_down.reshape(w_down.shape[1:])

    xp = x_prompt.reshape(mp, d)
    xs = x_sample.reshape(n_dec, d)
    hx = _rmsnorm(xp, xs, norm_mix[0], ROW_BLOCK)

    a_p, conv_p, w_in_b = _conv_prompt(hx, w_in0, conv_w[0], batch, seq, d_conv, ROW_BLOCK, CONV_COL_BLOCK)
    st2 = state_conv[0].reshape(n_dec, 2 * d_conv)
    a_s, u_s = _conv_sample(hx, mp // n_dec, n_dec, w_in_b, conv_w[0], st2, d_conv, CONV_COL_BLOCK)

    p, w_up_b = _proj(hx, w_in_b, col_hgrn, 4 * d_hgrn, ROW_BLOCK, PROJ_COL_BLOCK, n_dec, w_up0, SIDE_SLABS)
    o_p, hgrn_p, w_down_b = _hgrn_prompt(p, lb_logits, onorm_g[0], batch, seq, heads, HGRN_ROWS, HGRN_HEADS,
                                         w_down0)
    o_s, hgrn_s = _hgrn_sample(p[mp:].reshape(n_dec, 4 * heads, dk), lb_logits.reshape(2, heads, dk),
                               onorm_g[0], state_hgrn[0], HGRN_DECODE_SEQS)

    mix, w_out_b = _merge(hx, a_p, o_p, a_s, o_s.reshape(n_dec, d_hgrn), w_in_b, w_a0, w_b0,
                          col_ga, ROW_BLOCK, MERGE_COL_BLOCK, w_out0, SIDE_SLABS // 2)
    x1, h2 = _outproj(xp, xs, mix, w_out_b, norm_ffn[0], ROW_BLOCK)
    y_p, y_s = _mlp(h2, x1, w_up_b, w_down_b, norm_final, MLP_ROW_BLOCK, MLP_FF_BLOCK, n_dec)

    conv_s = jnp.stack([state_conv[0, :, 1, :], u_s], axis=1)
    return (y_p.reshape(batch, seq, d), y_s.reshape(n_dec, 1, d),
            conv_p[None], hgrn_p[None], conv_s[None], hgrn_s[None])
```

```python
import functools

import jax
import jax.numpy as jnp
from jax import lax
from jax.experimental import pallas as pl
from jax.experimental.pallas import tpu as pltpu

EPS = 1e-6
CHUNK = 64
HEAD_DIM = 128
V7X_VMEM_BYTES = 64 * 1024 * 1024
VMEM_LIMIT = V7X_VMEM_BYTES * 7 // 8

ROW_BLOCK = 512
MLP_ROW_BLOCK = 1024
MLP_FF_BLOCK = 512
CONV_COL_BLOCK = 512
PROJ_COL_BLOCK = 1024
MERGE_COL_BLOCK = 1024
SIDE_SLABS = 64
HGRN_ROWS = 256
HGRN_HEADS = 4
HGRN_DECODE_SEQS = 4

BF16 = jnp.bfloat16
F32 = jnp.float32


def _params(semantics):
    return pltpu.CompilerParams(dimension_semantics=semantics, vmem_limit_bytes=VMEM_LIMIT)


def _dot(a, b):
    return jnp.dot(a, b, preferred_element_type=F32)


def _dot_nt(a, b):
    return lax.dot_general(a, b, (((1,), (1,)), ((), ())), preferred_element_type=F32)


def _dot_tn(a, b):
    return lax.dot_general(a, b, (((0,), (0,)), ((), ())), preferred_element_type=F32)


def _sigmoid(x):
    return jax.nn.sigmoid(x)


def _silu(x):
    return x * jax.nn.sigmoid(x)


def _rms_scale(x, g):
    ms = jnp.mean(x * x, axis=-1, keepdims=True)
    return x * lax.rsqrt(ms + EPS) * g


def _row_split(i, n_full, full_rows, tail_rows, body):
    @pl.when(i < n_full)
    def _():
        body(full_rows, False)

    @pl.when(i == n_full)
    def _():
        body(tail_rows, True)


def _slab_cast(step, n_slabs, src_ref, dst_ref):
    @pl.when(step < n_slabs)
    def _():
        dst_ref[...] = src_ref[...].astype(dst_ref.dtype)


def _slab_specs(w, n_slabs, step_of):
    r, c = w.shape
    assert r % n_slabs == 0

    def index(*g):
        return (jnp.minimum(step_of(*g), n_slabs - 1), 0)

    spec = pl.BlockSpec((r // n_slabs, c), index)
    return spec, spec, jax.ShapeDtypeStruct((r, c), BF16)


def _rmsnorm_kernel(xp_ref, xs_ref, g_ref, o_ref, *, n_full):
    tm, tail = xp_ref.shape[0], xs_ref.shape[0]

    def body(rows, is_tail):
        x = xs_ref[...] if is_tail else xp_ref[...]
        o_ref[0:rows, :] = _rms_scale(x, g_ref[...]).astype(o_ref.dtype)

    _row_split(pl.program_id(0), n_full, tm, tail, body)


def _rmsnorm(xp, xs, g, tm):
    mp, d = xp.shape
    tail = xs.shape[0]
    n_full = mp // tm
    return pl.pallas_call(
        functools.partial(_rmsnorm_kernel, n_full=n_full),
        grid=(n_full + 1,),
        in_specs=[pl.BlockSpec((tm, d), lambda i: (jnp.minimum(i, n_full - 1), 0)),
                  pl.BlockSpec((tail, d), lambda i: (0, 0)),
                  pl.BlockSpec((1, d), lambda i: (0, 0))],
        out_specs=pl.BlockSpec((tm, d), lambda i: (i, 0)),
        out_shape=jax.ShapeDtypeStruct((mp + tail, d), BF16),
        compiler_params=_params(("arbitrary",)),
        name="rmsnorm",
    )(xp, xs, g.reshape(1, d))


def _conv_prompt_kernel(hx_ref, whc_ref, wbg_ref, wcg_ref, cw_ref, side_ref, a_ref, nc_ref, side_out_ref,
                        wb_ref, carry_ref, *, n_slabs):
    c, b, t = pl.program_id(0), pl.program_id(1), pl.program_id(2)
    tm = hx_ref.shape[0]
    _slab_cast((c * pl.num_programs(1) + b) * pl.num_programs(2) + t, n_slabs, side_ref, side_out_ref)

    @pl.when((b == 0) & (t == 0))
    def _():
        wb_ref[0] = whc_ref[...].astype(BF16)
        wb_ref[1] = wbg_ref[...].astype(BF16)
        wb_ref[2] = wcg_ref[...].astype(BF16)

    @pl.when(t == 0)
    def _():
        carry_ref[...] = jnp.zeros_like(carry_ref)

    hx = hx_ref[...]
    hc = _dot(hx, wb_ref[0])
    bg = _dot(hx, wb_ref[1])
    cg = _dot(hx, wb_ref[2])
    u = cg * hc
    c0 = carry_ref[0:1, :]
    c1 = carry_ref[1:2, :]
    row = lax.broadcasted_iota(jnp.int32, u.shape, 0)
    u1 = jnp.where(row == 0, c1, pltpu.roll(u, 1, 0))
    u2 = jnp.where(row == 0, c0, jnp.where(row == 1, c1, pltpu.roll(u, 2, 0)))
    cw = cw_ref[...]
    conv = cw[0:1, :] * u2 + cw[1:2, :] * u1 + cw[2:3, :] * u
    a_ref[...] = (bg * conv).astype(a_ref.dtype)
    tail = u[tm - 2:tm, :]
    carry_ref[0:2, :] = tail

    @pl.when(t == pl.num_programs(2) - 1)
    def _():
        nc_ref[0] = tail


def _conv_prompt(hx, w_in, conv_w, batch, seq, d_conv, tm, tc):
    d = hx.shape[1]
    n_c = d_conv // tc
    n_t = seq // tm
    n_slabs = n_c * batch * n_t
    side_in, side_out, side_shape = _slab_specs(w_in, n_slabs, lambda c, b, t: (c * batch + b) * n_t + t)
    return pl.pallas_call(
        functools.partial(_conv_prompt_kernel, n_slabs=n_slabs),
        grid=(n_c, batch, n_t),
        in_specs=[pl.BlockSpec((tm, d), lambda c, b, t: (b * n_t + t, 0)),
                  pl.BlockSpec((d, tc), lambda c, b, t: (0, c)),
                  pl.BlockSpec((d, tc), lambda c, b, t: (0, n_c + c)),
                  pl.BlockSpec((d, tc), lambda c, b, t: (0, 2 * n_c + c)),
                  pl.BlockSpec((3, tc), lambda c, b, t: (0, c)),
                  side_in],
        out_specs=[pl.BlockSpec((tm, tc), lambda c, b, t: (b * n_t + t, c)),
                   pl.BlockSpec((1, 2, tc), lambda c, b, t: (b, 0, c)),
                   side_out],
        out_shape=[jax.ShapeDtypeStruct((batch * seq, d_conv), BF16),
                   jax.ShapeDtypeStruct((batch, 2, d_conv), F32),
                   side_shape],
        scratch_shapes=[pltpu.VMEM((3, d, tc), BF16), pltpu.VMEM((8, tc), F32)],
        compiler_params=_params(("arbitrary", "arbitrary", "arbitrary")),
        name="conv_prompt",
    )(hx, w_in, w_in, w_in, conv_w, w_in)


def _conv_sample_kernel(hx_ref, whc_ref, wbg_ref, wcg_ref, cw_ref, s0_ref, s1_ref, a_ref, u_ref):
    hx = hx_ref[...]
    hc = _dot(hx, whc_ref[...])
    bg = _dot(hx, wbg_ref[...])
    cg = _dot(hx, wcg_ref[...])
    u = cg * hc
    cw = cw_ref[...]
    conv = cw[0:1, :] * s0_ref[...] + cw[1:2, :] * s1_ref[...] + cw[2:3, :] * u
    a_ref[...] = (bg * conv).astype(a_ref.dtype)
    u_ref[...] = u


def _conv_sample(hx, row_block, n, w_in, conv_w, state2d, d_conv, tc):
    d = hx.shape[1]
    n_c = d_conv // tc
    return pl.pallas_call(
        _conv_sample_kernel,
        grid=(n_c,),
        in_specs=[pl.BlockSpec((n, d), lambda c: (row_block, 0)),
                  pl.BlockSpec((d, tc), lambda c: (0, c)),
                  pl.BlockSpec((d, tc), lambda c: (0, n_c + c)),
                  pl.BlockSpec((d, tc), lambda c: (0, 2 * n_c + c)),
                  pl.BlockSpec((3, tc), lambda c: (0, c)),
                  pl.BlockSpec((n, tc), lambda c: (0, c)),
                  pl.BlockSpec((n, tc), lambda c: (0, n_c + c))],
        out_specs=[pl.BlockSpec((n, tc), lambda c: (0, c)),
                   pl.BlockSpec((n, tc), lambda c: (0, c))],
        out_shape=[jax.ShapeDtypeStruct((n, d_conv), BF16),
                   jax.ShapeDtypeStruct((n, d_conv), F32)],
        compiler_params=_params(("arbitrary",)),
        name="conv_sample",
    )(hx, w_in, w_in, w_in, conv_w, state2d, state2d)


def _proj_kernel(x_ref, w_ref, o_ref):
    o_ref[...] = _dot(x_ref[...], w_ref[...])


def _proj(x, row_block, rows, w, col0, n, tn):
    d = x.shape[1]
    return pl.pallas_call(
        _proj_kernel,
        grid=(n // tn,),
        in_specs=[pl.BlockSpec((rows, d), lambda j: (row_block, 0)),
                  pl.BlockSpec((d, tn), lambda j: (0, col0 // tn + j))],
        out_specs=pl.BlockSpec((rows, tn), lambda j: (0, j)),
        out_shape=jax.ShapeDtypeStruct((rows, n), F32),
        compiler_params=_params(("arbitrary",)),
        name="hgrn_proj_decode",
    )(x, w)


def _lower_bound(lbl):
    e = jnp.exp(lbl - jnp.max(lbl, axis=0))
    return e[0] / jnp.sum(e, axis=0)


def _head_norm_gate(o, g, og):
    return _rms_scale(o, g) * _silu(og)


def _cumsum_rows(x, tril_bf16):
    hi = x.astype(BF16)
    r1 = x - hi.astype(F32)
    mid = r1.astype(BF16)
    lo = (r1 - mid.astype(F32)).astype(BF16)
    n = x.shape[1]
    parts = _dot(tril_bf16, jnp.concatenate([hi, mid, lo], axis=1))
    return parts[:, 0:n] + parts[:, n:2 * n] + parts[:, 2 * n:3 * n]


def _hgrn_prompt_kernel(hx_ref, wq_ref, wf_ref, wi_ref, wo_ref, lbl_ref, g_ref, side_a_ref, side_b_ref,
                        o_ref, s_ref, side_a_out_ref, side_b_out_ref, pa_ref, pb_ref, st_ref, *, n_slabs, n_t):
    s = pl.program_id(0)
    _slab_cast(s, n_slabs, side_a_ref, side_a_out_ref)
    _slab_cast(s, n_slabs, side_b_ref, side_b_out_ref)
    tb = hx_ref.shape[0]
    heads_per_step = st_ref.shape[0]
    dk = HEAD_DIM
    n_chunks = tb // CHUNK
    t = lax.rem(jnp.maximum(s - 1, 0), n_t)

    @pl.when(s == 0)
    def _():
        pb_ref[...] = jnp.zeros_like(pb_ref)

    @pl.when(t == 0)
    def _():
        st_ref[...] = jnp.zeros_like(st_ref)

    lb_all = _lower_bound(lbl_ref[...])
    g = g_ref[...]
    row = lax.broadcasted_iota(jnp.int32, (tb, tb), 0)
    col = lax.broadcasted_iota(jnp.int32, (tb, tb), 1)
    shift = CHUNK.bit_length() - 1
    causal = (row >= col) & (jnp.right_shift(row, shift) == jnp.right_shift(col, shift))
    tril = causal.astype(BF16)

    def body(p_next_ref, p_ref):
        heads = range(heads_per_step)
        cols = [slice(hh * dk, (hh + 1) * dk) for hh in heads]

        def project(k, w_ref):
            p_next_ref[k] = _dot(hx_ref[...], w_ref[...])

        qs, kk, lf = [], [], []
        for hh in heads:
            lb = lb_all[:, cols[hh]]
            f = lb + (1.0 - lb) * _sigmoid(p_ref[1, :, cols[hh]])
            qs.append(_silu(p_ref[0, :, cols[hh]]))
            kk.append(1.0 - f)
            lf.append(jnp.log(f))
        project(0, wq_ref)
        b = [_cumsum_rows(lf[hh], tril) for hh in heads]
        q_in, k_in, k_end, decay = [], [], [], []
        for hh in heads:
            b3 = b[hh].reshape(n_chunks, CHUNK, dk)
            b_last = b3[:, CHUNK - 1:CHUNK, :]
            b_end = jnp.broadcast_to(b_last, b3.shape).reshape(tb, dk)
            q_in.append((qs[hh] * jnp.exp(b[hh])).astype(BF16))
            k_in.append((kk[hh] * jnp.exp(-b[hh])).astype(BF16))
            k_end.append((kk[hh] * jnp.exp(b_end - b[hh])).astype(BF16))
            decay.append(jnp.exp(b_last))
        project(1, wf_ref)
        v = [p_ref[2, :, cols[hh]].astype(BF16) for hh in heads]
        raw = [_dot_nt(q_in[hh], k_in[hh]) for hh in heads]
        delta = [[_dot_tn(v[hh][c * CHUNK:(c + 1) * CHUNK], k_end[hh][c * CHUNK:(c + 1) * CHUNK])
                  for c in range(n_chunks)] for hh in heads]
        scores = [jnp.where(causal, raw[hh], 0.0).astype(BF16) for hh in heads]
        starts = []
        for hh in heads:
            st = st_ref[hh]
            per_chunk = []
            for c in range(n_chunks):
                per_chunk.append(st.astype(BF16))
                st = decay[hh][c] * st + delta[hh][c]
            st_ref[hh] = st
            starts.append(per_chunk)
        project(2, wi_ref)
        o = []
        for hh in heads:
            inter = [_dot_nt(q_in[hh][c * CHUNK:(c + 1) * CHUNK], starts[hh][c]) for c in range(n_chunks)]
            o.append(_dot(scores[hh], v[hh]) + jnp.concatenate(inter, axis=0))
        for hh in heads:
            o_ref[:, cols[hh]] = _head_norm_gate(o[hh], g, p_ref[3, :, cols[hh]]).astype(o_ref.dtype)
        project(3, wo_ref)

    parity = lax.rem(s, 2)

    @pl.when(parity == 0)
    def _():
        body(pa_ref, pb_ref)

    @pl.when(parity == 1)
    def _():
        body(pb_ref, pa_ref)

    @pl.when(t == n_t - 1)
    def _():
        for hh in range(heads_per_step):
            s_ref[0, hh] = st_ref[hh].T


def _hgrn_prompt(hx, w_in, col0, lb_logits, onorm_g, batch, seq, heads, tb, hp, side_a, side_b):
    d = hx.shape[1]
    n_t = seq // tb
    n_h = heads // hp
    dk = HEAD_DIM
    wc = hp * dk
    assert CHUNK & (CHUNK - 1) == 0 and col0 % wc == 0
    rows_blocks = batch * n_t
    n_steps = n_h * rows_blocks
    proj = lambda s: jnp.minimum(s, n_steps - 1)
    rec = lambda s: jnp.maximum(s - 1, 0)
    a_in, a_out, a_shape = _slab_specs(side_a, n_steps, lambda s: s)
    b_in, b_out, b_shape = _slab_specs(side_b, n_steps, lambda s: s)

    def wcol(k):
        return pl.BlockSpec((d, wc), lambda s: (0, col0 // wc + k * n_h + proj(s) // rows_blocks))

    return pl.pallas_call(
        functools.partial(_hgrn_prompt_kernel, n_slabs=n_steps, n_t=n_t),
        grid=(n_steps + 1,),
        in_specs=[pl.BlockSpec((tb, d), lambda s: (proj(s) % rows_blocks, 0)),
                  wcol(0), wcol(1), wcol(2), wcol(3),
                  pl.BlockSpec((2, 1, wc), lambda s: (0, 0, rec(s) // rows_blocks)),
                  pl.BlockSpec((1, dk), lambda s: (0, 0)),
                  a_in, b_in],
        out_specs=[pl.BlockSpec((tb, wc), lambda s: (rec(s) % rows_blocks, rec(s) // rows_blocks)),
                   pl.BlockSpec((1, hp, dk, dk),
                                lambda s: ((rec(s) % rows_blocks) // n_t, rec(s) // rows_blocks, 0, 0)),
                   a_out, b_out],
        out_shape=[jax.ShapeDtypeStruct((batch * seq, heads * dk), BF16),
                   jax.ShapeDtypeStruct((batch, heads, dk, dk), F32),
                   a_shape, b_shape],
        scratch_shapes=[pltpu.VMEM((4, tb, wc), F32), pltpu.VMEM((4, tb, wc), F32),
                        pltpu.VMEM((hp, dk, dk), F32)],
        compiler_params=_params(("arbitrary",)),
        name="hgrn_prompt",
    )(hx, w_in, w_in, w_in, w_in, lb_logits.reshape(2, 1, heads * dk), onorm_g.reshape(1, dk), side_a, side_b)


def _hgrn_sample_kernel(p_ref, lbl_ref, g_ref, s0_ref, o_ref, sn_ref):
    bb, heads, dk = s0_ref.shape[0], s0_ref.shape[1], s0_ref.shape[2]
    lb = _lower_bound(lbl_ref[...])
    g = g_ref[...]
    rows, per_seq = [], []
    for i in range(bb):
        p = p_ref[i]
        qs = _silu(p[0:heads])
        f = lb + (1.0 - lb) * _sigmoid(p[heads:2 * heads])
        kk = 1.0 - f
        rows += [f, kk, qs * f]
        per_seq.append((qs, kk, p[2 * heads:3 * heads], p[3 * heads:4 * heads]))
    pad = dk - 3 * heads * bb
    cols = jnp.concatenate(rows + [jnp.zeros((pad, dk), F32)], axis=0).T
    for i in range(bb):
        qs, kk, v, og = per_seq[i]
        o_inter = []
        for h in range(heads):
            c0 = 3 * heads * i + h
            f_c = cols[:, c0:c0 + 1]
            k_c = cols[:, c0 + heads:c0 + heads + 1]
            q_c = cols[:, c0 + 2 * heads:c0 + 2 * heads + 1]
            s = s0_ref[i, h]
            sn_ref[i, h] = f_c * s + k_c * v[h:h + 1, :]
            o_inter.append(jnp.sum(q_c * s, axis=0, keepdims=True))
        o = jnp.sum(qs * kk, axis=-1, keepdims=True) * v + jnp.concatenate(o_inter, axis=0)
        o_ref[i] = _head_norm_gate(o, g, og)


def _hgrn_sample(p3, lbl3, onorm_g, state, bb):
    n, rows, dk = p3.shape
    heads = rows // 4
    assert n % bb == 0 and 3 * heads * bb <= dk
    return pl.pallas_call(
        _hgrn_sample_kernel,
        grid=(n // bb,),
        in_specs=[pl.BlockSpec((bb, rows, dk), lambda b: (b, 0, 0)),
                  pl.BlockSpec((2, heads, dk), lambda b: (0, 0, 0)),
                  pl.BlockSpec((1, dk), lambda b: (0, 0)),
                  pl.BlockSpec((bb, heads, dk, dk), lambda b: (b, 0, 0, 0))],
        out_specs=[pl.BlockSpec((bb, heads, dk), lambda b: (b, 0, 0)),
                   pl.BlockSpec((bb, heads, dk, dk), lambda b: (b, 0, 0, 0))],
        out_shape=[jax.ShapeDtypeStruct((n, heads, dk), F32),
                   jax.ShapeDtypeStruct((n, heads, dk, dk), F32)],
        compiler_params=_params(("parallel",)),
        name="hgrn_sample",
    )(p3, lbl3, onorm_g.reshape(1, dk), state)


def _merge_kernel(hx_ref, a_ref, o_ref, at_ref, ot_ref, wga_ref, wgb_ref, wa_ref, wb_ref, side_ref,
                  mix_ref, side_out_ref, wabb_ref, *, n_full, n_slabs):
    j, i = pl.program_id(0), pl.program_id(1)
    _slab_cast(j * pl.num_programs(1) + i, n_slabs, side_ref, side_out_ref)

    @pl.when(i == 0)
    def _():
        wabb_ref[0] = wa_ref[...].astype(BF16)
        wabb_ref[1] = wb_ref[...].astype(BF16)

    def body(rows, is_tail):
        hx = hx_ref[0:rows, :]
        a = at_ref[...] if is_tail else a_ref[...]
        o = ot_ref[...].astype(BF16) if is_tail else o_ref[...]
        ga = _sigmoid(_dot(hx, wga_ref[...]))
        gb = _sigmoid(_dot(hx, wgb_ref[...]))
        mix = ga * _dot(a, wabb_ref[0]) + gb * _dot(o, wabb_ref[1])
        mix_ref[0:rows, :] = mix.astype(mix_ref.dtype)

    _row_split(i, n_full, hx_ref.shape[0], at_ref.shape[0], body)


def _merge(hx, a, o, a_tail, o_tail, w_in, w_a, w_b, col_ga, tm, tn, side_w, n_slabs):
    m, d = hx.shape
    dc, dh = a.shape[1], o.shape[1]
    tail = a_tail.shape[0]
    n_full = a.shape[0] // tm
    n_n = d // tn
    clamp = lambda j, i: (jnp.minimum(i, n_full - 1), 0)
    assert n_slabs <= n_n * (n_full + 1)
    side_in, side_out, side_shape = _slab_specs(side_w, n_slabs, lambda j, i: j * (n_full + 1) + i)
    return pl.pallas_call(
        functools.partial(_merge_kernel, n_full=n_full, n_slabs=n_slabs),
        grid=(n_n, n_full + 1),
        in_specs=[pl.BlockSpec((tm, d), lambda j, i: (i, 0)),
                  pl.BlockSpec((tm, dc), clamp),
                  pl.BlockSpec((tm, dh), clamp),
                  pl.BlockSpec((tail, dc), lambda j, i: (0, 0)),
                  pl.BlockSpec((tail, dh), lambda j, i: (0, 0)),
                  pl.BlockSpec((d, tn), lambda j, i: (0, col_ga // tn + j)),
                  pl.BlockSpec((d, tn), lambda j, i: (0, col_ga // tn + n_n + j)),
                  pl.BlockSpec((dc, tn), lambda j, i: (0, j)),
                  pl.BlockSpec((dh, tn), lambda j, i: (0, j)),
                  side_in],
        out_specs=[pl.BlockSpec((tm, tn), lambda j, i: (i, j)), side_out],
        out_shape=[jax.ShapeDtypeStruct((m, d), BF16), side_shape],
        scratch_shapes=[pltpu.VMEM((2, dc, tn), BF16)],
        compiler_params=_params(("arbitrary", "arbitrary")),
        name="merge",
    )(hx, a, o, a_tail, o_tail, w_in, w_in, w_a, w_b, side_w)


def _outproj_kernel(xp_ref, xs_ref, mix_ref, w_ref, g_ref, x1_ref, h2_ref, *, n_full):
    i = pl.program_id(0)

    def body(rows, is_tail):
        x = xs_ref[...] if is_tail else xp_ref[...]
        x1 = x + _dot(mix_ref[0:rows, :], w_ref[...])
        x1_ref[0:rows, :] = x1
        h2_ref[0:rows, :] = _rms_scale(x1, g_ref[...]).astype(h2_ref.dtype)

    _row_split(i, n_full, xp_ref.shape[0], xs_ref.shape[0], body)


def _outproj(xp, xs, mix, w_out, g, tm):
    mp, d = xp.shape
    tail = xs.shape[0]
    n_full = mp // tm
    return pl.pallas_call(
        functools.partial(_outproj_kernel, n_full=n_full),
        grid=(n_full + 1,),
        in_specs=[pl.BlockSpec((tm, d), lambda i: (jnp.minimum(i, n_full - 1), 0)),
                  pl.BlockSpec((tail, d), lambda i: (0, 0)),
                  pl.BlockSpec((tm, d), lambda i: (i, 0)),
                  pl.BlockSpec((d, d), lambda i: (0, 0)),
                  pl.BlockSpec((1, d), lambda i: (0, 0))],
        out_specs=[pl.BlockSpec((tm, d), lambda i: (i, 0)),
                   pl.BlockSpec((tm, d), lambda i: (i, 0))],
        out_shape=[jax.ShapeDtypeStruct((mp + tail, d), F32),
                   jax.ShapeDtypeStruct((mp + tail, d), BF16)],
        compiler_params=_params(("arbitrary",)),
        name="outproj",
    )(xp, xs, mix, w_out, g.reshape(1, d))


def _mlp_kernel(h2_ref, x1_ref, wup_ref, wdn_ref, g_ref, yp_ref, ys_ref, *, n_full):
    i, j = pl.program_id(0), pl.program_id(1)
    last = pl.num_programs(1) - 1

    def body(rows, is_tail):
        y_ref = ys_ref if is_tail else yp_ref

        @pl.when(j == 0)
        def _():
            y_ref[...] = x1_ref[0:rows, :]

        h = jnp.maximum(_dot(h2_ref[0:rows, :], wup_ref[...]), 0.0)
        y_ref[...] += _dot((h * h).astype(BF16), wdn_ref[...])

        @pl.when(j == last)
        def _():
            y_ref[...] = _rms_scale(y_ref[...], g_ref[...])

    _row_split(i, n_full, yp_ref.shape[0], ys_ref.shape[0], body)


def _mlp(h2, x1, w_up, w_down, g, tm, tf, tail):
    m, d = h2.shape
    dff = w_up.shape[1]
    n_full = (m - tail) // tm
    return pl.pallas_call(
        functools.partial(_mlp_kernel, n_full=n_full),
        grid=(n_full + 1, dff // tf),
        in_specs=[pl.BlockSpec((tm, d), lambda i, j: (i, 0)),
                  pl.BlockSpec((tm, d), lambda i, j: (i, 0)),
                  pl.BlockSpec((d, tf), lambda i, j: (0, j)),
                  pl.BlockSpec((tf, d), lambda i, j: (j, 0)),
                  pl.BlockSpec((1, d), lambda i, j: (0, 0))],
        out_specs=[pl.BlockSpec((tm, d), lambda i, j: (jnp.minimum(i, n_full - 1), 0)),
                   pl.BlockSpec((tail, d), lambda i, j: (0, 0))],
        out_shape=[jax.ShapeDtypeStruct((m - tail, d), F32),
                   jax.ShapeDtypeStruct((tail, d), F32)],
        compiler_params=_params(("arbitrary", "arbitrary")),
        name="mlp",
    )(h2, x1, w_up, w_down, g.reshape(1, d))


def kernel(x_prompt, x_sample, state_conv, state_hgrn, norm_mix, w_in, conv_w, lb_logits, onorm_g,
           w_branch_a, w_branch_b, w_out, norm_ffn, w_up, w_down, norm_final):
    batch, seq, d = x_prompt.shape
    n_dec = x_sample.shape[0]
    depth, _, d_conv = conv_w.shape
    heads, dk = state_hgrn.shape[2], state_hgrn.shape[3]
    d_hgrn = heads * dk
    mp = batch * seq
    assert depth == 1 and x_sample.shape[1] == 1 and dk == HEAD_DIM and state_hgrn.shape[4] == dk
    assert seq % ROW_BLOCK == 0 and mp % MLP_ROW_BLOCK == 0 and ROW_BLOCK % n_dec == 0
    col_hgrn = 3 * d_conv
    col_ga = col_hgrn + 4 * d_hgrn
    w_in0 = w_in.reshape(w_in.shape[1:])
    w_a0 = w_branch_a.reshape(w_branch_a.shape[1:])
    w_b0 = w_branch_b.reshape(w_branch_b.shape[1:])
    w_out0 = w_out.reshape(w_out.shape[1:])
    w_up0 = w_up.reshape(w_up.shape[1:])
    w_down0 = w_down.reshape(w_down.shape[1:])

    xp = x_prompt.reshape(mp, d)
    xs = x_sample.reshape(n_dec, d)
    hx = _rmsnorm(xp, xs, norm_mix[0], ROW_BLOCK)

    a_p, conv_p, w_in_b = _conv_prompt(hx, w_in0, conv_w[0], batch, seq, d_conv, ROW_BLOCK, CONV_COL_BLOCK)
    st2 = state_conv[0].reshape(n_dec, 2 * d_conv)
    a_s, u_s = _conv_sample(hx, mp // n_dec, n_dec, w_in_b, conv_w[0], st2, d_conv, CONV_COL_BLOCK)

    o_p, hgrn_p, w_up_b, w_down_b = _hgrn_prompt(hx, w_in_b, col_hgrn, lb_logits, onorm_g[0], batch, seq, heads,
                                                 HGRN_ROWS, HGRN_HEADS, w_up0, w_down0)
    p_s = _proj(hx, mp // n_dec, n_dec, w_in_b, col_hgrn, 4 * d_hgrn, PROJ_COL_BLOCK)
    o_s, hgrn_s = _hgrn_sample(p_s.reshape(n_dec, 4 * heads, dk), lb_logits.reshape(2, heads, dk),
                               onorm_g[0], state_hgrn[0], HGRN_DECODE_SEQS)

    mix, w_out_b = _merge(hx, a_p, o_p, a_s, o_s.reshape(n_dec, d_hgrn), w_in_b, w_a0, w_b0,
                          col_ga, ROW_BLOCK, MERGE_COL_BLOCK, w_out0, SIDE_SLABS // 2)
    x1, h2 = _outproj(xp, xs, mix, w_out_b, norm_ffn[0], ROW_BLOCK)
    y_p, y_s = _mlp(h2, x1, w_up_b, w_down_b, norm_final, MLP_ROW_BLOCK, MLP_FF_BLOCK, n_dec)

    conv_s = jnp.stack([state_conv[0, :, 1, :], u_s], axis=1)
    return (y_p.reshape(batch, seq, d), y_s.reshape(n_dec, 1, d),
            conv_p[None], hgrn_p[None], conv_s[None], hgrn_s[None])
```

```python
import functools

import jax
import jax.numpy as jnp
from jax import lax
from jax.experimental import pallas as pl
from jax.experimental.pallas import tpu as pltpu

EPS = 1e-6
CHUNK = 64
HEAD_DIM = 128
V7X_VMEM_BYTES = 64 * 1024 * 1024
VMEM_LIMIT = V7X_VMEM_BYTES * 7 // 8

ROW_BLOCK = 512
MLP_ROW_BLOCK = 1024
MLP_FF_BLOCK = 512
CONV_COL_BLOCK = 512
CONV_SUB_BLOCK = 256
PROJ_COL_BLOCK = 1024
MERGE_COL_BLOCK = 1024
SIDE_SLABS = 64
HGRN_ROWS = 256
HGRN_HEADS = 4
HGRN_DECODE_SEQS = 8

BF16 = jnp.bfloat16
F32 = jnp.float32


def _params(semantics):
    return pltpu.CompilerParams(dimension_semantics=semantics, vmem_limit_bytes=VMEM_LIMIT)


def _dot(a, b):
    return jnp.dot(a, b, preferred_element_type=F32)


def _dot_nt(a, b):
    return lax.dot_general(a, b, (((1,), (1,)), ((), ())), preferred_element_type=F32)


def _dot_tn(a, b):
    return lax.dot_general(a, b, (((0,), (0,)), ((), ())), preferred_element_type=F32)


def _sigmoid(x):
    return jax.nn.sigmoid(x)


def _silu(x):
    return x * jax.nn.sigmoid(x)


def _rms_scale(x, g):
    ms = jnp.mean(x * x, axis=-1, keepdims=True)
    return x * lax.rsqrt(ms + EPS) * g


def _row_split(i, n_full, full_rows, tail_rows, body):
    @pl.when(i < n_full)
    def _():
        body(full_rows, False)

    @pl.when(i == n_full)
    def _():
        body(tail_rows, True)


def _slab_cast(step, n_slabs, src_ref, dst_ref):
    @pl.when(step < n_slabs)
    def _():
        dst_ref[...] = src_ref[...].astype(dst_ref.dtype)


def _slab_specs(w, n_slabs, step_of):
    r, c = w.shape
    assert r % n_slabs == 0

    def index(*g):
        return (jnp.minimum(step_of(*g), n_slabs - 1), 0)

    spec = pl.BlockSpec((r // n_slabs, c), index)
    return spec, spec, jax.ShapeDtypeStruct((r, c), BF16)


def _rmsnorm_kernel(xp_ref, xs_ref, g_ref, o_ref, *, n_full):
    tm, tail = xp_ref.shape[0], xs_ref.shape[0]

    def body(rows, is_tail):
        x = xs_ref[...] if is_tail else xp_ref[...]
        o_ref[0:rows, :] = _rms_scale(x, g_ref[...]).astype(o_ref.dtype)

    _row_split(pl.program_id(0), n_full, tm, tail, body)


def _rmsnorm(xp, xs, g, tm):
    mp, d = xp.shape
    tail = xs.shape[0]
    n_full = mp // tm
    return pl.pallas_call(
        functools.partial(_rmsnorm_kernel, n_full=n_full),
        grid=(n_full + 1,),
        in_specs=[pl.BlockSpec((tm, d), lambda i: (jnp.minimum(i, n_full - 1), 0)),
                  pl.BlockSpec((tail, d), lambda i: (0, 0)),
                  pl.BlockSpec((1, d), lambda i: (0, 0))],
        out_specs=pl.BlockSpec((tm, d), lambda i: (i, 0)),
        out_shape=jax.ShapeDtypeStruct((mp + tail, d), BF16),
        compiler_params=_params(("arbitrary",)),
        name="rmsnorm",
    )(xp, xs, g.reshape(1, d))


def _conv_prompt_kernel(hx_ref, whc_ref, wbg_ref, wcg_ref, cw_ref, side_ref, a_ref, nc_ref, side_out_ref,
                        wb_ref, carry_ref, *, n_slabs):
    c, b, t = pl.program_id(0), pl.program_id(1), pl.program_id(2)
    tm = hx_ref.shape[0]
    _slab_cast((c * pl.num_programs(1) + b) * pl.num_programs(2) + t, n_slabs, side_ref, side_out_ref)

    @pl.when((b == 0) & (t == 0))
    def _():
        wb_ref[0] = whc_ref[...].astype(BF16)
        wb_ref[1] = wbg_ref[...].astype(BF16)
        wb_ref[2] = wcg_ref[...].astype(BF16)

    @pl.when(t == 0)
    def _():
        carry_ref[...] = jnp.zeros_like(carry_ref)

    hx = hx_ref[...]
    cw = cw_ref[...]
    row = lax.broadcasted_iota(jnp.int32, (tm, CONV_SUB_BLOCK), 0)
    for sub in range(a_ref.shape[1] // CONV_SUB_BLOCK):
        cs = slice(sub * CONV_SUB_BLOCK, (sub + 1) * CONV_SUB_BLOCK)
        hc = _dot(hx, wb_ref[0, :, cs])
        bg = _dot(hx, wb_ref[1, :, cs])
        cg = _dot(hx, wb_ref[2, :, cs])
        u = cg * hc
        c0 = carry_ref[0:1, cs]
        c1 = carry_ref[1:2, cs]
        u1 = jnp.where(row == 0, c1, pltpu.roll(u, 1, 0))
        u2 = jnp.where(row == 0, c0, jnp.where(row == 1, c1, pltpu.roll(u, 2, 0)))
        conv = cw[0:1, cs] * u2 + cw[1:2, cs] * u1 + cw[2:3, cs] * u
        a_ref[:, cs] = (bg * conv).astype(a_ref.dtype)
        carry_ref[0:2, cs] = u[tm - 2:tm, :]

    @pl.when(t == pl.num_programs(2) - 1)
    def _():
        nc_ref[0] = carry_ref[0:2, :]


def _conv_prompt(hx, w_in, conv_w, batch, seq, d_conv, tm, tc):
    d = hx.shape[1]
    n_c = d_conv // tc
    n_t = seq // tm
    n_slabs = n_c * batch * n_t
    side_in, side_out, side_shape = _slab_specs(w_in, n_slabs, lambda c, b, t: (c * batch + b) * n_t + t)
    return pl.pallas_call(
        functools.partial(_conv_prompt_kernel, n_slabs=n_slabs),
        grid=(n_c, batch, n_t),
        in_specs=[pl.BlockSpec((tm, d), lambda c, b, t: (b * n_t + t, 0)),
                  pl.BlockSpec((d, tc), lambda c, b, t: (0, c)),
                  pl.BlockSpec((d, tc), lambda c, b, t: (0, n_c + c)),
                  pl.BlockSpec((d, tc), lambda c, b, t: (0, 2 * n_c + c)),
                  pl.BlockSpec((3, tc), lambda c, b, t: (0, c)),
                  side_in],
        out_specs=[pl.BlockSpec((tm, tc), lambda c, b, t: (b * n_t + t, c)),
                   pl.BlockSpec((1, 2, tc), lambda c, b, t: (b, 0, c)),
                   side_out],
        out_shape=[jax.ShapeDtypeStruct((batch * seq, d_conv), BF16),
                   jax.ShapeDtypeStruct((batch, 2, d_conv), F32),
                   side_shape],
        scratch_shapes=[pltpu.VMEM((3, d, tc), BF16), pltpu.VMEM((8, tc), F32)],
        compiler_params=_params(("arbitrary", "arbitrary", "arbitrary")),
        name="conv_prompt",
    )(hx, w_in, w_in, w_in, conv_w, w_in)


def _conv_sample_kernel(hx_ref, whc_ref, wbg_ref, wcg_ref, cw_ref, s0_ref, s1_ref, a_ref, u_ref):
    hx = hx_ref[...]
    hc = _dot(hx, whc_ref[...])
    bg = _dot(hx, wbg_ref[...])
    cg = _dot(hx, wcg_ref[...])
    u = cg * hc
    cw = cw_ref[...]
    conv = cw[0:1, :] * s0_ref[...] + cw[1:2, :] * s1_ref[...] + cw[2:3, :] * u
    a_ref[...] = (bg * conv).astype(a_ref.dtype)
    u_ref[...] = u


def _conv_sample(hx, row_block, n, w_in, conv_w, state2d, d_conv, tc):
    d = hx.shape[1]
    n_c = d_conv // tc
    return pl.pallas_call(
        _conv_sample_kernel,
        grid=(n_c,),
        in_specs=[pl.BlockSpec((n, d), lambda c: (row_block, 0)),
                  pl.BlockSpec((d, tc), lambda c: (0, c)),
                  pl.BlockSpec((d, tc), lambda c: (0, n_c + c)),
                  pl.BlockSpec((d, tc), lambda c: (0, 2 * n_c + c)),
                  pl.BlockSpec((3, tc), lambda c: (0, c)),
                  pl.BlockSpec((n, tc), lambda c: (0, c)),
                  pl.BlockSpec((n, tc), lambda c: (0, n_c + c))],
        out_specs=[pl.BlockSpec((n, tc), lambda c: (0, c)),
                   pl.BlockSpec((n, tc), lambda c: (0, c))],
        out_shape=[jax.ShapeDtypeStruct((n, d_conv), BF16),
                   jax.ShapeDtypeStruct((n, d_conv), F32)],
        compiler_params=_params(("arbitrary",)),
        name="conv_sample",
    )(hx, w_in, w_in, w_in, conv_w, state2d, state2d)


def _proj_kernel(x_ref, w_ref, o_ref):
    o_ref[...] = _dot(x_ref[...], w_ref[...])


def _proj(x, row_block, rows, w, col0, n, tn):
    d = x.shape[1]
    return pl.pallas_call(
        _proj_kernel,
        grid=(n // tn,),
        in_specs=[pl.BlockSpec((rows, d), lambda j: (row_block, 0)),
                  pl.BlockSpec((d, tn), lambda j: (0, col0 // tn + j))],
        out_specs=pl.BlockSpec((rows, tn), lambda j: (0, j)),
        out_shape=jax.ShapeDtypeStruct((rows, n), F32),
        compiler_params=_params(("arbitrary",)),
        name="hgrn_proj_decode",
    )(x, w)


def _lower_bound(lbl):
    e = jnp.exp(lbl - jnp.max(lbl, axis=0))
    return e[0] / jnp.sum(e, axis=0)


def _head_norm_gate(o, g, og):
    return _rms_scale(o, g) * _silu(og)


def _cumsum_rows(x, tril_bf16):
    hi = x.astype(BF16)
    r1 = x - hi.astype(F32)
    mid = r1.astype(BF16)
    lo = (r1 - mid.astype(F32)).astype(BF16)
    n = x.shape[1]
    parts = _dot(tril_bf16, jnp.concatenate([hi, mid, lo], axis=1))
    return parts[:, 0:n] + parts[:, n:2 * n] + parts[:, 2 * n:3 * n]


def _hgrn_prompt_kernel(hx_ref, wq_ref, wf_ref, wi_ref, wo_ref, lbl_ref, g_ref, side_a_ref, side_b_ref,
                        o_ref, s_ref, side_a_out_ref, side_b_out_ref, pa_ref, pb_ref, st_ref, *, n_slabs, n_t):
    s = pl.program_id(0)
    _slab_cast(s, n_slabs, side_a_ref, side_a_out_ref)
    _slab_cast(s, n_slabs, side_b_ref, side_b_out_ref)
    tb = hx_ref.shape[0]
    heads_per_step = st_ref.shape[0]
    dk = HEAD_DIM
    n_chunks = tb // CHUNK
    t = lax.rem(jnp.maximum(s - 1, 0), n_t)

    @pl.when(s == 0)
    def _():
        pb_ref[...] = jnp.zeros_like(pb_ref)

    @pl.when(t == 0)
    def _():
        st_ref[...] = jnp.zeros_like(st_ref)

    lb_all = _lower_bound(lbl_ref[...])
    g = g_ref[...]
    row = lax.broadcasted_iota(jnp.int32, (tb, tb), 0)
    col = lax.broadcasted_iota(jnp.int32, (tb, tb), 1)
    shift = CHUNK.bit_length() - 1
    causal = (row >= col) & (jnp.right_shift(row, shift) == jnp.right_shift(col, shift))
    tril = causal.astype(BF16)

    def body(p_next_ref, p_ref):
        heads = range(heads_per_step)
        cols = [slice(hh * dk, (hh + 1) * dk) for hh in heads]

        def project(k, w_ref):
            p_next_ref[k] = _dot(hx_ref[...], w_ref[...])

        qs, kk, lf = [], [], []
        for hh in heads:
            lb = lb_all[:, cols[hh]]
            f = lb + (1.0 - lb) * _sigmoid(p_ref[1, :, cols[hh]])
            qs.append(_silu(p_ref[0, :, cols[hh]]))
            kk.append(1.0 - f)
            lf.append(jnp.log(f))
        project(0, wq_ref)
        b = [_cumsum_rows(lf[hh], tril) for hh in heads]
        q_in, k_in, k_end, decay = [], [], [], []
        for hh in heads:
            b3 = b[hh].reshape(n_chunks, CHUNK, dk)
            b_last = b3[:, CHUNK - 1:CHUNK, :]
            b_end = jnp.broadcast_to(b_last, b3.shape).reshape(tb, dk)
            q_in.append((qs[hh] * jnp.exp(b[hh])).astype(BF16))
            k_in.append((kk[hh] * jnp.exp(-b[hh])).astype(BF16))
            k_end.append((kk[hh] * jnp.exp(b_end - b[hh])).astype(BF16))
            decay.append(jnp.exp(b_last))
        project(1, wf_ref)
        v = [p_ref[2, :, cols[hh]].astype(BF16) for hh in heads]
        raw = [_dot_nt(q_in[hh], k_in[hh]) for hh in heads]
        delta = [[_dot_tn(v[hh][c * CHUNK:(c + 1) * CHUNK], k_end[hh][c * CHUNK:(c + 1) * CHUNK])
                  for c in range(n_chunks)] for hh in heads]
        scores = [jnp.where(causal, raw[hh], 0.0).astype(BF16) for hh in heads]
        starts = []
        for hh in heads:
            st = st_ref[hh]
            per_chunk = []
            for c in range(n_chunks):
                per_chunk.append(st.astype(BF16))
                st = decay[hh][c] * st + delta[hh][c]
            st_ref[hh] = st
            starts.append(per_chunk)
        project(2, wi_ref)
        o = []
        for hh in heads:
            inter = [_dot_nt(q_in[hh][c * CHUNK:(c + 1) * CHUNK], starts[hh][c]) for c in range(n_chunks)]
            o.append(_dot(scores[hh], v[hh]) + jnp.concatenate(inter, axis=0))
        for hh in heads:
            o_ref[:, cols[hh]] = _head_norm_gate(o[hh], g, p_ref[3, :, cols[hh]]).astype(o_ref.dtype)
        project(3, wo_ref)

    parity = lax.rem(s, 2)

    @pl.when(parity == 0)
    def _():
        body(pa_ref, pb_ref)

    @pl.when(parity == 1)
    def _():
        body(pb_ref, pa_ref)

    @pl.when(t == n_t - 1)
    def _():
        for hh in range(heads_per_step):
            s_ref[0, hh] = st_ref[hh].T


def _hgrn_prompt(hx, w_in, col0, lb_logits, onorm_g, batch, seq, heads, tb, hp, side_a, side_b):
    d = hx.shape[1]
    n_t = seq // tb
    n_h = heads // hp
    dk = HEAD_DIM
    wc = hp * dk
    assert CHUNK & (CHUNK - 1) == 0 and col0 % wc == 0
    rows_blocks = batch * n_t
    n_steps = n_h * rows_blocks
    proj = lambda s: jnp.minimum(s, n_steps - 1)
    rec = lambda s: jnp.maximum(s - 1, 0)
    a_in, a_out, a_shape = _slab_specs(side_a, n_steps, lambda s: s)
    b_in, b_out, b_shape = _slab_specs(side_b, n_steps, lambda s: s)

    def wcol(k):
        return pl.BlockSpec((d, wc), lambda s: (0, col0 // wc + k * n_h + proj(s) // rows_blocks))

    return pl.pallas_call(
        functools.partial(_hgrn_prompt_kernel, n_slabs=n_steps, n_t=n_t),
        grid=(n_steps + 1,),
        in_specs=[pl.BlockSpec((tb, d), lambda s: (proj(s) % rows_blocks, 0)),
                  wcol(0), wcol(1), wcol(2), wcol(3),
                  pl.BlockSpec((2, 1, wc), lambda s: (0, 0, rec(s) // rows_blocks)),
                  pl.BlockSpec((1, dk), lambda s: (0, 0)),
                  a_in, b_in],
        out_specs=[pl.BlockSpec((tb, wc), lambda s: (rec(s) % rows_blocks, rec(s) // rows_blocks)),
                   pl.BlockSpec((1, hp, dk, dk),
                                lambda s: ((rec(s) % rows_blocks) // n_t, rec(s) // rows_blocks, 0, 0)),
                   a_out, b_out],
        out_shape=[jax.ShapeDtypeStruct((batch * seq, heads * dk), BF16),
                   jax.ShapeDtypeStruct((batch, heads, dk, dk), F32),
                   a_shape, b_shape],
        scratch_shapes=[pltpu.VMEM((4, tb, wc), F32), pltpu.VMEM((4, tb, wc), F32),
                        pltpu.VMEM((hp, dk, dk), F32)],
        compiler_params=_params(("arbitrary",)),
        name="hgrn_prompt",
    )(hx, w_in, w_in, w_in, w_in, lb_logits.reshape(2, 1, heads * dk), onorm_g.reshape(1, dk), side_a, side_b)


def _hgrn_sample_kernel(p_ref, lbl_ref, g_ref, s0_ref, o_ref, sn_ref):
    bb, heads, dk = s0_ref.shape[0], s0_ref.shape[1], s0_ref.shape[2]
    lb = _lower_bound(lbl_ref[...])
    g = g_ref[...]
    head_row = lax.broadcasted_iota(jnp.int32, (heads, dk), 0)

    def column(row):
        return jnp.broadcast_to(row, (dk, dk)).T

    for i in range(bb):
        p = p_ref[i]
        qs = _silu(p[0:heads])
        f = lb + (1.0 - lb) * _sigmoid(p[heads:2 * heads])
        kk = 1.0 - f
        v = p[2 * heads:3 * heads]
        og = p[3 * heads:4 * heads]
        q_in = (qs * f).astype(BF16)
        o_inter = jnp.zeros((heads, dk), F32)
        for h in range(heads):
            s = s0_ref[i, h]
            sn_ref[i, h] = column(f[h:h + 1, :]) * s + column(kk[h:h + 1, :]) * v[h:h + 1, :]
            o_inter = o_inter + jnp.where(head_row == h, _dot(q_in, s.astype(BF16)), 0.0)
        o = jnp.sum(qs * kk, axis=-1, keepdims=True) * v + o_inter
        o_ref[i] = _head_norm_gate(o, g, og)


def _hgrn_sample(p3, lbl3, onorm_g, state, bb):
    n, rows, dk = p3.shape
    heads = rows // 4
    assert n % bb == 0
    return pl.pallas_call(
        _hgrn_sample_kernel,
        grid=(n // bb,),
        in_specs=[pl.BlockSpec((bb, rows, dk), lambda b: (b, 0, 0)),
                  pl.BlockSpec((2, heads, dk), lambda b: (0, 0, 0)),
                  pl.BlockSpec((1, dk), lambda b: (0, 0)),
                  pl.BlockSpec((bb, heads, dk, dk), lambda b: (b, 0, 0, 0))],
        out_specs=[pl.BlockSpec((bb, heads, dk), lambda b: (b, 0, 0)),
                   pl.BlockSpec((bb, heads, dk, dk), lambda b: (b, 0, 0, 0))],
        out_shape=[jax.ShapeDtypeStruct((n, heads, dk), F32),
                   jax.ShapeDtypeStruct((n, heads, dk, dk), F32)],
        compiler_params=_params(("parallel",)),
        name="hgrn_sample",
    )(p3, lbl3, onorm_g.reshape(1, dk), state)


def _merge_kernel(hx_ref, a_ref, o_ref, at_ref, ot_ref, wga_ref, wgb_ref, wa_ref, wb_ref, side_ref,
                  mix_ref, side_out_ref, wabb_ref, *, n_full, n_slabs):
    j, i = pl.program_id(0), pl.program_id(1)
    _slab_cast(j * pl.num_programs(1) + i, n_slabs, side_ref, side_out_ref)

    @pl.when(i == 0)
    def _():
        wabb_ref[0] = wa_ref[...].astype(BF16)
        wabb_ref[1] = wb_ref[...].astype(BF16)

    def body(rows, is_tail):
        hx = hx_ref[0:rows, :]
        a = at_ref[...] if is_tail else a_ref[...]
        o = ot_ref[...].astype(BF16) if is_tail else o_ref[...]
        ga = _sigmoid(_dot(hx, wga_ref[...]))
        gb = _sigmoid(_dot(hx, wgb_ref[...]))
        mix = ga * _dot(a, wabb_ref[0]) + gb * _dot(o, wabb_ref[1])
        mix_ref[0:rows, :] = mix.astype(mix_ref.dtype)

    _row_split(i, n_full, hx_ref.shape[0], at_ref.shape[0], body)


def _merge(hx, a, o, a_tail, o_tail, w_in, w_a, w_b, col_ga, tm, tn, side_w, n_slabs):
    m, d = hx.shape
    dc, dh = a.shape[1], o.shape[1]
    tail = a_tail.shape[0]
    n_full = a.shape[0] // tm
    n_n = d // tn
    clamp = lambda j, i: (jnp.minimum(i, n_full - 1), 0)
    assert n_slabs <= n_n * (n_full + 1)
    side_in, side_out, side_shape = _slab_specs(side_w, n_slabs, lambda j, i: j * (n_full + 1) + i)
    return pl.pallas_call(
        functools.partial(_merge_kernel, n_full=n_full, n_slabs=n_slabs),
        grid=(n_n, n_full + 1),
        in_specs=[pl.BlockSpec((tm, d), lambda j, i: (i, 0)),
                  pl.BlockSpec((tm, dc), clamp),
                  pl.BlockSpec((tm, dh), clamp),
                  pl.BlockSpec((tail, dc), lambda j, i: (0, 0)),
                  pl.BlockSpec((tail, dh), lambda j, i: (0, 0)),
                  pl.BlockSpec((d, tn), lambda j, i: (0, col_ga // tn + j)),
                  pl.BlockSpec((d, tn), lambda j, i: (0, col_ga // tn + n_n + j)),
                  pl.BlockSpec((dc, tn), lambda j, i: (0, j)),
                  pl.BlockSpec((dh, tn), lambda j, i: (0, j)),
                  side_in],
        out_specs=[pl.BlockSpec((tm, tn), lambda j, i: (i, j)), side_out],
        out_shape=[jax.ShapeDtypeStruct((m, d), BF16), side_shape],
        scratch_shapes=[pltpu.VMEM((2, dc, tn), BF16)],
        compiler_params=_params(("arbitrary", "arbitrary")),
        name="merge",
    )(hx, a, o, a_tail, o_tail, w_in, w_in, w_a, w_b, side_w)


def _outproj_kernel(xp_ref, xs_ref, mix_ref, w_ref, g_ref, x1_ref, h2_ref, *, n_full):
    i = pl.program_id(0)

    def body(rows, is_tail):
        x = xs_ref[...] if is_tail else xp_ref[...]
        x1 = x + _dot(mix_ref[0:rows, :], w_ref[...])
        x1_ref[0:rows, :] = x1
        h2_ref[0:rows, :] = _rms_scale(x1, g_ref[...]).astype(h2_ref.dtype)

    _row_split(i, n_full, xp_ref.shape[0], xs_ref.shape[0], body)


def _outproj(xp, xs, mix, w_out, g, tm):
    mp, d = xp.shape
    tail = xs.shape[0]
    n_full = mp // tm
    return pl.pallas_call(
        functools.partial(_outproj_kernel, n_full=n_full),
        grid=(n_full + 1,),
        in_specs=[pl.BlockSpec((tm, d), lambda i: (jnp.minimum(i, n_full - 1), 0)),
                  pl.BlockSpec((tail, d), lambda i: (0, 0)),
                  pl.BlockSpec((tm, d), lambda i: (i, 0)),
                  pl.BlockSpec((d, d), lambda i: (0, 0)),
                  pl.BlockSpec((1, d), lambda i: (0, 0))],
        out_specs=[pl.BlockSpec((tm, d), lambda i: (i, 0)),
                   pl.BlockSpec((tm, d), lambda i: (i, 0))],
        out_shape=[jax.ShapeDtypeStruct((mp + tail, d), F32),
                   jax.ShapeDtypeStruct((mp + tail, d), BF16)],
        compiler_params=_params(("arbitrary",)),
        name="outproj",
    )(xp, xs, mix, w_out, g.reshape(1, d))


def _mlp_kernel(h2_ref, x1_ref, wup_ref, wdn_ref, g_ref, yp_ref, ys_ref, *, n_full):
    i, j = pl.program_id(0), pl.program_id(1)
    last = pl.num_programs(1) - 1

    def body(rows, is_tail):
        y_ref = ys_ref if is_tail else yp_ref

        @pl.when(j == 0)
        def _():
            y_ref[...] = x1_ref[0:rows, :]

        h = jnp.maximum(_dot(h2_ref[0:rows, :], wup_ref[...]), 0.0)
        y_ref[...] += _dot((h * h).astype(BF16), wdn_ref[...])

        @pl.when(j == last)
        def _():
            y_ref[...] = _rms_scale(y_ref[...], g_ref[...])

    _row_split(i, n_full, yp_ref.shape[0], ys_ref.shape[0], body)


def _mlp(h2, x1, w_up, w_down, g, tm, tf, tail):
    m, d = h2.shape
    dff = w_up.shape[1]
    n_full = (m - tail) // tm
    return pl.pallas_call(
        functools.partial(_mlp_kernel, n_full=n_full),
        grid=(n_full + 1, dff // tf),
        in_specs=[pl.BlockSpec((tm, d), lambda i, j: (i, 0)),
                  pl.BlockSpec((tm, d), lambda i, j: (i, 0)),
                  pl.BlockSpec((d, tf), lambda i, j: (0, j)),
                  pl.BlockSpec((tf, d), lambda i, j: (j, 0)),
                  pl.BlockSpec((1, d), lambda i, j: (0, 0))],
        out_specs=[pl.BlockSpec((tm, d), lambda i, j: (jnp.minimum(i, n_full - 1), 0)),
                   pl.BlockSpec((tail, d), lambda i, j: (0, 0))],
        out_shape=[jax.ShapeDtypeStruct((m - tail, d), F32),
                   jax.ShapeDtypeStruct((tail, d), F32)],
        compiler_params=_params(("arbitrary", "arbitrary")),
        name="mlp",
    )(h2, x1, w_up, w_down, g.reshape(1, d))


def kernel(x_prompt, x_sample, state_conv, state_hgrn, norm_mix, w_in, conv_w, lb_logits, onorm_g,
           w_branch_a, w_branch_b, w_out, norm_ffn, w_up, w_down, norm_final):
    batch, seq, d = x_prompt.shape
    n_dec = x_sample.shape[0]
    depth, _, d_conv = conv_w.shape
    heads, dk = state_hgrn.shape[2], state_hgrn.shape[3]
    d_hgrn = heads * dk
    mp = batch * seq
    assert depth == 1 and x_sample.shape[1] == 1 and dk == HEAD_DIM and state_hgrn.shape[4] == dk
    assert seq % ROW_BLOCK == 0 and mp % MLP_ROW_BLOCK == 0 and ROW_BLOCK % n_dec == 0
    col_hgrn = 3 * d_conv
    col_ga = col_hgrn + 4 * d_hgrn
    w_in0 = w_in.reshape(w_in.shape[1:])
    w_a0 = w_branch_a.reshape(w_branch_a.shape[1:])
    w_b0 = w_branch_b.reshape(w_branch_b.shape[1:])
    w_out0 = w_out.reshape(w_out.shape[1:])
    w_up0 = w_up.reshape(w_up.shape[1:])
    w_down0 = w_down.reshape(w_down.shape[1:])

    xp = x_prompt.reshape(mp, d)
    xs = x_sample.reshape(n_dec, d)
    hx = _rmsnorm(xp, xs, norm_mix[0], ROW_BLOCK)

    a_p, conv_p, w_in_b = _conv_prompt(hx, w_in0, conv_w[0], batch, seq, d_conv, ROW_BLOCK, CONV_COL_BLOCK)
    st2 = state_conv[0].reshape(n_dec, 2 * d_conv)
    a_s, u_s = _conv_sample(hx, mp // n_dec, n_dec, w_in_b, conv_w[0], st2, d_conv, CONV_COL_BLOCK)

    o_p, hgrn_p, w_up_b, w_down_b = _hgrn_prompt(hx, w_in_b, col_hgrn, lb_logits, onorm_g[0], batch, seq, heads,
                                                 HGRN_ROWS, HGRN_HEADS, w_up0, w_down0)
    p_s = _proj(hx, mp // n_dec, n_dec, w_in_b, col_hgrn, 4 * d_hgrn, PROJ_COL_BLOCK)
    o_s, hgrn_s = _hgrn_sample(p_s.reshape(n_dec, 4 * heads, dk), lb_logits.reshape(2, heads, dk),
                               onorm_g[0], state_hgrn[0], HGRN_DECODE_SEQS)

    mix, w_out_b = _merge(hx, a_p, o_p, a_s, o_s.reshape(n_dec, d_hgrn), w_in_b, w_a0, w_b0,
                          col_ga, ROW_BLOCK, MERGE_COL_BLOCK, w_out0, SIDE_SLABS // 2)
    x1, h2 = _outproj(xp, xs, mix, w_out_b, norm_ffn[0], ROW_BLOCK)
    y_p, y_s = _mlp(h2, x1, w_up_b, w_down_b, norm_final, MLP_ROW_BLOCK, MLP_FF_BLOCK, n_dec)

    conv_s = jnp.stack([state_conv[0, :, 1, :], u_s], axis=1)
    return (y_p.reshape(batch, seq, d), y_s.reshape(n_dec, 1, d),
            conv_p[None], hgrn_p[None], conv_s[None], hgrn_s[None])
```

```python
import functools

import jax
import jax.numpy as jnp
from jax import lax
from jax.experimental import pallas as pl
from jax.experimental.pallas import tpu as pltpu

EPS = 1e-6
CHUNK = 64
HEAD_DIM = 128
V7X_VMEM_BYTES = 64 * 1024 * 1024
VMEM_LIMIT = V7X_VMEM_BYTES * 7 // 8

ROW_BLOCK = 512
MLP_ROW_BLOCK = 1024
MLP_FF_BLOCK = 512
CONV_COL_BLOCK = 512
CONV_SUB_BLOCK = 256
PROJ_COL_BLOCK = 1024
MERGE_COL_BLOCK = 1024
SIDE_SLABS = 64
HGRN_ROWS = 256
HGRN_HEADS = 4

BF16 = jnp.bfloat16
F32 = jnp.float32


def _params(semantics):
    return pltpu.CompilerParams(dimension_semantics=semantics, vmem_limit_bytes=VMEM_LIMIT)


def _dot(a, b):
    return jnp.dot(a, b, preferred_element_type=F32)


def _dot_nt(a, b):
    return lax.dot_general(a, b, (((1,), (1,)), ((), ())), preferred_element_type=F32)


def _dot_tn(a, b):
    return lax.dot_general(a, b, (((0,), (0,)), ((), ())), preferred_element_type=F32)


def _sigmoid(x):
    return jax.nn.sigmoid(x)


def _silu(x):
    return x * jax.nn.sigmoid(x)


def _rms_scale(x, g):
    ms = jnp.mean(x * x, axis=-1, keepdims=True)
    return x * lax.rsqrt(ms + EPS) * g


def _row_split(i, n_full, full_rows, tail_rows, body):
    @pl.when(i < n_full)
    def _():
        body(full_rows, False)

    @pl.when(i == n_full)
    def _():
        body(tail_rows, True)


def _slab_cast(step, n_slabs, src_ref, dst_ref):
    @pl.when(step < n_slabs)
    def _():
        dst_ref[...] = src_ref[...].astype(dst_ref.dtype)


def _slab_specs(w, n_slabs, step_of):
    r, c = w.shape
    assert r % n_slabs == 0

    def index(*g):
        return (jnp.minimum(step_of(*g), n_slabs - 1), 0)

    spec = pl.BlockSpec((r // n_slabs, c), index)
    return spec, spec, jax.ShapeDtypeStruct((r, c), BF16)


def _rmsnorm_kernel(xp_ref, xs_ref, g_ref, o_ref, *, n_full):
    tm, tail = xp_ref.shape[0], xs_ref.shape[0]

    def body(rows, is_tail):
        x = xs_ref[...] if is_tail else xp_ref[...]
        o_ref[0:rows, :] = _rms_scale(x, g_ref[...]).astype(o_ref.dtype)

    _row_split(pl.program_id(0), n_full, tm, tail, body)


def _rmsnorm(xp, xs, g, tm):
    mp, d = xp.shape
    tail = xs.shape[0]
    n_full = mp // tm
    return pl.pallas_call(
        functools.partial(_rmsnorm_kernel, n_full=n_full),
        grid=(n_full + 1,),
        in_specs=[pl.BlockSpec((tm, d), lambda i: (jnp.minimum(i, n_full - 1), 0)),
                  pl.BlockSpec((tail, d), lambda i: (0, 0)),
                  pl.BlockSpec((1, d), lambda i: (0, 0))],
        out_specs=pl.BlockSpec((tm, d), lambda i: (i, 0)),
        out_shape=jax.ShapeDtypeStruct((mp + tail, d), BF16),
        compiler_params=_params(("arbitrary",)),
        name="rmsnorm",
    )(xp, xs, g.reshape(1, d))


def _conv_prompt_kernel(hx_ref, whc_ref, wbg_ref, wcg_ref, cw_ref, side_ref, a_ref, nc_ref, side_out_ref,
                        wb_ref, carry_ref, *, n_slabs):
    c, b, t = pl.program_id(0), pl.program_id(1), pl.program_id(2)
    tm = hx_ref.shape[0]
    _slab_cast((c * pl.num_programs(1) + b) * pl.num_programs(2) + t, n_slabs, side_ref, side_out_ref)

    @pl.when((b == 0) & (t == 0))
    def _():
        wb_ref[0] = whc_ref[...].astype(BF16)
        wb_ref[1] = wbg_ref[...].astype(BF16)
        wb_ref[2] = wcg_ref[...].astype(BF16)

    @pl.when(t == 0)
    def _():
        carry_ref[...] = jnp.zeros_like(carry_ref)

    hx = hx_ref[...]
    cw = cw_ref[...]
    row = lax.broadcasted_iota(jnp.int32, (tm, CONV_SUB_BLOCK), 0)
    for sub in range(a_ref.shape[1] // CONV_SUB_BLOCK):
        cs = slice(sub * CONV_SUB_BLOCK, (sub + 1) * CONV_SUB_BLOCK)
        hc = _dot(hx, wb_ref[0, :, cs])
        bg = _dot(hx, wb_ref[1, :, cs])
        cg = _dot(hx, wb_ref[2, :, cs])
        u = cg * hc
        c0 = carry_ref[0:1, cs]
        c1 = carry_ref[1:2, cs]
        u1 = jnp.where(row == 0, c1, pltpu.roll(u, 1, 0))
        u2 = jnp.where(row == 0, c0, jnp.where(row == 1, c1, pltpu.roll(u, 2, 0)))
        conv = cw[0:1, cs] * u2 + cw[1:2, cs] * u1 + cw[2:3, cs] * u
        a_ref[:, cs] = (bg * conv).astype(a_ref.dtype)
        carry_ref[0:2, cs] = u[tm - 2:tm, :]

    @pl.when(t == pl.num_programs(2) - 1)
    def _():
        nc_ref[0] = carry_ref[0:2, :]


def _conv_prompt(hx, w_in, conv_w, batch, seq, d_conv, tm, tc):
    d = hx.shape[1]
    n_c = d_conv // tc
    n_t = seq // tm
    n_slabs = n_c * batch * n_t
    side_in, side_out, side_shape = _slab_specs(w_in, n_slabs, lambda c, b, t: (c * batch + b) * n_t + t)
    return pl.pallas_call(
        functools.partial(_conv_prompt_kernel, n_slabs=n_slabs),
        grid=(n_c, batch, n_t),
        in_specs=[pl.BlockSpec((tm, d), lambda c, b, t: (b * n_t + t, 0)),
                  pl.BlockSpec((d, tc), lambda c, b, t: (0, c)),
                  pl.BlockSpec((d, tc), lambda c, b, t: (0, n_c + c)),
                  pl.BlockSpec((d, tc), lambda c, b, t: (0, 2 * n_c + c)),
                  pl.BlockSpec((3, tc), lambda c, b, t: (0, c)),
                  side_in],
        out_specs=[pl.BlockSpec((tm, tc), lambda c, b, t: (b * n_t + t, c)),
                   pl.BlockSpec((1, 2, tc), lambda c, b, t: (b, 0, c)),
                   side_out],
        out_shape=[jax.ShapeDtypeStruct((batch * seq, d_conv), BF16),
                   jax.ShapeDtypeStruct((batch, 2, d_conv), F32),
                   side_shape],
        scratch_shapes=[pltpu.VMEM((3, d, tc), BF16), pltpu.VMEM((8, tc), F32)],
        compiler_params=_params(("arbitrary", "arbitrary", "arbitrary")),
        name="conv_prompt",
    )(hx, w_in, w_in, w_in, conv_w, w_in)


def _conv_sample_kernel(hx_ref, whc_ref, wbg_ref, wcg_ref, cw_ref, s0_ref, s1_ref, a_ref, u_ref):
    hx = hx_ref[...]
    hc = _dot(hx, whc_ref[...])
    bg = _dot(hx, wbg_ref[...])
    cg = _dot(hx, wcg_ref[...])
    u = cg * hc
    cw = cw_ref[...]
    conv = cw[0:1, :] * s0_ref[...] + cw[1:2, :] * s1_ref[...] + cw[2:3, :] * u
    a_ref[...] = (bg * conv).astype(a_ref.dtype)
    u_ref[...] = u


def _conv_sample(hx, row_block, n, w_in, conv_w, state2d, d_conv, tc):
    d = hx.shape[1]
    n_c = d_conv // tc
    return pl.pallas_call(
        _conv_sample_kernel,
        grid=(n_c,),
        in_specs=[pl.BlockSpec((n, d), lambda c: (row_block, 0)),
                  pl.BlockSpec((d, tc), lambda c: (0, c)),
                  pl.BlockSpec((d, tc), lambda c: (0, n_c + c)),
                  pl.BlockSpec((d, tc), lambda c: (0, 2 * n_c + c)),
                  pl.BlockSpec((3, tc), lambda c: (0, c)),
                  pl.BlockSpec((n, tc), lambda c: (0, c)),
                  pl.BlockSpec((n, tc), lambda c: (0, n_c + c))],
        out_specs=[pl.BlockSpec((n, tc), lambda c: (0, c)),
                   pl.BlockSpec((n, tc), lambda c: (0, c))],
        out_shape=[jax.ShapeDtypeStruct((n, d_conv), BF16),
                   jax.ShapeDtypeStruct((n, d_conv), F32)],
        compiler_params=_params(("arbitrary",)),
        name="conv_sample",
    )(hx, w_in, w_in, w_in, conv_w, state2d, state2d)


def _proj_kernel(x_ref, w_ref, o_ref):
    o_ref[...] = _dot(x_ref[...], w_ref[...])


def _proj(x, row_block, rows, w, col0, n, tn):
    d = x.shape[1]
    return pl.pallas_call(
        _proj_kernel,
        grid=(n // tn,),
        in_specs=[pl.BlockSpec((rows, d), lambda j: (row_block, 0)),
                  pl.BlockSpec((d, tn), lambda j: (0, col0 // tn + j))],
        out_specs=pl.BlockSpec((rows, tn), lambda j: (0, j)),
        out_shape=jax.ShapeDtypeStruct((rows, n), F32),
        compiler_params=_params(("arbitrary",)),
        name="hgrn_proj_decode",
    )(x, w)


def _lower_bound(lbl):
    e = jnp.exp(lbl - jnp.max(lbl, axis=0))
    return e[0] / jnp.sum(e, axis=0)


def _head_norm_gate(o, g, og):
    return _rms_scale(o, g) * _silu(og)


def _cumsum_rows(x, tril_bf16):
    hi = x.astype(BF16)
    r1 = x - hi.astype(F32)
    mid = r1.astype(BF16)
    lo = (r1 - mid.astype(F32)).astype(BF16)
    n = x.shape[1]
    parts = _dot(tril_bf16, jnp.concatenate([hi, mid, lo], axis=1))
    return parts[:, 0:n] + parts[:, n:2 * n] + parts[:, 2 * n:3 * n]


def _hgrn_decode_step(p_ref, lb, g, s0_ref, o_ref, sn_ref):
    bb, heads, dk = s0_ref.shape[0], s0_ref.shape[1], s0_ref.shape[2]
    head_row = lax.broadcasted_iota(jnp.int32, (heads, dk), 0)

    def column(row):
        return jnp.broadcast_to(row, (dk, dk)).T

    for i in range(bb):
        p = p_ref[i]
        qs = _silu(p[0:heads])
        f = lb + (1.0 - lb) * _sigmoid(p[heads:2 * heads])
        kk = 1.0 - f
        v = p[2 * heads:3 * heads]
        og = p[3 * heads:4 * heads]
        q_in = (qs * f).astype(BF16)
        o_inter = jnp.zeros((heads, dk), F32)
        for h in range(heads):
            s = s0_ref[i, h]
            sn_ref[i, h] = column(f[h:h + 1, :]) * s + column(kk[h:h + 1, :]) * v[h:h + 1, :]
            o_inter = o_inter + jnp.where(head_row == h, _dot(q_in, s.astype(BF16)), 0.0)
        o = jnp.sum(qs * kk, axis=-1, keepdims=True) * v + o_inter
        o_ref[i] = _head_norm_gate(o, g, og)


def _hgrn_prompt_kernel(hx_ref, wq_ref, wf_ref, wi_ref, wo_ref, lbl_ref, g_ref, side_a_ref, side_b_ref,
                        dp_ref, dlbl_ref, ds0_ref,
                        o_ref, s_ref, side_a_out_ref, side_b_out_ref, do_ref, dsn_ref,
                        pa_ref, pb_ref, st_ref, *, n_slabs, n_t):
    s = pl.program_id(0)
    _slab_cast(s, n_slabs, side_a_ref, side_a_out_ref)
    _slab_cast(s, n_slabs, side_b_ref, side_b_out_ref)
    tb = hx_ref.shape[0]
    heads_per_step = st_ref.shape[0]
    dk = HEAD_DIM
    n_chunks = tb // CHUNK
    t = lax.rem(jnp.maximum(s - 1, 0), n_t)

    @pl.when(s == 0)
    def _():
        pb_ref[...] = jnp.zeros_like(pb_ref)

    @pl.when(t == 0)
    def _():
        st_ref[...] = jnp.zeros_like(st_ref)

    lb_all = _lower_bound(lbl_ref[...])
    g = g_ref[...]
    row = lax.broadcasted_iota(jnp.int32, (tb, tb), 0)
    col = lax.broadcasted_iota(jnp.int32, (tb, tb), 1)
    shift = CHUNK.bit_length() - 1
    causal = (row >= col) & (jnp.right_shift(row, shift) == jnp.right_shift(col, shift))
    tril = causal.astype(BF16)

    def body(p_next_ref, p_ref):
        heads = range(heads_per_step)
        cols = [slice(hh * dk, (hh + 1) * dk) for hh in heads]

        def project(k, w_ref):
            p_next_ref[k] = _dot(hx_ref[...], w_ref[...])

        qs, kk, lf = [], [], []
        for hh in heads:
            lb = lb_all[:, cols[hh]]
            f = lb + (1.0 - lb) * _sigmoid(p_ref[1, :, cols[hh]])
            qs.append(_silu(p_ref[0, :, cols[hh]]))
            kk.append(1.0 - f)
            lf.append(jnp.log(f))
        project(0, wq_ref)
        b = [_cumsum_rows(lf[hh], tril) for hh in heads]
        q_in, k_in, k_end, decay = [], [], [], []
        for hh in heads:
            b3 = b[hh].reshape(n_chunks, CHUNK, dk)
            b_last = b3[:, CHUNK - 1:CHUNK, :]
            b_end = jnp.broadcast_to(b_last, b3.shape).reshape(tb, dk)
            q_in.append((qs[hh] * jnp.exp(b[hh])).astype(BF16))
            k_in.append((kk[hh] * jnp.exp(-b[hh])).astype(BF16))
            k_end.append((kk[hh] * jnp.exp(b_end - b[hh])).astype(BF16))
            decay.append(jnp.exp(b_last))
        project(1, wf_ref)
        v = [p_ref[2, :, cols[hh]].astype(BF16) for hh in heads]
        raw = [_dot_nt(q_in[hh], k_in[hh]) for hh in heads]
        delta = [[_dot_tn(v[hh][c * CHUNK:(c + 1) * CHUNK], k_end[hh][c * CHUNK:(c + 1) * CHUNK])
                  for c in range(n_chunks)] for hh in heads]
        scores = [jnp.where(causal, raw[hh], 0.0).astype(BF16) for hh in heads]
        starts = []
        for hh in heads:
            st = st_ref[hh]
            per_chunk = []
            for c in range(n_chunks):
                per_chunk.append(st.astype(BF16))
                st = decay[hh][c] * st + delta[hh][c]
            st_ref[hh] = st
            starts.append(per_chunk)
        project(2, wi_ref)
        _hgrn_decode_step(dp_ref, _lower_bound(dlbl_ref[...]), g, ds0_ref, do_ref, dsn_ref)
        o = []
        for hh in heads:
            inter = [_dot_nt(q_in[hh][c * CHUNK:(c + 1) * CHUNK], starts[hh][c]) for c in range(n_chunks)]
            o.append(_dot(scores[hh], v[hh]) + jnp.concatenate(inter, axis=0))
        for hh in heads:
            o_ref[:, cols[hh]] = _head_norm_gate(o[hh], g, p_ref[3, :, cols[hh]]).astype(o_ref.dtype)
        project(3, wo_ref)

    parity = lax.rem(s, 2)

    @pl.when(parity == 0)
    def _():
        body(pa_ref, pb_ref)

    @pl.when(parity == 1)
    def _():
        body(pb_ref, pa_ref)

    @pl.when(t == n_t - 1)
    def _():
        for hh in range(heads_per_step):
            s_ref[0, hh] = st_ref[hh].T


def _hgrn_prompt(hx, w_in, col0, lb_logits, onorm_g, batch, seq, heads, tb, hp, side_a, side_b, dec_p, dec_state):
    d = hx.shape[1]
    n_t = seq // tb
    n_h = heads // hp
    dk = HEAD_DIM
    wc = hp * dk
    assert CHUNK & (CHUNK - 1) == 0 and col0 % wc == 0
    rows_blocks = batch * n_t
    n_steps = n_h * rows_blocks
    proj = lambda s: jnp.minimum(s, n_steps - 1)
    rec = lambda s: jnp.maximum(s - 1, 0)
    a_in, a_out, a_shape = _slab_specs(side_a, n_steps, lambda s: s)
    b_in, b_out, b_shape = _slab_specs(side_b, n_steps, lambda s: s)

    n_dec, dec_rows = dec_p.shape[0], dec_p.shape[1]
    assert n_dec % n_steps == 0
    db = n_dec // n_steps
    dec = lambda s: jnp.minimum(s, n_steps - 1)

    def wcol(k):
        return pl.BlockSpec((d, wc), lambda s: (0, col0 // wc + k * n_h + proj(s) // rows_blocks))

    return pl.pallas_call(
        functools.partial(_hgrn_prompt_kernel, n_slabs=n_steps, n_t=n_t),
        grid=(n_steps + 1,),
        in_specs=[pl.BlockSpec((tb, d), lambda s: (proj(s) % rows_blocks, 0)),
                  wcol(0), wcol(1), wcol(2), wcol(3),
                  pl.BlockSpec((2, 1, wc), lambda s: (0, 0, rec(s) // rows_blocks)),
                  pl.BlockSpec((1, dk), lambda s: (0, 0)),
                  a_in, b_in,
                  pl.BlockSpec((db, dec_rows, dk), lambda s: (dec(s), 0, 0)),
                  pl.BlockSpec((2, heads, dk), lambda s: (0, 0, 0)),
                  pl.BlockSpec((db, heads, dk, dk), lambda s: (dec(s), 0, 0, 0))],
        out_specs=[pl.BlockSpec((tb, wc), lambda s: (rec(s) % rows_blocks, rec(s) // rows_blocks)),
                   pl.BlockSpec((1, hp, dk, dk),
                                lambda s: ((rec(s) % rows_blocks) // n_t, rec(s) // rows_blocks, 0, 0)),
                   a_out, b_out,
                   pl.BlockSpec((db, heads, dk), lambda s: (dec(s), 0, 0)),
                   pl.BlockSpec((db, heads, dk, dk), lambda s: (dec(s), 0, 0, 0))],
        out_shape=[jax.ShapeDtypeStruct((batch * seq, heads * dk), BF16),
                   jax.ShapeDtypeStruct((batch, heads, dk, dk), F32),
                   a_shape, b_shape,
                   jax.ShapeDtypeStruct((n_dec, heads, dk), F32),
                   jax.ShapeDtypeStruct((n_dec, heads, dk, dk), F32)],
        scratch_shapes=[pltpu.VMEM((4, tb, wc), F32), pltpu.VMEM((4, tb, wc), F32),
                        pltpu.VMEM((hp, dk, dk), F32)],
        compiler_params=_params(("arbitrary",)),
        name="hgrn_prompt",
    )(hx, w_in, w_in, w_in, w_in, lb_logits.reshape(2, 1, heads * dk), onorm_g.reshape(1, dk), side_a, side_b,
      dec_p, lb_logits.reshape(2, heads, dk), dec_state)


def _merge_kernel(hx_ref, a_ref, o_ref, at_ref, ot_ref, wga_ref, wgb_ref, wa_ref, wb_ref, side_ref,
                  mix_ref, side_out_ref, wabb_ref, *, n_full, n_slabs):
    j, i = pl.program_id(0), pl.program_id(1)
    _slab_cast(j * pl.num_programs(1) + i, n_slabs, side_ref, side_out_ref)

    @pl.when(i == 0)
    def _():
        wabb_ref[0] = wa_ref[...].astype(BF16)
        wabb_ref[1] = wb_ref[...].astype(BF16)

    def body(rows, is_tail):
        hx = hx_ref[0:rows, :]
        a = at_ref[...] if is_tail else a_ref[...]
        o = ot_ref[...].astype(BF16) if is_tail else o_ref[...]
        ga = _sigmoid(_dot(hx, wga_ref[...]))
        gb = _sigmoid(_dot(hx, wgb_ref[...]))
        mix = ga * _dot(a, wabb_ref[0]) + gb * _dot(o, wabb_ref[1])
        mix_ref[0:rows, :] = mix.astype(mix_ref.dtype)

    _row_split(i, n_full, hx_ref.shape[0], at_ref.shape[0], body)


def _merge(hx, a, o, a_tail, o_tail, w_in, w_a, w_b, col_ga, tm, tn, side_w, n_slabs):
    m, d = hx.shape
    dc, dh = a.shape[1], o.shape[1]
    tail = a_tail.shape[0]
    n_full = a.shape[0] // tm
    n_n = d // tn
    clamp = lambda j, i: (jnp.minimum(i, n_full - 1), 0)
    assert n_slabs <= n_n * (n_full + 1)
    side_in, side_out, side_shape = _slab_specs(side_w, n_slabs, lambda j, i: j * (n_full + 1) + i)
    return pl.pallas_call(
        functools.partial(_merge_kernel, n_full=n_full, n_slabs=n_slabs),
        grid=(n_n, n_full + 1),
        in_specs=[pl.BlockSpec((tm, d), lambda j, i: (i, 0)),
                  pl.BlockSpec((tm, dc), clamp),
                  pl.BlockSpec((tm, dh), clamp),
                  pl.BlockSpec((tail, dc), lambda j, i: (0, 0)),
                  pl.BlockSpec((tail, dh), lambda j, i: (0, 0)),
                  pl.BlockSpec((d, tn), lambda j, i: (0, col_ga // tn + j)),
                  pl.BlockSpec((d, tn), lambda j, i: (0, col_ga // tn + n_n + j)),
                  pl.BlockSpec((dc, tn), lambda j, i: (0, j)),
                  pl.BlockSpec((dh, tn), lambda j, i: (0, j)),
                  side_in],
        out_specs=[pl.BlockSpec((tm, tn), lambda j, i: (i, j)), side_out],
        out_shape=[jax.ShapeDtypeStruct((m, d), BF16), side_shape],
        scratch_shapes=[pltpu.VMEM((2, dc, tn), BF16)],
        compiler_params=_params(("arbitrary", "arbitrary")),
        name="merge",
    )(hx, a, o, a_tail, o_tail, w_in, w_in, w_a, w_b, side_w)


def _outproj_kernel(xp_ref, xs_ref, mix_ref, w_ref, g_ref, x1_ref, h2_ref, *, n_full):
    i = pl.program_id(0)

    def body(rows, is_tail):
        x = xs_ref[...] if is_tail else xp_ref[...]
        x1 = x + _dot(mix_ref[0:rows, :], w_ref[...])
        x1_ref[0:rows, :] = x1
        h2_ref[0:rows, :] = _rms_scale(x1, g_ref[...]).astype(h2_ref.dtype)

    _row_split(i, n_full, xp_ref.shape[0], xs_ref.shape[0], body)


def _outproj(xp, xs, mix, w_out, g, tm):
    mp, d = xp.shape
    tail = xs.shape[0]
    n_full = mp // tm
    return pl.pallas_call(
        functools.partial(_outproj_kernel, n_full=n_full),
        grid=(n_full + 1,),
        in_specs=[pl.BlockSpec((tm, d), lambda i: (jnp.minimum(i, n_full - 1), 0)),
                  pl.BlockSpec((tail, d), lambda i: (0, 0)),
                  pl.BlockSpec((tm, d), lambda i: (i, 0)),
                  pl.BlockSpec((d, d), lambda i: (0, 0)),
                  pl.BlockSpec((1, d), lambda i: (0, 0))],
        out_specs=[pl.BlockSpec((tm, d), lambda i: (i, 0)),
                   pl.BlockSpec((tm, d), lambda i: (i, 0))],
        out_shape=[jax.ShapeDtypeStruct((mp + tail, d), F32),
                   jax.ShapeDtypeStruct((mp + tail, d), BF16)],
        compiler_params=_params(("arbitrary",)),
        name="outproj",
    )(xp, xs, mix, w_out, g.reshape(1, d))


def _mlp_kernel(h2_ref, x1_ref, wup_ref, wdn_ref, g_ref, yp_ref, ys_ref, *, n_full):
    i, j = pl.program_id(0), pl.program_id(1)
    last = pl.num_programs(1) - 1

    def body(rows, is_tail):
        y_ref = ys_ref if is_tail else yp_ref

        @pl.when(j == 0)
        def _():
            y_ref[...] = x1_ref[0:rows, :]

        h = jnp.maximum(_dot(h2_ref[0:rows, :], wup_ref[...]), 0.0)
        y_ref[...] += _dot((h * h).astype(BF16), wdn_ref[...])

        @pl.when(j == last)
        def _():
            y_ref[...] = _rms_scale(y_ref[...], g_ref[...])

    _row_split(i, n_full, yp_ref.shape[0], ys_ref.shape[0], body)


def _mlp(h2, x1, w_up, w_down, g, tm, tf, tail):
    m, d = h2.shape
    dff = w_up.shape[1]
    n_full = (m - tail) // tm
    return pl.pallas_call(
        functools.partial(_mlp_kernel, n_full=n_full),
        grid=(n_full + 1, dff // tf),
        in_specs=[pl.BlockSpec((tm, d), lambda i, j: (i, 0)),
                  pl.BlockSpec((tm, d), lambda i, j: (i, 0)),
                  pl.BlockSpec((d, tf), lambda i, j: (0, j)),
                  pl.BlockSpec((tf, d), lambda i, j: (j, 0)),
                  pl.BlockSpec((1, d), lambda i, j: (0, 0))],
        out_specs=[pl.BlockSpec((tm, d), lambda i, j: (jnp.minimum(i, n_full - 1), 0)),
                   pl.BlockSpec((tail, d), lambda i, j: (0, 0))],
        out_shape=[jax.ShapeDtypeStruct((m - tail, d), F32),
                   jax.ShapeDtypeStruct((tail, d), F32)],
        compiler_params=_params(("arbitrary", "arbitrary")),
        name="mlp",
    )(h2, x1, w_up, w_down, g.reshape(1, d))


def kernel(x_prompt, x_sample, state_conv, state_hgrn, norm_mix, w_in, conv_w, lb_logits, onorm_g,
           w_branch_a, w_branch_b, w_out, norm_ffn, w_up, w_down, norm_final):
    batch, seq, d = x_prompt.shape
    n_dec = x_sample.shape[0]
    depth, _, d_conv = conv_w.shape
    heads, dk = state_hgrn.shape[2], state_hgrn.shape[3]
    d_hgrn = heads * dk
    mp = batch * seq
    assert depth == 1 and x_sample.shape[1] == 1 and dk == HEAD_DIM and state_hgrn.shape[4] == dk
    assert seq % ROW_BLOCK == 0 and mp % MLP_ROW_BLOCK == 0 and ROW_BLOCK % n_dec == 0
    col_hgrn = 3 * d_conv
    col_ga = col_hgrn + 4 * d_hgrn
    w_in0 = w_in.reshape(w_in.shape[1:])
    w_a0 = w_branch_a.reshape(w_branch_a.shape[1:])
    w_b0 = w_branch_b.reshape(w_branch_b.shape[1:])
    w_out0 = w_out.reshape(w_out.shape[1:])
    w_up0 = w_up.reshape(w_up.shape[1:])
    w_down0 = w_down.reshape(w_down.shape[1:])

    xp = x_prompt.reshape(mp, d)
    xs = x_sample.reshape(n_dec, d)
    hx = _rmsnorm(xp, xs, norm_mix[0], ROW_BLOCK)

    a_p, conv_p, w_in_b = _conv_prompt(hx, w_in0, conv_w[0], batch, seq, d_conv, ROW_BLOCK, CONV_COL_BLOCK)
    st2 = state_conv[0].reshape(n_dec, 2 * d_conv)
    a_s, u_s = _conv_sample(hx, mp // n_dec, n_dec, w_in_b, conv_w[0], st2, d_conv, CONV_COL_BLOCK)

    p_s = _proj(hx, mp // n_dec, n_dec, w_in_b, col_hgrn, 4 * d_hgrn, PROJ_COL_BLOCK)
    o_p, hgrn_p, w_up_b, w_down_b, o_s, hgrn_s = _hgrn_prompt(
        hx, w_in_b, col_hgrn, lb_logits, onorm_g[0], batch, seq, heads, HGRN_ROWS, HGRN_HEADS, w_up0, w_down0,
        p_s.reshape(n_dec, 4 * heads, dk), state_hgrn[0])

    mix, w_out_b = _merge(hx, a_p, o_p, a_s, o_s.reshape(n_dec, d_hgrn), w_in_b, w_a0, w_b0,
                          col_ga, ROW_BLOCK, MERGE_COL_BLOCK, w_out0, SIDE_SLABS // 2)
    x1, h2 = _outproj(xp, xs, mix, w_out_b, norm_ffn[0], ROW_BLOCK)
    y_p, y_s = _mlp(h2, x1, w_up_b, w_down_b, norm_final, MLP_ROW_BLOCK, MLP_FF_BLOCK, n_dec)

    conv_s = jnp.stack([state_conv[0, :, 1, :], u_s], axis=1)
    return (y_p.reshape(batch, seq, d), y_s.reshape(n_dec, 1, d),
            conv_p[None], hgrn_p[None], conv_s[None], hgrn_s[None])
```

```python
import functools

import jax
import jax.numpy as jnp
from jax import lax
from jax.experimental import pallas as pl
from jax.experimental.pallas import tpu as pltpu

EPS = 1e-6
CHUNK = 64
HEAD_DIM = 128
V7X_VMEM_BYTES = 64 * 1024 * 1024
VMEM_LIMIT = V7X_VMEM_BYTES * 7 // 8

ROW_BLOCK = 512
MLP_ROW_BLOCK = 1024
MLP_FF_BLOCK = 512
CONV_COL_BLOCK = 512
CONV_SUB_BLOCK = 256
PROJ_COL_BLOCK = 1024
MERGE_COL_BLOCK = 1024
SIDE_SLABS = 64
HGRN_ROWS = 512
HGRN_SUB_ROWS = 256
HGRN_HEADS = 4

BF16 = jnp.bfloat16
F32 = jnp.float32


def _params(semantics):
    return pltpu.CompilerParams(dimension_semantics=semantics, vmem_limit_bytes=VMEM_LIMIT)


def _dot(a, b):
    return jnp.dot(a, b, preferred_element_type=F32)


def _dot_nt(a, b):
    return lax.dot_general(a, b, (((1,), (1,)), ((), ())), preferred_element_type=F32)


def _dot_tn(a, b):
    return lax.dot_general(a, b, (((0,), (0,)), ((), ())), preferred_element_type=F32)


def _sigmoid(x):
    return jax.nn.sigmoid(x)


def _silu(x):
    return x * jax.nn.sigmoid(x)


def _rms_scale(x, g):
    ms = jnp.mean(x * x, axis=-1, keepdims=True)
    return x * lax.rsqrt(ms + EPS) * g


def _row_split(i, n_full, full_rows, tail_rows, body):
    @pl.when(i < n_full)
    def _():
        body(full_rows, False)

    @pl.when(i == n_full)
    def _():
        body(tail_rows, True)


def _slab_cast(step, n_slabs, src_ref, dst_ref):
    @pl.when(step < n_slabs)
    def _():
        dst_ref[...] = src_ref[...].astype(dst_ref.dtype)


def _slab_specs(w, n_slabs, step_of):
    r, c = w.shape
    assert r % n_slabs == 0

    def index(*g):
        return (jnp.minimum(step_of(*g), n_slabs - 1), 0)

    spec = pl.BlockSpec((r // n_slabs, c), index)
    return spec, spec, jax.ShapeDtypeStruct((r, c), BF16)


def _rmsnorm_kernel(xp_ref, xs_ref, g_ref, o_ref, *, n_full):
    tm, tail = xp_ref.shape[0], xs_ref.shape[0]

    def body(rows, is_tail):
        x = xs_ref[...] if is_tail else xp_ref[...]
        o_ref[0:rows, :] = _rms_scale(x, g_ref[...]).astype(o_ref.dtype)

    _row_split(pl.program_id(0), n_full, tm, tail, body)


def _rmsnorm(xp, xs, g, tm):
    mp, d = xp.shape
    tail = xs.shape[0]
    n_full = mp // tm
    return pl.pallas_call(
        functools.partial(_rmsnorm_kernel, n_full=n_full),
        grid=(n_full + 1,),
        in_specs=[pl.BlockSpec((tm, d), lambda i: (jnp.minimum(i, n_full - 1), 0)),
                  pl.BlockSpec((tail, d), lambda i: (0, 0)),
                  pl.BlockSpec((1, d), lambda i: (0, 0))],
        out_specs=pl.BlockSpec((tm, d), lambda i: (i, 0)),
        out_shape=jax.ShapeDtypeStruct((mp + tail, d), BF16),
        compiler_params=_params(("arbitrary",)),
        name="rmsnorm",
    )(xp, xs, g.reshape(1, d))


def _conv_prompt_kernel(hx_ref, whc_ref, wbg_ref, wcg_ref, cw_ref, side_ref, a_ref, nc_ref, side_out_ref,
                        wb_ref, carry_ref, *, n_slabs):
    c, b, t = pl.program_id(0), pl.program_id(1), pl.program_id(2)
    tm = hx_ref.shape[0]
    _slab_cast((c * pl.num_programs(1) + b) * pl.num_programs(2) + t, n_slabs, side_ref, side_out_ref)

    @pl.when((b == 0) & (t == 0))
    def _():
        wb_ref[0] = whc_ref[...].astype(BF16)
        wb_ref[1] = wbg_ref[...].astype(BF16)
        wb_ref[2] = wcg_ref[...].astype(BF16)

    @pl.when(t == 0)
    def _():
        carry_ref[...] = jnp.zeros_like(carry_ref)

    hx = hx_ref[...]
    cw = cw_ref[...]
    row = lax.broadcasted_iota(jnp.int32, (tm, CONV_SUB_BLOCK), 0)
    for sub in range(a_ref.shape[1] // CONV_SUB_BLOCK):
        cs = slice(sub * CONV_SUB_BLOCK, (sub + 1) * CONV_SUB_BLOCK)
        hc = _dot(hx, wb_ref[0, :, cs])
        bg = _dot(hx, wb_ref[1, :, cs])
        cg = _dot(hx, wb_ref[2, :, cs])
        u = cg * hc
        c0 = carry_ref[0:1, cs]
        c1 = carry_ref[1:2, cs]
        u1 = jnp.where(row == 0, c1, pltpu.roll(u, 1, 0))
        u2 = jnp.where(row == 0, c0, jnp.where(row == 1, c1, pltpu.roll(u, 2, 0)))
        conv = cw[0:1, cs] * u2 + cw[1:2, cs] * u1 + cw[2:3, cs] * u
        a_ref[:, cs] = (bg * conv).astype(a_ref.dtype)
        carry_ref[0:2, cs] = u[tm - 2:tm, :]

    @pl.when(t == pl.num_programs(2) - 1)
    def _():
        nc_ref[0] = carry_ref[0:2, :]


def _conv_prompt(hx, w_in, conv_w, batch, seq, d_conv, tm, tc):
    d = hx.shape[1]
    n_c = d_conv // tc
    n_t = seq // tm
    n_slabs = n_c * batch * n_t
    side_in, side_out, side_shape = _slab_specs(w_in, n_slabs, lambda c, b, t: (c * batch + b) * n_t + t)
    return pl.pallas_call(
        functools.partial(_conv_prompt_kernel, n_slabs=n_slabs),
        grid=(n_c, batch, n_t),
        in_specs=[pl.BlockSpec((tm, d), lambda c, b, t: (b * n_t + t, 0)),
                  pl.BlockSpec((d, tc), lambda c, b, t: (0, c)),
                  pl.BlockSpec((d, tc), lambda c, b, t: (0, n_c + c)),
                  pl.BlockSpec((d, tc), lambda c, b, t: (0, 2 * n_c + c)),
                  pl.BlockSpec((3, tc), lambda c, b, t: (0, c)),
                  side_in],
        out_specs=[pl.BlockSpec((tm, tc), lambda c, b, t: (b * n_t + t, c)),
                   pl.BlockSpec((1, 2, tc), lambda c, b, t: (b, 0, c)),
                   side_out],
        out_shape=[jax.ShapeDtypeStruct((batch * seq, d_conv), BF16),
                   jax.ShapeDtypeStruct((batch, 2, d_conv), F32),
                   side_shape],
        scratch_shapes=[pltpu.VMEM((3, d, tc), BF16), pltpu.VMEM((8, tc), F32)],
        compiler_params=_params(("arbitrary", "arbitrary", "arbitrary")),
        name="conv_prompt",
    )(hx, w_in, w_in, w_in, conv_w, w_in)


def _conv_sample_kernel(hx_ref, whc_ref, wbg_ref, wcg_ref, cw_ref, s0_ref, s1_ref, a_ref, u_ref):
    hx = hx_ref[...]
    hc = _dot(hx, whc_ref[...])
    bg = _dot(hx, wbg_ref[...])
    cg = _dot(hx, wcg_ref[...])
    u = cg * hc
    cw = cw_ref[...]
    conv = cw[0:1, :] * s0_ref[...] + cw[1:2, :] * s1_ref[...] + cw[2:3, :] * u
    a_ref[...] = (bg * conv).astype(a_ref.dtype)
    u_ref[...] = u


def _conv_sample(hx, row_block, n, w_in, conv_w, state2d, d_conv, tc):
    d = hx.shape[1]
    n_c = d_conv // tc
    return pl.pallas_call(
        _conv_sample_kernel,
        grid=(n_c,),
        in_specs=[pl.BlockSpec((n, d), lambda c: (row_block, 0)),
                  pl.BlockSpec((d, tc), lambda c: (0, c)),
                  pl.BlockSpec((d, tc), lambda c: (0, n_c + c)),
                  pl.BlockSpec((d, tc), lambda c: (0, 2 * n_c + c)),
                  pl.BlockSpec((3, tc), lambda c: (0, c)),
                  pl.BlockSpec((n, tc), lambda c: (0, c)),
                  pl.BlockSpec((n, tc), lambda c: (0, n_c + c))],
        out_specs=[pl.BlockSpec((n, tc), lambda c: (0, c)),
                   pl.BlockSpec((n, tc), lambda c: (0, c))],
        out_shape=[jax.ShapeDtypeStruct((n, d_conv), BF16),
                   jax.ShapeDtypeStruct((n, d_conv), F32)],
        compiler_params=_params(("arbitrary",)),
        name="conv_sample",
    )(hx, w_in, w_in, w_in, conv_w, state2d, state2d)


def _proj_kernel(x_ref, w_ref, o_ref):
    o_ref[...] = _dot(x_ref[...], w_ref[...])


def _proj(x, row_block, rows, w, col0, n, tn):
    d = x.shape[1]
    return pl.pallas_call(
        _proj_kernel,
        grid=(n // tn,),
        in_specs=[pl.BlockSpec((rows, d), lambda j: (row_block, 0)),
                  pl.BlockSpec((d, tn), lambda j: (0, col0 // tn + j))],
        out_specs=pl.BlockSpec((rows, tn), lambda j: (0, j)),
        out_shape=jax.ShapeDtypeStruct((rows, n), F32),
        compiler_params=_params(("arbitrary",)),
        name="hgrn_proj_decode",
    )(x, w)


def _lower_bound(lbl):
    e = jnp.exp(lbl - jnp.max(lbl, axis=0))
    return e[0] / jnp.sum(e, axis=0)


def _head_norm_gate(o, g, og):
    return _rms_scale(o, g) * _silu(og)


def _cumsum_rows(x, tril_bf16):
    hi = x.astype(BF16)
    r1 = x - hi.astype(F32)
    mid = r1.astype(BF16)
    lo = (r1 - mid.astype(F32)).astype(BF16)
    n = x.shape[1]
    parts = _dot(tril_bf16, jnp.concatenate([hi, mid, lo], axis=1))
    return parts[:, 0:n] + parts[:, n:2 * n] + parts[:, 2 * n:3 * n]


def _hgrn_decode_step(p_ref, lb, g, s0_ref, o_ref, sn_ref):
    bb, heads, dk = s0_ref.shape[0], s0_ref.shape[1], s0_ref.shape[2]
    head_row = lax.broadcasted_iota(jnp.int32, (heads, dk), 0)

    def column(row):
        return jnp.broadcast_to(row, (dk, dk)).T

    for i in range(bb):
        p = p_ref[i]
        qs = _silu(p[0:heads])
        f = lb + (1.0 - lb) * _sigmoid(p[heads:2 * heads])
        kk = 1.0 - f
        v = p[2 * heads:3 * heads]
        og = p[3 * heads:4 * heads]
        q_in = (qs * f).astype(BF16)
        o_inter = jnp.zeros((heads, dk), F32)
        for h in range(heads):
            s = s0_ref[i, h]
            sn_ref[i, h] = column(f[h:h + 1, :]) * s + column(kk[h:h + 1, :]) * v[h:h + 1, :]
            o_inter = o_inter + jnp.where(head_row == h, _dot(q_in, s.astype(BF16)), 0.0)
        o = jnp.sum(qs * kk, axis=-1, keepdims=True) * v + o_inter
        o_ref[i] = _head_norm_gate(o, g, og)


def _hgrn_prompt_kernel(hx_ref, wq_ref, wf_ref, wi_ref, wo_ref, lbl_ref, g_ref, side_a_ref, side_b_ref,
                        dp_ref, dlbl_ref, ds0_ref,
                        o_ref, s_ref, side_a_out_ref, side_b_out_ref, do_ref, dsn_ref,
                        pa_ref, pb_ref, st_ref, *, n_slabs, n_t):
    s = pl.program_id(0)
    _slab_cast(s, n_slabs, side_a_ref, side_a_out_ref)
    _slab_cast(s, n_slabs, side_b_ref, side_b_out_ref)
    tb = hx_ref.shape[0]
    heads_per_step = st_ref.shape[0]
    dk = HEAD_DIM
    sub = HGRN_SUB_ROWS
    n_sub = tb // sub
    n_chunks = sub // CHUNK
    t = lax.rem(jnp.maximum(s - 1, 0), n_t)

    @pl.when(s == 0)
    def _():
        pb_ref[...] = jnp.zeros_like(pb_ref)

    @pl.when(t == 0)
    def _():
        st_ref[...] = jnp.zeros_like(st_ref)

    lb_all = _lower_bound(lbl_ref[...])
    g = g_ref[...]
    row = lax.broadcasted_iota(jnp.int32, (sub, sub), 0)
    col = lax.broadcasted_iota(jnp.int32, (sub, sub), 1)
    shift = CHUNK.bit_length() - 1
    causal = (row >= col) & (jnp.right_shift(row, shift) == jnp.right_shift(col, shift))
    tril = causal.astype(BF16)

    def sub_block(p_next_ref, p_ref, sb):
        heads = range(heads_per_step)
        cols = [slice(hh * dk, (hh + 1) * dk) for hh in heads]
        rows = slice(sb * sub, (sb + 1) * sub)
        seqs = ds0_ref.shape[0] // n_sub
        own = pl.ds(sb * seqs, seqs)

        def project(k, w_ref):
            p_next_ref[k, rows] = _dot(hx_ref[rows, :], w_ref[...])

        qs, kk, lf = [], [], []
        for hh in heads:
            lb = lb_all[:, cols[hh]]
            f = lb + (1.0 - lb) * _sigmoid(p_ref[1, rows, cols[hh]])
            qs.append(_silu(p_ref[0, rows, cols[hh]]))
            kk.append(1.0 - f)
            lf.append(jnp.log(f))
        project(0, wq_ref)
        b = [_cumsum_rows(lf[hh], tril) for hh in heads]
        q_in, k_in, k_end, decay = [], [], [], []
        for hh in heads:
            b3 = b[hh].reshape(n_chunks, CHUNK, dk)
            b_last = b3[:, CHUNK - 1:CHUNK, :]
            b_end = jnp.broadcast_to(b_last, b3.shape).reshape(sub, dk)
            q_in.append((qs[hh] * jnp.exp(b[hh])).astype(BF16))
            k_in.append((kk[hh] * jnp.exp(-b[hh])).astype(BF16))
            k_end.append((kk[hh] * jnp.exp(b_end - b[hh])).astype(BF16))
            decay.append(jnp.exp(b_last))
        project(1, wf_ref)
        v = [p_ref[2, rows, cols[hh]].astype(BF16) for hh in heads]
        raw = [_dot_nt(q_in[hh], k_in[hh]) for hh in heads]
        delta = [[_dot_tn(v[hh][c * CHUNK:(c + 1) * CHUNK], k_end[hh][c * CHUNK:(c + 1) * CHUNK])
                  for c in range(n_chunks)] for hh in heads]
        scores = [jnp.where(causal, raw[hh], 0.0).astype(BF16) for hh in heads]
        starts = []
        for hh in heads:
            st = st_ref[hh]
            per_chunk = []
            for c in range(n_chunks):
                per_chunk.append(st.astype(BF16))
                st = decay[hh][c] * st + delta[hh][c]
            st_ref[hh] = st
            starts.append(per_chunk)
        project(2, wi_ref)
        _hgrn_decode_step(dp_ref.at[own], _lower_bound(dlbl_ref[...]), g, ds0_ref.at[own], do_ref.at[own],
                          dsn_ref.at[own])
        o = []
        for hh in heads:
            inter = [_dot_nt(q_in[hh][c * CHUNK:(c + 1) * CHUNK], starts[hh][c]) for c in range(n_chunks)]
            o.append(_dot(scores[hh], v[hh]) + jnp.concatenate(inter, axis=0))
        for hh in heads:
            o_ref[rows, cols[hh]] = _head_norm_gate(o[hh], g, p_ref[3, rows, cols[hh]]).astype(o_ref.dtype)
        project(3, wo_ref)

    def body(p_next_ref, p_ref):
        for sb in range(n_sub):
            sub_block(p_next_ref, p_ref, sb)

    parity = lax.rem(s, 2)

    @pl.when(parity == 0)
    def _():
        body(pa_ref, pb_ref)

    @pl.when(parity == 1)
    def _():
        body(pb_ref, pa_ref)

    @pl.when(t == n_t - 1)
    def _():
        for hh in range(heads_per_step):
            s_ref[0, hh] = st_ref[hh].T


def _hgrn_prompt(hx, w_in, col0, lb_logits, onorm_g, batch, seq, heads, tb, hp, side_a, side_b, dec_p, dec_state):
    d = hx.shape[1]
    n_t = seq // tb
    n_h = heads // hp
    dk = HEAD_DIM
    wc = hp * dk
    assert CHUNK & (CHUNK - 1) == 0 and col0 % wc == 0
    rows_blocks = batch * n_t
    n_steps = n_h * rows_blocks
    proj = lambda s: jnp.minimum(s, n_steps - 1)
    rec = lambda s: jnp.maximum(s - 1, 0)
    a_in, a_out, a_shape = _slab_specs(side_a, n_steps, lambda s: s)
    b_in, b_out, b_shape = _slab_specs(side_b, n_steps, lambda s: s)

    n_dec, dec_rows = dec_p.shape[0], dec_p.shape[1]
    assert n_dec % n_steps == 0
    db = n_dec // n_steps
    dec = lambda s: jnp.minimum(s, n_steps - 1)

    def wcol(k):
        return pl.BlockSpec((d, wc), lambda s: (0, col0 // wc + k * n_h + proj(s) // rows_blocks))

    return pl.pallas_call(
        functools.partial(_hgrn_prompt_kernel, n_slabs=n_steps, n_t=n_t),
        grid=(n_steps + 1,),
        in_specs=[pl.BlockSpec((tb, d), lambda s: (proj(s) % rows_blocks, 0)),
                  wcol(0), wcol(1), wcol(2), wcol(3),
                  pl.BlockSpec((2, 1, wc), lambda s: (0, 0, rec(s) // rows_blocks)),
                  pl.BlockSpec((1, dk), lambda s: (0, 0)),
                  a_in, b_in,
                  pl.BlockSpec((db, dec_rows, dk), lambda s: (dec(s), 0, 0)),
                  pl.BlockSpec((2, heads, dk), lambda s: (0, 0, 0)),
                  pl.BlockSpec((db, heads, dk, dk), lambda s: (dec(s), 0, 0, 0))],
        out_specs=[pl.BlockSpec((tb, wc), lambda s: (rec(s) % rows_blocks, rec(s) // rows_blocks)),
                   pl.BlockSpec((1, hp, dk, dk),
                                lambda s: ((rec(s) % rows_blocks) // n_t, rec(s) // rows_blocks, 0, 0)),
                   a_out, b_out,
                   pl.BlockSpec((db, heads, dk), lambda s: (dec(s), 0, 0)),
                   pl.BlockSpec((db, heads, dk, dk), lambda s: (dec(s), 0, 0, 0))],
        out_shape=[jax.ShapeDtypeStruct((batch * seq, heads * dk), BF16),
                   jax.ShapeDtypeStruct((batch, heads, dk, dk), F32),
                   a_shape, b_shape,
                   jax.ShapeDtypeStruct((n_dec, heads, dk), F32),
                   jax.ShapeDtypeStruct((n_dec, heads, dk, dk), F32)],
        scratch_shapes=[pltpu.VMEM((4, tb, wc), F32), pltpu.VMEM((4, tb, wc), F32),
                        pltpu.VMEM((hp, dk, dk), F32)],
        compiler_params=_params(("arbitrary",)),
        name="hgrn_prompt",
    )(hx, w_in, w_in, w_in, w_in, lb_logits.reshape(2, 1, heads * dk), onorm_g.reshape(1, dk), side_a, side_b,
      dec_p, lb_logits.reshape(2, heads, dk), dec_state)


def _merge_kernel(hx_ref, a_ref, o_ref, at_ref, ot_ref, wga_ref, wgb_ref, wa_ref, wb_ref, side_ref,
                  mix_ref, side_out_ref, wabb_ref, *, n_full, n_slabs):
    j, i = pl.program_id(0), pl.program_id(1)
    _slab_cast(j * pl.num_programs(1) + i, n_slabs, side_ref, side_out_ref)

    @pl.when(i == 0)
    def _():
        wabb_ref[0] = wa_ref[...].astype(BF16)
        wabb_ref[1] = wb_ref[...].astype(BF16)

    def body(rows, is_tail):
        hx = hx_ref[0:rows, :]
        a = at_ref[...] if is_tail else a_ref[...]
        o = ot_ref[...].astype(BF16) if is_tail else o_ref[...]
        ga = _sigmoid(_dot(hx, wga_ref[...]))
        gb = _sigmoid(_dot(hx, wgb_ref[...]))
        mix = ga * _dot(a, wabb_ref[0]) + gb * _dot(o, wabb_ref[1])
        mix_ref[0:rows, :] = mix.astype(mix_ref.dtype)

    _row_split(i, n_full, hx_ref.shape[0], at_ref.shape[0], body)


def _merge(hx, a, o, a_tail, o_tail, w_in, w_a, w_b, col_ga, tm, tn, side_w, n_slabs):
    m, d = hx.shape
    dc, dh = a.shape[1], o.shape[1]
    tail = a_tail.shape[0]
    n_full = a.shape[0] // tm
    n_n = d // tn
    clamp = lambda j, i: (jnp.minimum(i, n_full - 1), 0)
    assert n_slabs <= n_n * (n_full + 1)
    side_in, side_out, side_shape = _slab_specs(side_w, n_slabs, lambda j, i: j * (n_full + 1) + i)
    return pl.pallas_call(
        functools.partial(_merge_kernel, n_full=n_full, n_slabs=n_slabs),
        grid=(n_n, n_full + 1),
        in_specs=[pl.BlockSpec((tm, d), lambda j, i: (i, 0)),
                  pl.BlockSpec((tm, dc), clamp),
                  pl.BlockSpec((tm, dh), clamp),
                  pl.BlockSpec((tail, dc), lambda j, i: (0, 0)),
                  pl.BlockSpec((tail, dh), lambda j, i: (0, 0)),
                  pl.BlockSpec((d, tn), lambda j, i: (0, col_ga // tn + j)),
                  pl.BlockSpec((d, tn), lambda j, i: (0, col_ga // tn + n_n + j)),
                  pl.BlockSpec((dc, tn), lambda j, i: (0, j)),
                  pl.BlockSpec((dh, tn), lambda j, i: (0, j)),
                  side_in],
        out_specs=[pl.BlockSpec((tm, tn), lambda j, i: (i, j)), side_out],
        out_shape=[jax.ShapeDtypeStruct((m, d), BF16), side_shape],
        scratch_shapes=[pltpu.VMEM((2, dc, tn), BF16)],
        compiler_params=_params(("arbitrary", "arbitrary")),
        name="merge",
    )(hx, a, o, a_tail, o_tail, w_in, w_in, w_a, w_b, side_w)


def _outproj_kernel(xp_ref, xs_ref, mix_ref, w_ref, g_ref, x1_ref, h2_ref, *, n_full):
    i = pl.program_id(0)

    def body(rows, is_tail):
        x = xs_ref[...] if is_tail else xp_ref[...]
        x1 = x + _dot(mix_ref[0:rows, :], w_ref[...])
        x1_ref[0:rows, :] = x1
        h2_ref[0:rows, :] = _rms_scale(x1, g_ref[...]).astype(h2_ref.dtype)

    _row_split(i, n_full, xp_ref.shape[0], xs_ref.shape[0], body)


def _outproj(xp, xs, mix, w_out, g, tm):
    mp, d = xp.shape
    tail = xs.shape[0]
    n_full = mp // tm
    return pl.pallas_call(
        functools.partial(_outproj_kernel, n_full=n_full),
        grid=(n_full + 1,),
        in_specs=[pl.BlockSpec((tm, d), lambda i: (jnp.minimum(i, n_full - 1), 0)),
                  pl.BlockSpec((tail, d), lambda i: (0, 0)),
                  pl.BlockSpec((tm, d), lambda i: (i, 0)),
                  pl.BlockSpec((d, d), lambda i: (0, 0)),
                  pl.BlockSpec((1, d), lambda i: (0, 0))],
        out_specs=[pl.BlockSpec((tm, d), lambda i: (i, 0)),
                   pl.BlockSpec((tm, d), lambda i: (i, 0))],
        out_shape=[jax.ShapeDtypeStruct((mp + tail, d), F32),
                   jax.ShapeDtypeStruct((mp + tail, d), BF16)],
        compiler_params=_params(("arbitrary",)),
        name="outproj",
    )(xp, xs, mix, w_out, g.reshape(1, d))


def _mlp_kernel(h2_ref, x1_ref, wup_ref, wdn_ref, g_ref, yp_ref, ys_ref, *, n_full):
    i, j = pl.program_id(0), pl.program_id(1)
    last = pl.num_programs(1) - 1

    def body(rows, is_tail):
        y_ref = ys_ref if is_tail else yp_ref

        @pl.when(j == 0)
        def _():
            y_ref[...] = x1_ref[0:rows, :]

        h = jnp.maximum(_dot(h2_ref[0:rows, :], wup_ref[...]), 0.0)
        y_ref[...] += _dot((h * h).astype(BF16), wdn_ref[...])

        @pl.when(j == last)
        def _():
            y_ref[...] = _rms_scale(y_ref[...], g_ref[...])

    _row_split(i, n_full, yp_ref.shape[0], ys_ref.shape[0], body)


def _mlp(h2, x1, w_up, w_down, g, tm, tf, tail):
    m, d = h2.shape
    dff = w_up.shape[1]
    n_full = (m - tail) // tm
    return pl.pallas_call(
        functools.partial(_mlp_kernel, n_full=n_full),
        grid=(n_full + 1, dff // tf),
        in_specs=[pl.BlockSpec((tm, d), lambda i, j: (i, 0)),
                  pl.BlockSpec((tm, d), lambda i, j: (i, 0)),
                  pl.BlockSpec((d, tf), lambda i, j: (0, j)),
                  pl.BlockSpec((tf, d), lambda i, j: (j, 0)),
                  pl.BlockSpec((1, d), lambda i, j: (0, 0))],
        out_specs=[pl.BlockSpec((tm, d), lambda i, j: (jnp.minimum(i, n_full - 1), 0)),
                   pl.BlockSpec((tail, d), lambda i, j: (0, 0))],
        out_shape=[jax.ShapeDtypeStruct((m - tail, d), F32),
                   jax.ShapeDtypeStruct((tail, d), F32)],
        compiler_params=_params(("arbitrary", "arbitrary")),
        name="mlp",
    )(h2, x1, w_up, w_down, g.reshape(1, d))


def kernel(x_prompt, x_sample, state_conv, state_hgrn, norm_mix, w_in, conv_w, lb_logits, onorm_g,
           w_branch_a, w_branch_b, w_out, norm_ffn, w_up, w_down, norm_final):
    batch, seq, d = x_prompt.shape
    n_dec = x_sample.shape[0]
    depth, _, d_conv = conv_w.shape
    heads, dk = state_hgrn.shape[2], state_hgrn.shape[3]
    d_hgrn = heads * dk
    mp = batch * seq
    assert depth == 1 and x_sample.shape[1] == 1 and dk == HEAD_DIM and state_hgrn.shape[4] == dk
    assert seq % ROW_BLOCK == 0 and mp % MLP_ROW_BLOCK == 0 and ROW_BLOCK % n_dec == 0
    col_hgrn = 3 * d_conv
    col_ga = col_hgrn + 4 * d_hgrn
    w_in0 = w_in.reshape(w_in.shape[1:])
    w_a0 = w_branch_a.reshape(w_branch_a.shape[1:])
    w_b0 = w_branch_b.reshape(w_branch_b.shape[1:])
    w_out0 = w_out.reshape(w_out.shape[1:])
    w_up0 = w_up.reshape(w_up.shape[1:])
    w_down0 = w_down.reshape(w_down.shape[1:])

    xp = x_prompt.reshape(mp, d)
    xs = x_sample.reshape(n_dec, d)
    hx = _rmsnorm(xp, xs, norm_mix[0], ROW_BLOCK)

    a_p, conv_p, w_in_b = _conv_prompt(hx, w_in0, conv_w[0], batch, seq, d_conv, ROW_BLOCK, CONV_COL_BLOCK)
    st2 = state_conv[0].reshape(n_dec, 2 * d_conv)
    a_s, u_s = _conv_sample(hx, mp // n_dec, n_dec, w_in_b, conv_w[0], st2, d_conv, CONV_COL_BLOCK)

    p_s = _proj(hx, mp // n_dec, n_dec, w_in_b, col_hgrn, 4 * d_hgrn, PROJ_COL_BLOCK)
    o_p, hgrn_p, w_up_b, w_down_b, o_s, hgrn_s = _hgrn_prompt(
        hx, w_in_b, col_hgrn, lb_logits, onorm_g[0], batch, seq, heads, HGRN_ROWS, HGRN_HEADS, w_up0, w_down0,
        p_s.reshape(n_dec, 4 * heads, dk), state_hgrn[0])

    mix, w_out_b = _merge(hx, a_p, o_p, a_s, o_s.reshape(n_dec, d_hgrn), w_in_b, w_a0, w_b0,
                          col_ga, ROW_BLOCK, MERGE_COL_BLOCK, w_out0, SIDE_SLABS // 2)
    x1, h2 = _outproj(xp, xs, mix, w_out_b, norm_ffn[0], ROW_BLOCK)
    y_p, y_s = _mlp(h2, x1, w_up_b, w_down_b, norm_final, MLP_ROW_BLOCK, MLP_FF_BLOCK, n_dec)

    conv_s = jnp.stack([state_conv[0, :, 1, :], u_s], axis=1)
    return (y_p.reshape(batch, seq, d), y_s.reshape(n_dec, 1, d),
            conv_p[None], hgrn_p[None], conv_s[None], hgrn_s[None])
```

```python
import functools

import jax
import jax.numpy as jnp
from jax import lax
from jax.experimental import pallas as pl
from jax.experimental.pallas import tpu as pltpu

EPS = 1e-6
CHUNK = 64
HEAD_DIM = 128
V7X_VMEM_BYTES = 64 * 1024 * 1024
VMEM_LIMIT = V7X_VMEM_BYTES * 7 // 8

ROW_BLOCK = 512
MLP_ROW_BLOCK = 1024
MLP_FF_BLOCK = 512
CONV_COL_BLOCK = 512
CONV_SUB_BLOCK = 256
PROJ_COL_BLOCK = 1024
MERGE_COL_BLOCK = 1024
SIDE_SLABS = 64
HGRN_ROWS = 512
HGRN_SUB_ROWS = 256
HGRN_HEADS = 4

BF16 = jnp.bfloat16
F32 = jnp.float32


def _params(semantics):
    return pltpu.CompilerParams(dimension_semantics=semantics, vmem_limit_bytes=VMEM_LIMIT)


def _dot(a, b):
    return jnp.dot(a, b, preferred_element_type=F32)


def _dot_nt(a, b):
    return lax.dot_general(a, b, (((1,), (1,)), ((), ())), preferred_element_type=F32)


def _dot_tn(a, b):
    return lax.dot_general(a, b, (((0,), (0,)), ((), ())), preferred_element_type=F32)


def _sigmoid(x):
    return jax.nn.sigmoid(x)


def _silu(x):
    return x * jax.nn.sigmoid(x)


def _rms_scale(x, g):
    ms = jnp.mean(x * x, axis=-1, keepdims=True)
    return x * lax.rsqrt(ms + EPS) * g


def _row_split(i, n_full, full_rows, tail_rows, body):
    @pl.when(i < n_full)
    def _():
        body(full_rows, False)

    @pl.when(i == n_full)
    def _():
        body(tail_rows, True)


def _slab_cast(step, n_slabs, src_ref, dst_ref):
    @pl.when(step < n_slabs)
    def _():
        dst_ref[...] = src_ref[...].astype(dst_ref.dtype)


def _slab_specs(w, n_slabs, step_of):
    r, c = w.shape
    assert r % n_slabs == 0

    def index(*g):
        return (jnp.minimum(step_of(*g), n_slabs - 1), 0)

    spec = pl.BlockSpec((r // n_slabs, c), index)
    return spec, spec, jax.ShapeDtypeStruct((r, c), BF16)


def _conv_prompt_kernel(x_ref, g_ref, whc_ref, wbg_ref, wcg_ref, cw_ref, side_ref,
                        hx_out_ref, a_ref, nc_ref, side_out_ref,
                        h_ref, wb_ref, carry_ref, *, n_slabs, n_t, n_c):
    s = pl.program_id(0)
    tm, tc = a_ref.shape
    j = jnp.maximum(s - 1, 0)
    t = lax.rem(j // n_c, n_t)
    _slab_cast(s, n_slabs, side_ref, side_out_ref)

    def normalise(slot):
        hx = _rms_scale(x_ref[...], g_ref[...]).astype(BF16)
        h_ref[slot] = hx
        hx_out_ref[...] = hx
        return hx

    @pl.when(s == 0)
    def _():
        normalise(0)

    @pl.when((s >= 1) & (s <= n_c))
    def _():
        for c in range(n_c):
            @pl.when(s == c + 1)
            def _(c=c):
                wb_ref[c, 0] = whc_ref[...].astype(BF16)
                wb_ref[c, 1] = wbg_ref[...].astype(BF16)
                wb_ref[c, 2] = wcg_ref[...].astype(BF16)

    cw = cw_ref[...]
    row = lax.broadcasted_iota(jnp.int32, (tm, CONV_SUB_BLOCK), 0)

    def body(c, slot):
        @pl.when((t == 0) & (c == 0))
        def _():
            carry_ref[...] = jnp.zeros_like(carry_ref)

        last_channel = c == n_c - 1
        hx_next = normalise(1 - slot) if last_channel else None
        hx = h_ref[slot]
        n_sub = tc // CONV_SUB_BLOCK
        for sub in range(n_sub):
            cs = slice(sub * CONV_SUB_BLOCK, (sub + 1) * CONV_SUB_BLOCK)
            lhs = jnp.where(s < 0, hx_next, hx) if (last_channel and sub == n_sub - 1) else hx
            hc = _dot(lhs, wb_ref[c, 0, :, cs])
            bg = _dot(lhs, wb_ref[c, 1, :, cs])
            cg = _dot(lhs, wb_ref[c, 2, :, cs])
            u = cg * hc
            c0 = carry_ref[c, 0:1, cs]
            c1 = carry_ref[c, 1:2, cs]
            u1 = jnp.where(row == 0, c1, pltpu.roll(u, 1, 0))
            u2 = jnp.where(row == 0, c0, jnp.where(row == 1, c1, pltpu.roll(u, 2, 0)))
            conv = cw[0:1, cs] * u2 + cw[1:2, cs] * u1 + cw[2:3, cs] * u
            a_ref[:, cs] = (bg * conv).astype(a_ref.dtype)
            carry_ref[c, 0:2, cs] = u[tm - 2:tm, :]

        @pl.when(t == n_t - 1)
        def _():
            nc_ref[0, :, c * tc:(c + 1) * tc] = carry_ref[c, 0:2, :]

    phase = lax.rem(j, 2 * n_c)
    for m in range(2 * n_c):
        @pl.when((s >= 1) & (phase == m))
        def _(m=m):
            body(m % n_c, m // n_c)


def _conv_prompt(x, g, w_in, conv_w, batch, seq, d_conv, tm, tc):
    mp, d = x.shape
    n_c = d_conv // tc
    n_t = seq // tm
    rows_blocks = batch * n_t
    n_steps = n_c * rows_blocks
    conv = lambda s: jnp.maximum(s - 1, 0)
    norm = lambda s: jnp.minimum(s // n_c, rows_blocks - 1)
    col_rest = 3 * d_conv
    sw = tc * n_c
    n_col = (w_in.shape[1] - col_rest) // sw
    assert n_steps == 4 * n_col and col_rest % sw == 0
    tile = lambda s: jnp.minimum(s, n_steps - 1)
    side_in = pl.BlockSpec((d // 4, sw), lambda s: (tile(s) % 4, col_rest // sw + tile(s) // 4))
    side_out = pl.BlockSpec((d // 4, sw), lambda s: (tile(s) % 4, tile(s) // 4))
    side_shape = jax.ShapeDtypeStruct((d, w_in.shape[1] - col_rest), BF16)

    def wcol(k):
        return pl.BlockSpec((d, tc), lambda s: (0, k * n_c + jnp.minimum(conv(s), n_c - 1)),
                            pipeline_mode=pl.Buffered(1))

    return pl.pallas_call(
        functools.partial(_conv_prompt_kernel, n_slabs=n_steps, n_t=n_t, n_c=n_c),
        grid=(n_steps + 1,),
        in_specs=[pl.BlockSpec((tm, d), lambda s: (norm(s), 0)),
                  pl.BlockSpec((1, d), lambda s: (0, 0)),
                  wcol(0), wcol(1), wcol(2),
                  pl.BlockSpec((3, tc), lambda s: (0, conv(s) % n_c)),
                  side_in],
        out_specs=[pl.BlockSpec((tm, d), lambda s: (norm(s), 0)),
                   pl.BlockSpec((tm, tc), lambda s: (conv(s) // n_c, conv(s) % n_c)),
                   pl.BlockSpec((1, 2, d_conv), lambda s: (conv(s) // n_c // n_t, 0, 0)),
                   side_out],
        out_shape=[jax.ShapeDtypeStruct((mp, d), BF16),
                   jax.ShapeDtypeStruct((mp, d_conv), BF16),
                   jax.ShapeDtypeStruct((batch, 2, d_conv), F32),
                   side_shape],
        scratch_shapes=[pltpu.VMEM((2, tm, d), BF16), pltpu.VMEM((n_c, 3, d, tc), BF16),
                        pltpu.VMEM((n_c, 8, tc), F32)],
        compiler_params=_params(("arbitrary",)),
        name="conv_prompt",
    )(x, g.reshape(1, d), w_in, w_in, w_in, conv_w, w_in)


def _conv_sample_kernel(x_ref, g_ref, whc_ref, wbg_ref, wcg_ref, cw_ref, s0_ref, s1_ref, hx_ref, a_ref, u_ref):
    hx = _rms_scale(x_ref[...], g_ref[...]).astype(BF16)
    hx_ref[...] = hx
    hc = _dot(hx, whc_ref[...].astype(BF16))
    bg = _dot(hx, wbg_ref[...].astype(BF16))
    cg = _dot(hx, wcg_ref[...].astype(BF16))
    u = cg * hc
    cw = cw_ref[...]
    conv = cw[0:1, :] * s0_ref[...] + cw[1:2, :] * s1_ref[...] + cw[2:3, :] * u
    a_ref[...] = (bg * conv).astype(a_ref.dtype)
    u_ref[...] = u


def _conv_sample(x, g, w_in, conv_w, state2d, d_conv, tc):
    n, d = x.shape
    n_c = d_conv // tc
    return pl.pallas_call(
        _conv_sample_kernel,
        grid=(n_c,),
        in_specs=[pl.BlockSpec((n, d), lambda c: (0, 0)),
                  pl.BlockSpec((1, d), lambda c: (0, 0)),
                  pl.BlockSpec((d, tc), lambda c: (0, c)),
                  pl.BlockSpec((d, tc), lambda c: (0, n_c + c)),
                  pl.BlockSpec((d, tc), lambda c: (0, 2 * n_c + c)),
                  pl.BlockSpec((3, tc), lambda c: (0, c)),
                  pl.BlockSpec((n, tc), lambda c: (0, c)),
                  pl.BlockSpec((n, tc), lambda c: (0, n_c + c))],
        out_specs=[pl.BlockSpec((n, d), lambda c: (0, 0)),
                   pl.BlockSpec((n, tc), lambda c: (0, c)),
                   pl.BlockSpec((n, tc), lambda c: (0, c))],
        out_shape=[jax.ShapeDtypeStruct((n, d), BF16),
                   jax.ShapeDtypeStruct((n, d_conv), BF16),
                   jax.ShapeDtypeStruct((n, d_conv), F32)],
        compiler_params=_params(("arbitrary",)),
        name="conv_sample",
    )(x, g.reshape(1, d), w_in, w_in, w_in, conv_w, state2d, state2d)


def _proj_kernel(x_ref, w_ref, o_ref):
    o_ref[...] = _dot(x_ref[...], w_ref[...])


def _proj(x, w, col0, n, tn):
    rows, d = x.shape
    return pl.pallas_call(
        _proj_kernel,
        grid=(n // tn,),
        in_specs=[pl.BlockSpec((rows, d), lambda j: (0, 0)),
                  pl.BlockSpec((d, tn), lambda j: (0, col0 // tn + j))],
        out_specs=pl.BlockSpec((rows, tn), lambda j: (0, j)),
        out_shape=jax.ShapeDtypeStruct((rows, n), F32),
        compiler_params=_params(("arbitrary",)),
        name="hgrn_proj_decode",
    )(x, w)


def _lower_bound(lbl):
    e = jnp.exp(lbl - jnp.max(lbl, axis=0))
    return e[0] / jnp.sum(e, axis=0)


def _head_norm_gate(o, g, og):
    return _rms_scale(o, g) * _silu(og)


def _cumsum_rows(x, tril_bf16):
    hi = x.astype(BF16)
    r1 = x - hi.astype(F32)
    mid = r1.astype(BF16)
    lo = (r1 - mid.astype(F32)).astype(BF16)
    n = x.shape[1]
    parts = _dot(tril_bf16, jnp.concatenate([hi, mid, lo], axis=1))
    return parts[:, 0:n] + parts[:, n:2 * n] + parts[:, 2 * n:3 * n]


def _hgrn_decode_step(p_ref, lb, g, s0_ref, o_ref, sn_ref):
    bb, heads, dk = s0_ref.shape[0], s0_ref.shape[1], s0_ref.shape[2]
    head_row = lax.broadcasted_iota(jnp.int32, (heads, dk), 0)

    def column(row):
        return jnp.broadcast_to(row, (dk, dk)).T

    for i in range(bb):
        p = p_ref[i]
        qs = _silu(p[0:heads])
        f = lb + (1.0 - lb) * _sigmoid(p[heads:2 * heads])
        kk = 1.0 - f
        v = p[2 * heads:3 * heads]
        og = p[3 * heads:4 * heads]
        q_in = (qs * f).astype(BF16)
        o_inter = jnp.zeros((heads, dk), F32)
        for h in range(heads):
            s = s0_ref[i, h]
            sn_ref[i, h] = column(f[h:h + 1, :]) * s + column(kk[h:h + 1, :]) * v[h:h + 1, :]
            o_inter = o_inter + jnp.where(head_row == h, _dot(q_in, s.astype(BF16)), 0.0)
        o = jnp.sum(qs * kk, axis=-1, keepdims=True) * v + o_inter
        o_ref[i] = _head_norm_gate(o, g, og)


def _hgrn_prompt_kernel(hx_ref, wq_ref, wf_ref, wi_ref, wo_ref, lbl_ref, g_ref, side_ref,
                        dp_ref, dlbl_ref, ds0_ref,
                        o_ref, s_ref, side_out_ref, do_ref, dsn_ref,
                        pa_ref, pb_ref, st_ref, *, n_slabs, n_t):
    s = pl.program_id(0)
    _slab_cast(s, n_slabs, side_ref, side_out_ref)
    tb = hx_ref.shape[0]
    heads_per_step = st_ref.shape[0]
    dk = HEAD_DIM
    sub = HGRN_SUB_ROWS
    n_sub = tb // sub
    n_chunks = sub // CHUNK
    t = lax.rem(jnp.maximum(s - 1, 0), n_t)

    @pl.when(s == 0)
    def _():
        pb_ref[...] = jnp.zeros_like(pb_ref)

    @pl.when(t == 0)
    def _():
        st_ref[...] = jnp.zeros_like(st_ref)

    lb_all = _lower_bound(lbl_ref[...])
    g = g_ref[...]
    row = lax.broadcasted_iota(jnp.int32, (sub, sub), 0)
    col = lax.broadcasted_iota(jnp.int32, (sub, sub), 1)
    shift = CHUNK.bit_length() - 1
    causal = (row >= col) & (jnp.right_shift(row, shift) == jnp.right_shift(col, shift))
    tril = causal.astype(BF16)

    def sub_block(p_next_ref, p_ref, sb):
        heads = range(heads_per_step)
        cols = [slice(hh * dk, (hh + 1) * dk) for hh in heads]
        rows = slice(sb * sub, (sb + 1) * sub)
        seqs = ds0_ref.shape[0] // n_sub
        own = pl.ds(sb * seqs, seqs)

        def project(k, w_ref):
            p_next_ref[k, rows] = _dot(hx_ref[rows, :], w_ref[...])

        qs, kk, lf = [], [], []
        for hh in heads:
            lb = lb_all[:, cols[hh]]
            f = lb + (1.0 - lb) * _sigmoid(p_ref[1, rows, cols[hh]])
            qs.append(_silu(p_ref[0, rows, cols[hh]]))
            kk.append(1.0 - f)
            lf.append(jnp.log(f))
        project(0, wq_ref)
        b = [_cumsum_rows(lf[hh], tril) for hh in heads]
        q_in, k_in, k_end, decay = [], [], [], []
        for hh in heads:
            b3 = b[hh].reshape(n_chunks, CHUNK, dk)
            b_last = b3[:, CHUNK - 1:CHUNK, :]
            b_end = jnp.broadcast_to(b_last, b3.shape).reshape(sub, dk)
            q_in.append((qs[hh] * jnp.exp(b[hh])).astype(BF16))
            k_in.append((kk[hh] * jnp.exp(-b[hh])).astype(BF16))
            k_end.append((kk[hh] * jnp.exp(b_end - b[hh])).astype(BF16))
            decay.append(jnp.exp(b_last))
        project(1, wf_ref)
        v = [p_ref[2, rows, cols[hh]].astype(BF16) for hh in heads]
        raw = [_dot_nt(q_in[hh], k_in[hh]) for hh in heads]
        delta = [[_dot_tn(v[hh][c * CHUNK:(c + 1) * CHUNK], k_end[hh][c * CHUNK:(c + 1) * CHUNK])
                  for c in range(n_chunks)] for hh in heads]
        scores = [jnp.where(causal, raw[hh], 0.0).astype(BF16) for hh in heads]
        starts = []
        for hh in heads:
            st = st_ref[hh]
            per_chunk = []
            for c in range(n_chunks):
                per_chunk.append(st.astype(BF16))
                st = decay[hh][c] * st + delta[hh][c]
            st_ref[hh] = st
            starts.append(per_chunk)
        project(2, wi_ref)
        _hgrn_decode_step(dp_ref.at[own], _lower_bound(dlbl_ref[...]), g, ds0_ref.at[own], do_ref.at[own],
                          dsn_ref.at[own])
        o = []
        for hh in heads:
            inter = [_dot_nt(q_in[hh][c * CHUNK:(c + 1) * CHUNK], starts[hh][c]) for c in range(n_chunks)]
            o.append(_dot(scores[hh], v[hh]) + jnp.concatenate(inter, axis=0))
        for hh in heads:
            o_ref[rows, cols[hh]] = _head_norm_gate(o[hh], g, p_ref[3, rows, cols[hh]]).astype(o_ref.dtype)
        project(3, wo_ref)

    def body(p_next_ref, p_ref):
        for sb in range(n_sub):
            sub_block(p_next_ref, p_ref, sb)

    parity = lax.rem(s, 2)

    @pl.when(parity == 0)
    def _():
        body(pa_ref, pb_ref)

    @pl.when(parity == 1)
    def _():
        body(pb_ref, pa_ref)

    @pl.when(t == n_t - 1)
    def _():
        for hh in range(heads_per_step):
            s_ref[0, hh] = st_ref[hh].T


def _hgrn_prompt(hx, w_in, col0, lb_logits, onorm_g, batch, seq, heads, tb, hp, side_w, dec_p, dec_state):
    d = hx.shape[1]
    n_t = seq // tb
    n_h = heads // hp
    dk = HEAD_DIM
    wc = hp * dk
    assert CHUNK & (CHUNK - 1) == 0 and col0 % wc == 0
    rows_blocks = batch * n_t
    n_steps = n_h * rows_blocks
    proj = lambda s: jnp.minimum(s, n_steps - 1)
    rec = lambda s: jnp.maximum(s - 1, 0)
    side_in, side_out, side_shape = _slab_specs(side_w, n_steps, lambda s: s)

    n_dec, dec_rows = dec_p.shape[0], dec_p.shape[1]
    assert n_dec % n_steps == 0
    db = n_dec // n_steps
    dec = lambda s: jnp.minimum(s, n_steps - 1)

    def wcol(k):
        return pl.BlockSpec((d, wc), lambda s: (0, col0 // wc + k * n_h + proj(s) // rows_blocks))

    return pl.pallas_call(
        functools.partial(_hgrn_prompt_kernel, n_slabs=n_steps, n_t=n_t),
        grid=(n_steps + 1,),
        in_specs=[pl.BlockSpec((tb, d), lambda s: (proj(s) % rows_blocks, 0)),
                  wcol(0), wcol(1), wcol(2), wcol(3),
                  pl.BlockSpec((2, 1, wc), lambda s: (0, 0, rec(s) // rows_blocks)),
                  pl.BlockSpec((1, dk), lambda s: (0, 0)),
                  side_in,
                  pl.BlockSpec((db, dec_rows, dk), lambda s: (dec(s), 0, 0)),
                  pl.BlockSpec((2, heads, dk), lambda s: (0, 0, 0)),
                  pl.BlockSpec((db, heads, dk, dk), lambda s: (dec(s), 0, 0, 0))],
        out_specs=[pl.BlockSpec((tb, wc), lambda s: (rec(s) % rows_blocks, rec(s) // rows_blocks)),
                   pl.BlockSpec((1, hp, dk, dk),
                                lambda s: ((rec(s) % rows_blocks) // n_t, rec(s) // rows_blocks, 0, 0)),
                   side_out,
                   pl.BlockSpec((db, heads, dk), lambda s: (dec(s), 0, 0)),
                   pl.BlockSpec((db, heads, dk, dk), lambda s: (dec(s), 0, 0, 0))],
        out_shape=[jax.ShapeDtypeStruct((batch * seq, heads * dk), BF16),
                   jax.ShapeDtypeStruct((batch, heads, dk, dk), F32),
                   side_shape,
                   jax.ShapeDtypeStruct((n_dec, heads, dk), F32),
                   jax.ShapeDtypeStruct((n_dec, heads, dk, dk), F32)],
        scratch_shapes=[pltpu.VMEM((4, tb, wc), F32), pltpu.VMEM((4, tb, wc), F32),
                        pltpu.VMEM((hp, dk, dk), F32)],
        compiler_params=_params(("arbitrary",)),
        name="hgrn_prompt",
    )(hx, w_in, w_in, w_in, w_in, lb_logits.reshape(2, 1, heads * dk), onorm_g.reshape(1, dk), side_w,
      dec_p, lb_logits.reshape(2, heads, dk), dec_state)


def _merge_kernel(hx_ref, a_ref, o_ref, ht_ref, at_ref, ot_ref, wga_ref, wgb_ref, wa_ref, wb_ref, side_a_ref,
                  side_b_ref, mix_ref, side_a_out_ref, side_b_out_ref, wabb_ref, *, n_full, n_slabs):
    j, i = pl.program_id(0), pl.program_id(1)
    _slab_cast(j * pl.num_programs(1) + i, n_slabs, side_a_ref, side_a_out_ref)
    _slab_cast(j * pl.num_programs(1) + i, n_slabs, side_b_ref, side_b_out_ref)

    @pl.when(i == 0)
    def _():
        wabb_ref[0] = wa_ref[...].astype(BF16)
        wabb_ref[1] = wb_ref[...].astype(BF16)

    def body(rows, is_tail):
        hx = ht_ref[...] if is_tail else hx_ref[...]
        a = at_ref[...] if is_tail else a_ref[...]
        o = ot_ref[...].astype(BF16) if is_tail else o_ref[...]
        ga = _sigmoid(_dot(hx, wga_ref[...]))
        gb = _sigmoid(_dot(hx, wgb_ref[...]))
        mix = ga * _dot(a, wabb_ref[0]) + gb * _dot(o, wabb_ref[1])
        mix_ref[0:rows, :] = mix.astype(mix_ref.dtype)

    _row_split(i, n_full, hx_ref.shape[0], ht_ref.shape[0], body)


def _merge(hx, a, o, hx_tail, a_tail, o_tail, w_in, w_a, w_b, col_ga, tm, tn, side_a, side_b, n_slabs):
    d = hx.shape[1]
    m = a.shape[0] + hx_tail.shape[0]
    dc, dh = a.shape[1], o.shape[1]
    tail = a_tail.shape[0]
    n_full = a.shape[0] // tm
    n_n = d // tn
    clamp = lambda j, i: (jnp.minimum(i, n_full - 1), 0)
    assert n_slabs <= n_n * (n_full + 1)
    step_of = lambda j, i: j * (n_full + 1) + i
    a_in, a_out, a_shape = _slab_specs(side_a, n_slabs, step_of)
    b_in, b_out, b_shape = _slab_specs(side_b, n_slabs, step_of)
    return pl.pallas_call(
        functools.partial(_merge_kernel, n_full=n_full, n_slabs=n_slabs),
        grid=(n_n, n_full + 1),
        in_specs=[pl.BlockSpec((tm, d), clamp),
                  pl.BlockSpec((tm, dc), clamp),
                  pl.BlockSpec((tm, dh), clamp),
                  pl.BlockSpec((tail, d), lambda j, i: (0, 0)),
                  pl.BlockSpec((tail, dc), lambda j, i: (0, 0)),
                  pl.BlockSpec((tail, dh), lambda j, i: (0, 0)),
                  pl.BlockSpec((d, tn), lambda j, i: (0, col_ga // tn + j)),
                  pl.BlockSpec((d, tn), lambda j, i: (0, col_ga // tn + n_n + j)),
                  pl.BlockSpec((dc, tn), lambda j, i: (0, j), pipeline_mode=pl.Buffered(1)),
                  pl.BlockSpec((dh, tn), lambda j, i: (0, j), pipeline_mode=pl.Buffered(1)),
                  a_in, b_in],
        out_specs=[pl.BlockSpec((tm, tn), lambda j, i: (i, j)), a_out, b_out],
        out_shape=[jax.ShapeDtypeStruct((m, d), BF16), a_shape, b_shape],
        scratch_shapes=[pltpu.VMEM((2, dc, tn), BF16)],
        compiler_params=_params(("arbitrary", "arbitrary")),
        name="merge",
    )(hx, a, o, hx_tail, a_tail, o_tail, w_in, w_in, w_a, w_b, side_a, side_b)


def _outproj_kernel(xp_ref, xs_ref, mix_ref, w_ref, g_ref, x1_ref, h2_ref, *, n_full):
    i = pl.program_id(0)

    def body(rows, is_tail):
        x = xs_ref[...] if is_tail else xp_ref[...]
        x1 = x + _dot(mix_ref[0:rows, :], w_ref[...])
        x1_ref[0:rows, :] = x1
        h2_ref[0:rows, :] = _rms_scale(x1, g_ref[...]).astype(h2_ref.dtype)

    _row_split(i, n_full, xp_ref.shape[0], xs_ref.shape[0], body)


def _outproj(xp, xs, mix, w_out, g, tm):
    mp, d = xp.shape
    tail = xs.shape[0]
    n_full = mp // tm
    return pl.pallas_call(
        functools.partial(_outproj_kernel, n_full=n_full),
        grid=(n_full + 1,),
        in_specs=[pl.BlockSpec((tm, d), lambda i: (jnp.minimum(i, n_full - 1), 0)),
                  pl.BlockSpec((tail, d), lambda i: (0, 0)),
                  pl.BlockSpec((tm, d), lambda i: (i, 0)),
                  pl.BlockSpec((d, d), lambda i: (0, 0)),
                  pl.BlockSpec((1, d), lambda i: (0, 0))],
        out_specs=[pl.BlockSpec((tm, d), lambda i: (i, 0)),
                   pl.BlockSpec((tm, d), lambda i: (i, 0))],
        out_shape=[jax.ShapeDtypeStruct((mp + tail, d), F32),
                   jax.ShapeDtypeStruct((mp + tail, d), BF16)],
        compiler_params=_params(("arbitrary",)),
        name="outproj",
    )(xp, xs, mix, w_out, g.reshape(1, d))


def _mlp_kernel(h2_ref, x1_ref, wup_ref, wdn_ref, g_ref, yp_ref, ys_ref, *, n_full):
    i, j = pl.program_id(0), pl.program_id(1)
    last = pl.num_programs(1) - 1

    def body(rows, is_tail):
        y_ref = ys_ref if is_tail else yp_ref

        def step(first, final):
            halves = 2 if (final and not is_tail) else 1
            for hf in range(halves):
                rs = slice(hf * rows // halves, (hf + 1) * rows // halves)
                h = jnp.maximum(_dot(h2_ref[rs, :], wup_ref[...]), 0.0)
                acc = (x1_ref[rs, :] if first else y_ref[rs, :]) + _dot((h * h).astype(BF16), wdn_ref[...])
                y_ref[rs, :] = _rms_scale(acc, g_ref[...]) if final else acc

        @pl.when(j == 0)
        def _():
            step(True, False)

        @pl.when((j > 0) & (j < last))
        def _():
            step(False, False)

        @pl.when(j == last)
        def _():
            step(False, True)

    _row_split(i, n_full, yp_ref.shape[0], ys_ref.shape[0], body)


def _mlp(h2, x1, w_up, w_down, g, tm, tf, tail):
    m, d = h2.shape
    dff = w_up.shape[1]
    assert dff // tf >= 2
    n_full = (m - tail) // tm
    return pl.pallas_call(
        functools.partial(_mlp_kernel, n_full=n_full),
        grid=(n_full + 1, dff // tf),
        in_specs=[pl.BlockSpec((tm, d), lambda i, j: (i, 0)),
                  pl.BlockSpec((tm, d), lambda i, j: (i, 0)),
                  pl.BlockSpec((d, tf), lambda i, j: (0, j)),
                  pl.BlockSpec((tf, d), lambda i, j: (j, 0)),
                  pl.BlockSpec((1, d), lambda i, j: (0, 0))],
        out_specs=[pl.BlockSpec((tm, d), lambda i, j: (jnp.minimum(i, n_full - 1), 0)),
                   pl.BlockSpec((tail, d), lambda i, j: (0, 0))],
        out_shape=[jax.ShapeDtypeStruct((m - tail, d), F32),
                   jax.ShapeDtypeStruct((tail, d), F32)],
        compiler_params=_params(("arbitrary", "arbitrary")),
        name="mlp",
    )(h2, x1, w_up, w_down, g.reshape(1, d))


def kernel(x_prompt, x_sample, state_conv, state_hgrn, norm_mix, w_in, conv_w, lb_logits, onorm_g,
           w_branch_a, w_branch_b, w_out, norm_ffn, w_up, w_down, norm_final):
    batch, seq, d = x_prompt.shape
    n_dec = x_sample.shape[0]
    depth, _, d_conv = conv_w.shape
    heads, dk = state_hgrn.shape[2], state_hgrn.shape[3]
    d_hgrn = heads * dk
    mp = batch * seq
    assert depth == 1 and x_sample.shape[1] == 1 and dk == HEAD_DIM and state_hgrn.shape[4] == dk
    assert seq % ROW_BLOCK == 0 and mp % MLP_ROW_BLOCK == 0 and ROW_BLOCK % n_dec == 0
    col_hgrn = 0
    col_ga = col_hgrn + 4 * d_hgrn
    w_in0 = w_in.reshape(w_in.shape[1:])
    w_a0 = w_branch_a.reshape(w_branch_a.shape[1:])
    w_b0 = w_branch_b.reshape(w_branch_b.shape[1:])
    w_out0 = w_out.reshape(w_out.shape[1:])
    w_up0 = w_up.reshape(w_up.shape[1:])
    w_down0 = w_down.reshape(w_down.shape[1:])

    xp = x_prompt.reshape(mp, d)
    xs = x_sample.reshape(n_dec, d)
    hx, a_p, conv_p, w_rest_b = _conv_prompt(xp, norm_mix[0], w_in0, conv_w[0], batch, seq, d_conv,
                                             ROW_BLOCK, CONV_COL_BLOCK)
    st2 = state_conv[0].reshape(n_dec, 2 * d_conv)
    hx_s, a_s, u_s = _conv_sample(xs, norm_mix[0], w_in0, conv_w[0], st2, d_conv, CONV_COL_BLOCK)

    p_s = _proj(hx_s, w_rest_b, col_hgrn, 4 * d_hgrn, PROJ_COL_BLOCK)
    o_p, hgrn_p, w_up_b, o_s, hgrn_s = _hgrn_prompt(
        hx, w_rest_b, col_hgrn, lb_logits, onorm_g[0], batch, seq, heads, HGRN_ROWS, HGRN_HEADS, w_up0,
        p_s.reshape(n_dec, 4 * heads, dk), state_hgrn[0])

    mix, w_out_b, w_down_b = _merge(hx, a_p, o_p, hx_s, a_s, o_s.reshape(n_dec, d_hgrn), w_rest_b, w_a0, w_b0,
                                    col_ga, ROW_BLOCK, MERGE_COL_BLOCK, w_out0, w_down0, SIDE_SLABS // 2)
    x1, h2 = _outproj(xp, xs, mix, w_out_b, norm_ffn[0], ROW_BLOCK)
    y_p, y_s = _mlp(h2, x1, w_up_b, w_down_b, norm_final, MLP_ROW_BLOCK, MLP_FF_BLOCK, n_dec)

    conv_s = jnp.stack([state_conv[0, :, 1, :], u_s], axis=1)
    return (y_p.reshape(batch, seq, d), y_s.reshape(n_dec, 1, d),
            conv_p[None], hgrn_p[None], conv_s[None], hgrn_s[None])
```

```python
import functools

import jax
import jax.numpy as jnp
from jax import lax
from jax.experimental import pallas as pl
from jax.experimental.pallas import tpu as pltpu

EPS = 1e-6
CHUNK = 64
HEAD_DIM = 128
V7X_VMEM_BYTES = 64 * 1024 * 1024
VMEM_LIMIT = V7X_VMEM_BYTES * 7 // 8

ROW_BLOCK = 512
MLP_ROW_BLOCK = 1024
MLP_FF_BLOCK = 512
CONV_COL_BLOCK = 512
CONV_SUB_BLOCK = 256
PROJ_COL_BLOCK = 1024
MERGE_COL_BLOCK = 1024
SIDE_SLABS = 64
HGRN_ROWS = 512
HGRN_SUB_ROWS = 256
HGRN_HEADS = 4

BF16 = jnp.bfloat16
F32 = jnp.float32


def _params(semantics):
    return pltpu.CompilerParams(dimension_semantics=semantics, vmem_limit_bytes=VMEM_LIMIT)


def _dot(a, b):
    return jnp.dot(a, b, preferred_element_type=F32)


def _dot_nt(a, b):
    return lax.dot_general(a, b, (((1,), (1,)), ((), ())), preferred_element_type=F32)


def _dot_tn(a, b):
    return lax.dot_general(a, b, (((0,), (0,)), ((), ())), preferred_element_type=F32)


def _sigmoid(x):
    return jax.nn.sigmoid(x)


def _silu(x):
    return x * jax.nn.sigmoid(x)


def _rms_scale(x, g):
    ms = jnp.mean(x * x, axis=-1, keepdims=True)
    return x * lax.rsqrt(ms + EPS) * g


def _row_split(i, n_full, full_rows, tail_rows, body):
    @pl.when(i < n_full)
    def _():
        body(full_rows, False)

    @pl.when(i == n_full)
    def _():
        body(tail_rows, True)


def _slab_cast(step, n_slabs, src_ref, dst_ref):
    @pl.when(step < n_slabs)
    def _():
        dst_ref[...] = src_ref[...].astype(dst_ref.dtype)


def _slab_specs(w, n_slabs, step_of):
    r, c = w.shape
    assert r % n_slabs == 0

    def index(*g):
        return (jnp.minimum(step_of(*g), n_slabs - 1), 0)

    spec = pl.BlockSpec((r // n_slabs, c), index)
    return spec, spec, jax.ShapeDtypeStruct((r, c), BF16)


def _conv_prompt_kernel(x_ref, g_ref, whc_ref, wbg_ref, wcg_ref, cw_ref, side_ref,
                        hx_out_ref, a_ref, nc_ref, side_out_ref,
                        h_ref, wb_ref, carry_ref, *, n_slabs, n_t, n_c):
    s = pl.program_id(0)
    tm, tc = a_ref.shape
    j = jnp.maximum(s - 1, 0)
    t = lax.rem(j // n_c, n_t)
    _slab_cast(s, n_slabs, side_ref, side_out_ref)

    def normalise(slot):
        hx = _rms_scale(x_ref[...], g_ref[...]).astype(BF16)
        h_ref[slot] = hx
        hx_out_ref[...] = hx
        return hx

    @pl.when(s == 0)
    def _():
        normalise(0)

    @pl.when((s >= 1) & (s <= n_c))
    def _():
        for c in range(n_c):
            @pl.when(s == c + 1)
            def _(c=c):
                wb_ref[c, 0] = whc_ref[...].astype(BF16)
                wb_ref[c, 1] = wbg_ref[...].astype(BF16)
                wb_ref[c, 2] = wcg_ref[...].astype(BF16)

    cw = cw_ref[...]
    row = lax.broadcasted_iota(jnp.int32, (tm, CONV_SUB_BLOCK), 0)

    def body(c, slot):
        @pl.when((t == 0) & (c == 0))
        def _():
            carry_ref[...] = jnp.zeros_like(carry_ref)

        last_channel = c == n_c - 1
        hx_next = normalise(1 - slot) if last_channel else None
        hx = h_ref[slot]
        n_sub = tc // CONV_SUB_BLOCK
        for sub in range(n_sub):
            cs = slice(sub * CONV_SUB_BLOCK, (sub + 1) * CONV_SUB_BLOCK)
            lhs = jnp.where(s < 0, hx_next, hx) if (last_channel and sub == n_sub - 1) else hx
            hc = _dot(lhs, wb_ref[c, 0, :, cs])
            bg = _dot(lhs, wb_ref[c, 1, :, cs])
            cg = _dot(lhs, wb_ref[c, 2, :, cs])
            u = cg * hc
            c0 = carry_ref[c, 0:1, cs]
            c1 = carry_ref[c, 1:2, cs]
            u1 = jnp.where(row == 0, c1, pltpu.roll(u, 1, 0))
            u2 = jnp.where(row == 0, c0, jnp.where(row == 1, c1, pltpu.roll(u, 2, 0)))
            conv = cw[0:1, cs] * u2 + cw[1:2, cs] * u1 + cw[2:3, cs] * u
            a_ref[:, cs] = (bg * conv).astype(a_ref.dtype)
            carry_ref[c, 0:2, cs] = u[tm - 2:tm, :]

        @pl.when(t == n_t - 1)
        def _():
            nc_ref[0, :, c * tc:(c + 1) * tc] = carry_ref[c, 0:2, :]

    phase = lax.rem(j, 2 * n_c)
    for m in range(2 * n_c):
        @pl.when((s >= 1) & (phase == m))
        def _(m=m):
            body(m % n_c, m // n_c)


def _conv_prompt(x, g, w_in, conv_w, batch, seq, d_conv, tm, tc):
    mp, d = x.shape
    n_c = d_conv // tc
    n_t = seq // tm
    rows_blocks = batch * n_t
    n_steps = n_c * rows_blocks
    conv = lambda s: jnp.maximum(s - 1, 0)
    norm = lambda s: jnp.minimum(s // n_c, rows_blocks - 1)
    col_rest = 3 * d_conv
    sw = tc * n_c
    n_col = (w_in.shape[1] - col_rest) // sw
    assert n_steps == 4 * n_col and col_rest % sw == 0
    tile = lambda s: jnp.minimum(s, n_steps - 1)
    side_in = pl.BlockSpec((d // 4, sw), lambda s: (tile(s) % 4, col_rest // sw + tile(s) // 4))
    side_out = pl.BlockSpec((d // 4, sw), lambda s: (tile(s) % 4, tile(s) // 4))
    side_shape = jax.ShapeDtypeStruct((d, w_in.shape[1] - col_rest), BF16)

    def wcol(k):
        return pl.BlockSpec((d, tc), lambda s: (0, k * n_c + jnp.minimum(conv(s), n_c - 1)),
                            pipeline_mode=pl.Buffered(1))

    return pl.pallas_call(
        functools.partial(_conv_prompt_kernel, n_slabs=n_steps, n_t=n_t, n_c=n_c),
        grid=(n_steps + 1,),
        in_specs=[pl.BlockSpec((tm, d), lambda s: (norm(s), 0)),
                  pl.BlockSpec((1, d), lambda s: (0, 0)),
                  wcol(0), wcol(1), wcol(2),
                  pl.BlockSpec((3, tc), lambda s: (0, conv(s) % n_c)),
                  side_in],
        out_specs=[pl.BlockSpec((tm, d), lambda s: (norm(s), 0)),
                   pl.BlockSpec((tm, tc), lambda s: (conv(s) // n_c, conv(s) % n_c)),
                   pl.BlockSpec((1, 2, d_conv), lambda s: (conv(s) // n_c // n_t, 0, 0)),
                   side_out],
        out_shape=[jax.ShapeDtypeStruct((mp, d), BF16),
                   jax.ShapeDtypeStruct((mp, d_conv), BF16),
                   jax.ShapeDtypeStruct((batch, 2, d_conv), F32),
                   side_shape],
        scratch_shapes=[pltpu.VMEM((2, tm, d), BF16), pltpu.VMEM((n_c, 3, d, tc), BF16),
                        pltpu.VMEM((n_c, 8, tc), F32)],
        compiler_params=_params(("arbitrary",)),
        name="conv_prompt",
    )(x, g.reshape(1, d), w_in, w_in, w_in, conv_w, w_in)


def _conv_sample_kernel(x_ref, g_ref, whc_ref, wbg_ref, wcg_ref, cw_ref, s0_ref, s1_ref, hx_ref, a_ref, u_ref):
    hx = _rms_scale(x_ref[...], g_ref[...]).astype(BF16)
    hx_ref[...] = hx
    hc = _dot(hx, whc_ref[...].astype(BF16))
    bg = _dot(hx, wbg_ref[...].astype(BF16))
    cg = _dot(hx, wcg_ref[...].astype(BF16))
    u = cg * hc
    cw = cw_ref[...]
    conv = cw[0:1, :] * s0_ref[...] + cw[1:2, :] * s1_ref[...] + cw[2:3, :] * u
    a_ref[...] = (bg * conv).astype(a_ref.dtype)
    u_ref[...] = u


def _conv_sample(x, g, w_in, conv_w, state2d, d_conv, tc):
    n, d = x.shape
    n_c = d_conv // tc
    return pl.pallas_call(
        _conv_sample_kernel,
        grid=(n_c,),
        in_specs=[pl.BlockSpec((n, d), lambda c: (0, 0)),
                  pl.BlockSpec((1, d), lambda c: (0, 0)),
                  pl.BlockSpec((d, tc), lambda c: (0, c)),
                  pl.BlockSpec((d, tc), lambda c: (0, n_c + c)),
                  pl.BlockSpec((d, tc), lambda c: (0, 2 * n_c + c)),
                  pl.BlockSpec((3, tc), lambda c: (0, c)),
                  pl.BlockSpec((n, tc), lambda c: (0, c)),
                  pl.BlockSpec((n, tc), lambda c: (0, n_c + c))],
        out_specs=[pl.BlockSpec((n, d), lambda c: (0, 0)),
                   pl.BlockSpec((n, tc), lambda c: (0, c)),
                   pl.BlockSpec((n, tc), lambda c: (0, c))],
        out_shape=[jax.ShapeDtypeStruct((n, d), BF16),
                   jax.ShapeDtypeStruct((n, d_conv), BF16),
                   jax.ShapeDtypeStruct((n, d_conv), F32)],
        compiler_params=_params(("arbitrary",)),
        name="conv_sample",
    )(x, g.reshape(1, d), w_in, w_in, w_in, conv_w, state2d, state2d)


def _proj_kernel(x_ref, w_ref, o_ref):
    o_ref[...] = _dot(x_ref[...], w_ref[...])


def _proj(x, w, col0, n, tn):
    rows, d = x.shape
    return pl.pallas_call(
        _proj_kernel,
        grid=(n // tn,),
        in_specs=[pl.BlockSpec((rows, d), lambda j: (0, 0)),
                  pl.BlockSpec((d, tn), lambda j: (0, col0 // tn + j))],
        out_specs=pl.BlockSpec((rows, tn), lambda j: (0, j)),
        out_shape=jax.ShapeDtypeStruct((rows, n), F32),
        compiler_params=_params(("arbitrary",)),
        name="hgrn_proj_decode",
    )(x, w)


def _lower_bound(lbl):
    e = jnp.exp(lbl - jnp.max(lbl, axis=0))
    return e[0] / jnp.sum(e, axis=0)


def _head_norm_gate(o, g, og):
    return _rms_scale(o, g) * _silu(og)


def _cumsum_rows(x, tril_bf16):
    hi = x.astype(BF16)
    r1 = x - hi.astype(F32)
    mid = r1.astype(BF16)
    lo = (r1 - mid.astype(F32)).astype(BF16)
    n = x.shape[1]
    parts = _dot(tril_bf16, jnp.concatenate([hi, mid, lo], axis=1))
    return parts[:, 0:n] + parts[:, n:2 * n] + parts[:, 2 * n:3 * n]


def _hgrn_decode_step(p_ref, lb, g, s0_ref, o_ref, row0, sn_ref):
    bb, heads, dk = s0_ref.shape[0], s0_ref.shape[1], s0_ref.shape[2]
    head_row = lax.broadcasted_iota(jnp.int32, (heads, dk), 0)

    def column(row):
        return jnp.broadcast_to(row, (dk, dk)).T

    for i in range(bb):
        p = p_ref[i]
        qs = _silu(p[0:heads])
        f = lb + (1.0 - lb) * _sigmoid(p[heads:2 * heads])
        kk = 1.0 - f
        v = p[2 * heads:3 * heads]
        og = p[3 * heads:4 * heads]
        q_in = (qs * f).astype(BF16)
        o_inter = jnp.zeros((heads, dk), F32)
        for h in range(heads):
            s = s0_ref[i, h]
            sn_ref[i, h] = column(f[h:h + 1, :]) * s + column(kk[h:h + 1, :]) * v[h:h + 1, :]
            o_inter = o_inter + jnp.where(head_row == h, _dot(q_in, s.astype(BF16)), 0.0)
        o = jnp.sum(qs * kk, axis=-1, keepdims=True) * v + o_inter
        o = _head_norm_gate(o, g, og)
        o_ref[pl.ds(row0 + i, 1), :] = jnp.concatenate([o[h:h + 1, :] for h in range(heads)], axis=1)


def _hgrn_prompt_kernel(hx_ref, wq_ref, wf_ref, wi_ref, wo_ref, lbl_ref, g_ref, dp_ref, dlbl_ref, ds0_ref, *refs,
                        n_side, n_slabs, n_t):
    side_refs, refs = refs[:n_side], refs[n_side:]
    o_ref, s_ref, do_ref, dsn_ref = refs[:4]
    side_out_refs, (pa_ref, pb_ref, st_ref) = refs[4:4 + n_side], refs[4 + n_side:]
    s = pl.program_id(0)
    for side_ref, side_out_ref in zip(side_refs, side_out_refs):
        _slab_cast(s, n_slabs, side_ref, side_out_ref)
    tb = hx_ref.shape[0]
    heads_per_step = st_ref.shape[0]
    dk = HEAD_DIM
    sub = HGRN_SUB_ROWS
    n_sub = tb // sub
    n_chunks = sub // CHUNK
    t = lax.rem(jnp.maximum(s - 1, 0), n_t)

    @pl.when(s == 0)
    def _():
        pb_ref[...] = jnp.zeros_like(pb_ref)

    @pl.when(t == 0)
    def _():
        st_ref[...] = jnp.zeros_like(st_ref)

    lb_all = _lower_bound(lbl_ref[...])
    g = g_ref[...]
    row = lax.broadcasted_iota(jnp.int32, (sub, sub), 0)
    col = lax.broadcasted_iota(jnp.int32, (sub, sub), 1)
    shift = CHUNK.bit_length() - 1
    causal = (row >= col) & (jnp.right_shift(row, shift) == jnp.right_shift(col, shift))
    tril = causal.astype(BF16)

    def sub_block(p_next_ref, p_ref, sb):
        heads = range(heads_per_step)
        cols = [slice(hh * dk, (hh + 1) * dk) for hh in heads]
        rows = slice(sb * sub, (sb + 1) * sub)
        seqs = ds0_ref.shape[0] // n_sub
        own = pl.ds(sb * seqs, seqs)

        def project(k, w_ref):
            p_next_ref[k, rows] = _dot(hx_ref[rows, :], w_ref[...])

        qs, kk, lf = [], [], []
        for hh in heads:
            lb = lb_all[:, cols[hh]]
            f = lb + (1.0 - lb) * _sigmoid(p_ref[1, rows, cols[hh]])
            qs.append(_silu(p_ref[0, rows, cols[hh]]))
            kk.append(1.0 - f)
            lf.append(jnp.log(f))
        project(0, wq_ref)
        b = [_cumsum_rows(lf[hh], tril) for hh in heads]
        q_in, k_in, k_end, decay = [], [], [], []
        for hh in heads:
            b3 = b[hh].reshape(n_chunks, CHUNK, dk)
            b_last = b3[:, CHUNK - 1:CHUNK, :]
            b_end = jnp.broadcast_to(b_last, b3.shape).reshape(sub, dk)
            q_in.append((qs[hh] * jnp.exp(b[hh])).astype(BF16))
            k_in.append((kk[hh] * jnp.exp(-b[hh])).astype(BF16))
            k_end.append((kk[hh] * jnp.exp(b_end - b[hh])).astype(BF16))
            decay.append(jnp.exp(b_last))
        project(1, wf_ref)
        v = [p_ref[2, rows, cols[hh]].astype(BF16) for hh in heads]
        raw = [_dot_nt(q_in[hh], k_in[hh]) for hh in heads]
        delta = [[_dot_tn(v[hh][c * CHUNK:(c + 1) * CHUNK], k_end[hh][c * CHUNK:(c + 1) * CHUNK])
                  for c in range(n_chunks)] for hh in heads]
        scores = [jnp.where(causal, raw[hh], 0.0).astype(BF16) for hh in heads]
        starts = []
        for hh in heads:
            st = st_ref[hh]
            per_chunk = []
            for c in range(n_chunks):
                per_chunk.append(st.astype(BF16))
                st = decay[hh][c] * st + delta[hh][c]
            st_ref[hh] = st
            starts.append(per_chunk)
        project(2, wi_ref)
        row0 = jnp.minimum(s, n_slabs - 1) * ds0_ref.shape[0] + sb * seqs
        _hgrn_decode_step(dp_ref.at[own], _lower_bound(dlbl_ref[...]), g, ds0_ref.at[own], do_ref, row0,
                          dsn_ref.at[own])
        o = []
        for hh in heads:
            inter = [_dot_nt(q_in[hh][c * CHUNK:(c + 1) * CHUNK], starts[hh][c]) for c in range(n_chunks)]
            o.append(_dot(scores[hh], v[hh]) + jnp.concatenate(inter, axis=0))
        for hh in heads:
            o_ref[rows, cols[hh]] = _head_norm_gate(o[hh], g, p_ref[3, rows, cols[hh]]).astype(o_ref.dtype)
        project(3, wo_ref)

    def body(p_next_ref, p_ref):
        for sb in range(n_sub):
            sub_block(p_next_ref, p_ref, sb)

    parity = lax.rem(s, 2)

    @pl.when(parity == 0)
    def _():
        body(pa_ref, pb_ref)

    @pl.when(parity == 1)
    def _():
        body(pb_ref, pa_ref)

    @pl.when(t == n_t - 1)
    def _():
        for hh in range(heads_per_step):
            s_ref[0, hh] = st_ref[hh].T


def _hgrn_prompt(hx, w_in, col0, lb_logits, onorm_g, batch, seq, heads, tb, hp, side_ws, dec_p, dec_state):
    d = hx.shape[1]
    n_t = seq // tb
    n_h = heads // hp
    dk = HEAD_DIM
    wc = hp * dk
    assert CHUNK & (CHUNK - 1) == 0 and col0 % wc == 0
    rows_blocks = batch * n_t
    n_steps = n_h * rows_blocks
    proj = lambda s: jnp.minimum(s, n_steps - 1)
    rec = lambda s: jnp.maximum(s - 1, 0)
    sides = [_slab_specs(w, n_steps, lambda s: s) for w in side_ws]

    n_dec, dec_rows = dec_p.shape[0], dec_p.shape[1]
    assert n_dec % n_steps == 0
    db = n_dec // n_steps
    dec = lambda s: jnp.minimum(s, n_steps - 1)

    def wcol(k):
        return pl.BlockSpec((d, wc), lambda s: (0, col0 // wc + k * n_h + proj(s) // rows_blocks))

    return pl.pallas_call(
        functools.partial(_hgrn_prompt_kernel, n_side=len(sides), n_slabs=n_steps, n_t=n_t),
        grid=(n_steps + 1,),
        in_specs=[pl.BlockSpec((tb, d), lambda s: (proj(s) % rows_blocks, 0)),
                  wcol(0), wcol(1), wcol(2), wcol(3),
                  pl.BlockSpec((2, 1, wc), lambda s: (0, 0, rec(s) // rows_blocks)),
                  pl.BlockSpec((1, dk), lambda s: (0, 0)),
                  pl.BlockSpec((db, dec_rows, dk), lambda s: (dec(s), 0, 0)),
                  pl.BlockSpec((2, heads, dk), lambda s: (0, 0, 0)),
                  pl.BlockSpec((db, heads, dk, dk), lambda s: (dec(s), 0, 0, 0))]
        + [side[0] for side in sides],
        out_specs=[pl.BlockSpec((tb, wc), lambda s: (rec(s) % rows_blocks, rec(s) // rows_blocks)),
                   pl.BlockSpec((1, hp, dk, dk),
                                lambda s: ((rec(s) % rows_blocks) // n_t, rec(s) // rows_blocks, 0, 0)),
                   pl.BlockSpec((n_dec, heads * dk), lambda s: (0, 0)),
                   pl.BlockSpec((db, heads, dk, dk), lambda s: (dec(s), 0, 0, 0))]
        + [side[1] for side in sides],
        out_shape=[jax.ShapeDtypeStruct((batch * seq, heads * dk), BF16),
                   jax.ShapeDtypeStruct((batch, heads, dk, dk), F32),
                   jax.ShapeDtypeStruct((n_dec, heads * dk), F32),
                   jax.ShapeDtypeStruct((n_dec, heads, dk, dk), F32)]
        + [side[2] for side in sides],
        scratch_shapes=[pltpu.VMEM((4, tb, wc), F32), pltpu.VMEM((4, tb, wc), F32),
                        pltpu.VMEM((hp, dk, dk), F32)],
        compiler_params=_params(("arbitrary",)),
        name="hgrn_prompt",
    )(hx, w_in, w_in, w_in, w_in, lb_logits.reshape(2, 1, heads * dk), onorm_g.reshape(1, dk),
      dec_p, lb_logits.reshape(2, heads, dk), dec_state, *side_ws)


def _merge_kernel(hx_ref, a_ref, o_ref, ht_ref, at_ref, ot_ref, wga_ref, wgb_ref, wa_ref, wb_ref, side_a_ref,
                  side_b_ref, mix_ref, side_a_out_ref, side_b_out_ref, *, n_full, n_slabs):
    j, i = pl.program_id(0), pl.program_id(1)
    _slab_cast(j * pl.num_programs(1) + i, n_slabs, side_a_ref, side_a_out_ref)
    _slab_cast(j * pl.num_programs(1) + i, n_slabs, side_b_ref, side_b_out_ref)

    def body(rows, is_tail):
        hx = ht_ref[...] if is_tail else hx_ref[...]
        a = at_ref[...] if is_tail else a_ref[...]
        o = ot_ref[...].astype(BF16) if is_tail else o_ref[...]
        ga = _sigmoid(_dot(hx, wga_ref[...]))
        gb = _sigmoid(_dot(hx, wgb_ref[...]))
        mix = ga * _dot(a, wa_ref[...]) + gb * _dot(o, wb_ref[...])
        mix_ref[0:rows, :] = mix.astype(mix_ref.dtype)

    _row_split(i, n_full, hx_ref.shape[0], ht_ref.shape[0], body)


def _merge(hx, a, o, hx_tail, a_tail, o_tail, w_in, w_a, w_b, col_ga, tm, tn, side_a, side_b, n_slabs):
    d = hx.shape[1]
    m = a.shape[0] + hx_tail.shape[0]
    dc, dh = a.shape[1], o.shape[1]
    tail = a_tail.shape[0]
    n_full = a.shape[0] // tm
    n_n = d // tn
    clamp = lambda j, i: (jnp.minimum(i, n_full - 1), 0)
    assert n_slabs <= n_n * (n_full + 1)
    step_of = lambda j, i: j * (n_full + 1) + i
    a_in, a_out, a_shape = _slab_specs(side_a, n_slabs, step_of)
    b_in, b_out, b_shape = _slab_specs(side_b, n_slabs, step_of)
    return pl.pallas_call(
        functools.partial(_merge_kernel, n_full=n_full, n_slabs=n_slabs),
        grid=(n_n, n_full + 1),
        in_specs=[pl.BlockSpec((tm, d), clamp),
                  pl.BlockSpec((tm, dc), clamp),
                  pl.BlockSpec((tm, dh), clamp),
                  pl.BlockSpec((tail, d), lambda j, i: (0, 0)),
                  pl.BlockSpec((tail, dc), lambda j, i: (0, 0)),
                  pl.BlockSpec((tail, dh), lambda j, i: (0, 0)),
                  pl.BlockSpec((d, tn), lambda j, i: (0, col_ga // tn + j)),
                  pl.BlockSpec((d, tn), lambda j, i: (0, col_ga // tn + n_n + j)),
                  pl.BlockSpec((dc, tn), lambda j, i: (0, j)),
                  pl.BlockSpec((dh, tn), lambda j, i: (0, j)),
                  a_in, b_in],
        out_specs=[pl.BlockSpec((tm, tn), lambda j, i: (i, j)), a_out, b_out],
        out_shape=[jax.ShapeDtypeStruct((m, d), BF16), a_shape, b_shape],
        compiler_params=_params(("arbitrary", "arbitrary")),
        name="merge",
    )(hx, a, o, hx_tail, a_tail, o_tail, w_in, w_in, w_a, w_b, side_a, side_b)


def _outproj_kernel(xp_ref, xs_ref, mix_ref, w_ref, g_ref, x1_ref, h2_ref, *, n_full):
    i = pl.program_id(0)

    def body(rows, is_tail):
        x = xs_ref[...] if is_tail else xp_ref[...]
        x1 = x + _dot(mix_ref[0:rows, :], w_ref[...])
        x1_ref[0:rows, :] = x1
        h2_ref[0:rows, :] = _rms_scale(x1, g_ref[...]).astype(h2_ref.dtype)

    _row_split(i, n_full, xp_ref.shape[0], xs_ref.shape[0], body)


def _outproj(xp, xs, mix, w_out, g, tm):
    mp, d = xp.shape
    tail = xs.shape[0]
    n_full = mp // tm
    return pl.pallas_call(
        functools.partial(_outproj_kernel, n_full=n_full),
        grid=(n_full + 1,),
        in_specs=[pl.BlockSpec((tm, d), lambda i: (jnp.minimum(i, n_full - 1), 0)),
                  pl.BlockSpec((tail, d), lambda i: (0, 0)),
                  pl.BlockSpec((tm, d), lambda i: (i, 0)),
                  pl.BlockSpec((d, d), lambda i: (0, 0)),
                  pl.BlockSpec((1, d), lambda i: (0, 0))],
        out_specs=[pl.BlockSpec((tm, d), lambda i: (i, 0)),
                   pl.BlockSpec((tm, d), lambda i: (i, 0))],
        out_shape=[jax.ShapeDtypeStruct((mp + tail, d), F32),
                   jax.ShapeDtypeStruct((mp + tail, d), BF16)],
        compiler_params=_params(("arbitrary",)),
        name="outproj",
    )(xp, xs, mix, w_out, g.reshape(1, d))


def _mlp_kernel(h2_ref, x1_ref, wup_ref, wdn_ref, g_ref, yp_ref, ys_ref, *, n_full):
    i, j = pl.program_id(0), pl.program_id(1)
    last = pl.num_programs(1) - 1

    def body(rows, is_tail):
        y_ref = ys_ref if is_tail else yp_ref

        def step(first, final):
            halves = 2 if (final and not is_tail) else 1
            for hf in range(halves):
                rs = slice(hf * rows // halves, (hf + 1) * rows // halves)
                h = jnp.maximum(_dot(h2_ref[rs, :], wup_ref[...]), 0.0)
                acc = (x1_ref[rs, :] if first else y_ref[rs, :]) + _dot((h * h).astype(BF16), wdn_ref[...])
                y_ref[rs, :] = _rms_scale(acc, g_ref[...]) if final else acc

        @pl.when(j == 0)
        def _():
            step(True, False)

        @pl.when((j > 0) & (j < last))
        def _():
            step(False, False)

        @pl.when(j == last)
        def _():
            step(False, True)

    _row_split(i, n_full, yp_ref.shape[0], ys_ref.shape[0], body)


def _mlp(h2, x1, w_up, w_down, g, tm, tf, tail):
    m, d = h2.shape
    dff = w_up.shape[1]
    assert dff // tf >= 2
    n_full = (m - tail) // tm
    return pl.pallas_call(
        functools.partial(_mlp_kernel, n_full=n_full),
        grid=(n_full + 1, dff // tf),
        in_specs=[pl.BlockSpec((tm, d), lambda i, j: (i, 0)),
                  pl.BlockSpec((tm, d), lambda i, j: (i, 0)),
                  pl.BlockSpec((d, tf), lambda i, j: (0, j)),
                  pl.BlockSpec((tf, d), lambda i, j: (j, 0)),
                  pl.BlockSpec((1, d), lambda i, j: (0, 0))],
        out_specs=[pl.BlockSpec((tm, d), lambda i, j: (jnp.minimum(i, n_full - 1), 0)),
                   pl.BlockSpec((tail, d), lambda i, j: (0, 0))],
        out_shape=[jax.ShapeDtypeStruct((m - tail, d), F32),
                   jax.ShapeDtypeStruct((tail, d), F32)],
        compiler_params=_params(("arbitrary", "arbitrary")),
        name="mlp",
    )(h2, x1, w_up, w_down, g.reshape(1, d))


def kernel(x_prompt, x_sample, state_conv, state_hgrn, norm_mix, w_in, conv_w, lb_logits, onorm_g,
           w_branch_a, w_branch_b, w_out, norm_ffn, w_up, w_down, norm_final):
    batch, seq, d = x_prompt.shape
    n_dec = x_sample.shape[0]
    depth, _, d_conv = conv_w.shape
    heads, dk = state_hgrn.shape[2], state_hgrn.shape[3]
    d_hgrn = heads * dk
    mp = batch * seq
    assert depth == 1 and x_sample.shape[1] == 1 and dk == HEAD_DIM and state_hgrn.shape[4] == dk
    assert seq % ROW_BLOCK == 0 and mp % MLP_ROW_BLOCK == 0 and ROW_BLOCK % n_dec == 0
    col_hgrn = 0
    col_ga = col_hgrn + 4 * d_hgrn
    w_in0 = w_in.reshape(w_in.shape[1:])
    w_a0 = w_branch_a.reshape(w_branch_a.shape[1:])
    w_b0 = w_branch_b.reshape(w_branch_b.shape[1:])
    w_out0 = w_out.reshape(w_out.shape[1:])
    w_up0 = w_up.reshape(w_up.shape[1:])
    w_down0 = w_down.reshape(w_down.shape[1:])

    xp = x_prompt.reshape(mp, d)
    xs = x_sample.reshape(n_dec, d)
    hx, a_p, conv_p, w_rest_b = _conv_prompt(xp, norm_mix[0], w_in0, conv_w[0], batch, seq, d_conv,
                                             ROW_BLOCK, CONV_COL_BLOCK)
    st2 = state_conv[0].reshape(n_dec, 2 * d_conv)
    hx_s, a_s, u_s = _conv_sample(xs, norm_mix[0], w_in0, conv_w[0], st2, d_conv, CONV_COL_BLOCK)

    p_s = _proj(hx_s, w_rest_b, col_hgrn, 4 * d_hgrn, PROJ_COL_BLOCK)
    o_p, hgrn_p, o_s, hgrn_s, w_up_b, w_a_b, w_b_b = _hgrn_prompt(
        hx, w_rest_b, col_hgrn, lb_logits, onorm_g[0], batch, seq, heads, HGRN_ROWS, HGRN_HEADS, (w_up0, w_a0, w_b0),
        p_s.reshape(n_dec, 4 * heads, dk), state_hgrn[0])

    mix, w_out_b, w_down_b = _merge(hx, a_p, o_p, hx_s, a_s, o_s, w_rest_b, w_a_b, w_b_b,
                                    col_ga, ROW_BLOCK, MERGE_COL_BLOCK, w_out0, w_down0, SIDE_SLABS // 2)
    x1, h2 = _outproj(xp, xs, mix, w_out_b, norm_ffn[0], ROW_BLOCK)
    y_p, y_s = _mlp(h2, x1, w_up_b, w_down_b, norm_final, MLP_ROW_BLOCK, MLP_FF_BLOCK, n_dec)

    conv_s = jnp.stack([state_conv[0, :, 1, :], u_s], axis=1)
    return (y_p.reshape(batch, seq, d), y_s.reshape(n_dec, 1, d),
            conv_p[None], hgrn_p[None], conv_s[None], hgrn_s[None])
```

```python
import functools

import jax
import jax.numpy as jnp
from jax import lax
from jax.experimental import pallas as pl
from jax.experimental.pallas import tpu as pltpu

EPS = 1e-6
CHUNK = 64
HEAD_DIM = 128
V7X_VMEM_BYTES = 64 * 1024 * 1024
VMEM_LIMIT = V7X_VMEM_BYTES * 7 // 8

ROW_BLOCK = 512
MLP_ROW_BLOCK = 1024
MLP_FF_BLOCK = 512
CONV_COL_BLOCK = 512
CONV_SUB_BLOCK = 256
PROJ_COL_BLOCK = 1024
MERGE_COL_BLOCK = 1024
SIDE_SLABS = 64
HGRN_ROWS = 512
HGRN_SUB_ROWS = 256
HGRN_HEADS = 4

BF16 = jnp.bfloat16
F32 = jnp.float32


def _params(semantics):
    return pltpu.CompilerParams(dimension_semantics=semantics, vmem_limit_bytes=VMEM_LIMIT)


def _dot(a, b):
    return jnp.dot(a, b, preferred_element_type=F32)


def _dot_nt(a, b):
    return lax.dot_general(a, b, (((1,), (1,)), ((), ())), preferred_element_type=F32)


def _dot_tn(a, b):
    return lax.dot_general(a, b, (((0,), (0,)), ((), ())), preferred_element_type=F32)


def _sigmoid(x):
    return jax.nn.sigmoid(x)


def _silu(x):
    return x * jax.nn.sigmoid(x)


def _rms_scale(x, g):
    ms = jnp.mean(x * x, axis=-1, keepdims=True)
    return x * lax.rsqrt(ms + EPS) * g


def _row_split(i, n_full, full_rows, tail_rows, body):
    @pl.when(i < n_full)
    def _():
        body(full_rows, False)

    @pl.when(i == n_full)
    def _():
        body(tail_rows, True)


def _slab_cast(step, n_slabs, src_ref, dst_ref):
    @pl.when(step < n_slabs)
    def _():
        dst_ref[...] = src_ref[...].astype(dst_ref.dtype)


def _slab_specs(w, n_slabs, step_of):
    r, c = w.shape
    assert r % n_slabs == 0

    def index(*g):
        return (jnp.minimum(step_of(*g), n_slabs - 1), 0)

    spec = pl.BlockSpec((r // n_slabs, c), index)
    return spec, spec, jax.ShapeDtypeStruct((r, c), BF16)


def _conv_prompt_kernel(x_ref, g_ref, whc_ref, wbg_ref, wcg_ref, cw_ref, side_ref,
                        hx_out_ref, a_ref, nc_ref, side_out_ref,
                        h_ref, wb_ref, carry_ref, *, n_slabs, n_t, n_c):
    s = pl.program_id(0)
    tm, tc = a_ref.shape
    j = jnp.maximum(s - 1, 0)
    t = lax.rem(j // n_c, n_t)
    _slab_cast(s, n_slabs, side_ref, side_out_ref)

    def normalise(slot):
        hx = _rms_scale(x_ref[...], g_ref[...]).astype(BF16)
        h_ref[slot] = hx
        hx_out_ref[...] = hx
        return hx

    @pl.when(s == 0)
    def _():
        normalise(0)

    @pl.when((s >= 1) & (s <= n_c))
    def _():
        for c in range(n_c):
            @pl.when(s == c + 1)
            def _(c=c):
                wb_ref[c, 0] = whc_ref[...].astype(BF16)
                wb_ref[c, 1] = wbg_ref[...].astype(BF16)
                wb_ref[c, 2] = wcg_ref[...].astype(BF16)

    cw = cw_ref[...]
    row = lax.broadcasted_iota(jnp.int32, (tm, CONV_SUB_BLOCK), 0)

    def body(c, slot):
        @pl.when((t == 0) & (c == 0))
        def _():
            carry_ref[...] = jnp.zeros_like(carry_ref)

        last_channel = c == n_c - 1
        hx_next = normalise(1 - slot) if last_channel else None
        hx = h_ref[slot]
        n_sub = tc // CONV_SUB_BLOCK
        for sub in range(n_sub):
            cs = slice(sub * CONV_SUB_BLOCK, (sub + 1) * CONV_SUB_BLOCK)
            lhs = jnp.where(s < 0, hx_next, hx) if (last_channel and sub == n_sub - 1) else hx
            hc = _dot(lhs, wb_ref[c, 0, :, cs])
            bg = _dot(lhs, wb_ref[c, 1, :, cs])
            cg = _dot(lhs, wb_ref[c, 2, :, cs])
            u = cg * hc
            c0 = carry_ref[c, 0:1, cs]
            c1 = carry_ref[c, 1:2, cs]
            u1 = jnp.where(row == 0, c1, pltpu.roll(u, 1, 0))
            u2 = jnp.where(row == 0, c0, jnp.where(row == 1, c1, pltpu.roll(u, 2, 0)))
            conv = cw[0:1, cs] * u2 + cw[1:2, cs] * u1 + cw[2:3, cs] * u
            a_ref[:, cs] = (bg * conv).astype(a_ref.dtype)
            carry_ref[c, 0:2, cs] = u[tm - 2:tm, :]

        @pl.when(t == n_t - 1)
        def _():
            nc_ref[0, :, c * tc:(c + 1) * tc] = carry_ref[c, 0:2, :]

    phase = lax.rem(j, 2 * n_c)
    for m in range(2 * n_c):
        @pl.when((s >= 1) & (phase == m))
        def _(m=m):
            body(m % n_c, m // n_c)


def _conv_prompt(x, g, w_in, conv_w, batch, seq, d_conv, tm, tc):
    mp, d = x.shape
    n_c = d_conv // tc
    n_t = seq // tm
    rows_blocks = batch * n_t
    n_steps = n_c * rows_blocks
    conv = lambda s: jnp.maximum(s - 1, 0)
    norm = lambda s: jnp.minimum(s // n_c, rows_blocks - 1)
    col_rest = 3 * d_conv
    sw = tc * n_c
    n_col = (w_in.shape[1] - col_rest) // sw
    assert n_steps == 4 * n_col and col_rest % sw == 0
    tile = lambda s: jnp.minimum(s, n_steps - 1)
    side_in = pl.BlockSpec((d // 4, sw), lambda s: (tile(s) % 4, col_rest // sw + tile(s) // 4))
    side_out = pl.BlockSpec((d // 4, sw), lambda s: (tile(s) % 4, tile(s) // 4))
    side_shape = jax.ShapeDtypeStruct((d, w_in.shape[1] - col_rest), BF16)

    def wcol(k):
        return pl.BlockSpec((d, tc), lambda s: (0, k * n_c + jnp.minimum(conv(s), n_c - 1)),
                            pipeline_mode=pl.Buffered(1))

    return pl.pallas_call(
        functools.partial(_conv_prompt_kernel, n_slabs=n_steps, n_t=n_t, n_c=n_c),
        grid=(n_steps + 1,),
        in_specs=[pl.BlockSpec((tm, d), lambda s: (norm(s), 0)),
                  pl.BlockSpec((1, d), lambda s: (0, 0)),
                  wcol(0), wcol(1), wcol(2),
                  pl.BlockSpec((3, tc), lambda s: (0, conv(s) % n_c)),
                  side_in],
        out_specs=[pl.BlockSpec((tm, d), lambda s: (norm(s), 0)),
                   pl.BlockSpec((tm, tc), lambda s: (conv(s) // n_c, conv(s) % n_c)),
                   pl.BlockSpec((1, 2, d_conv), lambda s: (conv(s) // n_c // n_t, 0, 0)),
                   side_out],
        out_shape=[jax.ShapeDtypeStruct((mp, d), BF16),
                   jax.ShapeDtypeStruct((mp, d_conv), BF16),
                   jax.ShapeDtypeStruct((batch, 2, d_conv), F32),
                   side_shape],
        scratch_shapes=[pltpu.VMEM((2, tm, d), BF16), pltpu.VMEM((n_c, 3, d, tc), BF16),
                        pltpu.VMEM((n_c, 8, tc), F32)],
        compiler_params=_params(("arbitrary",)),
        name="conv_prompt",
    )(x, g.reshape(1, d), w_in, w_in, w_in, conv_w, w_in)


def _conv_sample_kernel(x_ref, g_ref, whc_ref, wbg_ref, wcg_ref, cw_ref, st_ref, hx_ref, a_ref, nc_ref):
    hx = _rms_scale(x_ref[...], g_ref[...]).astype(BF16)
    hx_ref[...] = hx
    hc = _dot(hx, whc_ref[...].astype(BF16))
    bg = _dot(hx, wbg_ref[...].astype(BF16))
    cg = _dot(hx, wcg_ref[...].astype(BF16))
    u = cg * hc
    cw = cw_ref[...]
    s1 = st_ref[:, 1, :]
    conv = cw[0:1, :] * st_ref[:, 0, :] + cw[1:2, :] * s1 + cw[2:3, :] * u
    a_ref[...] = (bg * conv).astype(a_ref.dtype)
    nc_ref[:, 0, :] = s1
    nc_ref[:, 1, :] = u


def _conv_sample(x, g, w_in, conv_w, state, d_conv, tc):
    n, d = x.shape
    n_c = d_conv // tc
    return pl.pallas_call(
        _conv_sample_kernel,
        grid=(n_c,),
        in_specs=[pl.BlockSpec((n, d), lambda c: (0, 0)),
                  pl.BlockSpec((1, d), lambda c: (0, 0)),
                  pl.BlockSpec((d, tc), lambda c: (0, c)),
                  pl.BlockSpec((d, tc), lambda c: (0, n_c + c)),
                  pl.BlockSpec((d, tc), lambda c: (0, 2 * n_c + c)),
                  pl.BlockSpec((3, tc), lambda c: (0, c)),
                  pl.BlockSpec((n, 2, tc), lambda c: (0, 0, c))],
        out_specs=[pl.BlockSpec((n, d), lambda c: (0, 0)),
                   pl.BlockSpec((n, tc), lambda c: (0, c)),
                   pl.BlockSpec((n, 2, tc), lambda c: (0, 0, c))],
        out_shape=[jax.ShapeDtypeStruct((n, d), BF16),
                   jax.ShapeDtypeStruct((n, d_conv), BF16),
                   jax.ShapeDtypeStruct((n, 2, d_conv), F32)],
        compiler_params=_params(("arbitrary",)),
        name="conv_sample",
    )(x, g.reshape(1, d), w_in, w_in, w_in, conv_w, state)


def _proj_kernel(x_ref, w_ref, o_ref):
    res = _dot(x_ref[...], w_ref[...])
    dk = o_ref.shape[2]
    for h in range(o_ref.shape[1]):
        o_ref[:, h, :] = res[:, h * dk:(h + 1) * dk]


def _proj(x, w, col0, n, tn, dk):
    rows, d = x.shape
    return pl.pallas_call(
        _proj_kernel,
        grid=(n // tn,),
        in_specs=[pl.BlockSpec((rows, d), lambda j: (0, 0)),
                  pl.BlockSpec((d, tn), lambda j: (0, col0 // tn + j))],
        out_specs=pl.BlockSpec((rows, tn // dk, dk), lambda j: (0, j, 0)),
        out_shape=jax.ShapeDtypeStruct((rows, n // dk, dk), F32),
        compiler_params=_params(("arbitrary",)),
        name="hgrn_proj_decode",
    )(x, w)


def _lower_bound(lbl):
    e = jnp.exp(lbl - jnp.max(lbl, axis=0))
    return e[0] / jnp.sum(e, axis=0)


def _head_norm_gate(o, g, og):
    return _rms_scale(o, g) * _silu(og)


def _cumsum_rows(x, tril_bf16):
    hi = x.astype(BF16)
    r1 = x - hi.astype(F32)
    mid = r1.astype(BF16)
    lo = (r1 - mid.astype(F32)).astype(BF16)
    n = x.shape[1]
    parts = _dot(tril_bf16, jnp.concatenate([hi, mid, lo], axis=1))
    return parts[:, 0:n] + parts[:, n:2 * n] + parts[:, 2 * n:3 * n]


def _hgrn_decode_step(p_ref, lb, g, s0_ref, o_ref, row0, sn_ref):
    bb, heads, dk = s0_ref.shape[0], s0_ref.shape[1], s0_ref.shape[2]
    head_row = lax.broadcasted_iota(jnp.int32, (heads, dk), 0)

    def column(row):
        return jnp.broadcast_to(row, (dk, dk)).T

    for i in range(bb):
        p = p_ref[i]
        qs = _silu(p[0:heads])
        f = lb + (1.0 - lb) * _sigmoid(p[heads:2 * heads])
        kk = 1.0 - f
        v = p[2 * heads:3 * heads]
        og = p[3 * heads:4 * heads]
        q_in = (qs * f).astype(BF16)
        o_inter = jnp.zeros((heads, dk), F32)
        for h in range(heads):
            s = s0_ref[i, h]
            sn_ref[i, h] = column(f[h:h + 1, :]) * s + column(kk[h:h + 1, :]) * v[h:h + 1, :]
            o_inter = o_inter + jnp.where(head_row == h, _dot(q_in, s.astype(BF16)), 0.0)
        o = jnp.sum(qs * kk, axis=-1, keepdims=True) * v + o_inter
        o = _head_norm_gate(o, g, og)
        o_ref[pl.ds(row0 + i, 1), :] = jnp.concatenate([o[h:h + 1, :] for h in range(heads)], axis=1)


def _hgrn_prompt_kernel(hx_ref, wq_ref, wf_ref, wi_ref, wo_ref, lbl_ref, g_ref, dp_ref, dlbl_ref, ds0_ref, *refs,
                        n_side, n_slabs, n_t):
    side_refs, refs = refs[:n_side], refs[n_side:]
    o_ref, s_ref, do_ref, dsn_ref = refs[:4]
    side_out_refs, (pa_ref, pb_ref, st_ref) = refs[4:4 + n_side], refs[4 + n_side:]
    s = pl.program_id(0)
    for side_ref, side_out_ref in zip(side_refs, side_out_refs):
        _slab_cast(s, n_slabs, side_ref, side_out_ref)
    tb = hx_ref.shape[0]
    heads_per_step = st_ref.shape[0]
    dk = HEAD_DIM
    sub = HGRN_SUB_ROWS
    n_sub = tb // sub
    n_chunks = sub // CHUNK
    t = lax.rem(jnp.maximum(s - 1, 0), n_t)

    @pl.when(s == 0)
    def _():
        pb_ref[...] = jnp.zeros_like(pb_ref)

    @pl.when(t == 0)
    def _():
        st_ref[...] = jnp.zeros_like(st_ref)

    lb_all = _lower_bound(lbl_ref[...])
    g = g_ref[...]
    row = lax.broadcasted_iota(jnp.int32, (sub, sub), 0)
    col = lax.broadcasted_iota(jnp.int32, (sub, sub), 1)
    shift = CHUNK.bit_length() - 1
    causal = (row >= col) & (jnp.right_shift(row, shift) == jnp.right_shift(col, shift))
    tril = causal.astype(BF16)

    def sub_block(p_next_ref, p_ref, sb):
        heads = range(heads_per_step)
        cols = [slice(hh * dk, (hh + 1) * dk) for hh in heads]
        rows = slice(sb * sub, (sb + 1) * sub)
        seqs = ds0_ref.shape[0] // n_sub
        own = pl.ds(sb * seqs, seqs)

        def project(k, w_ref):
            p_next_ref[k, rows] = _dot(hx_ref[rows, :], w_ref[...])

        qs, kk, lf = [], [], []
        for hh in heads:
            lb = lb_all[:, cols[hh]]
            f = lb + (1.0 - lb) * _sigmoid(p_ref[1, rows, cols[hh]])
            qs.append(_silu(p_ref[0, rows, cols[hh]]))
            kk.append(1.0 - f)
            lf.append(jnp.log(f))
        project(0, wq_ref)
        b = [_cumsum_rows(lf[hh], tril) for hh in heads]
        q_in, k_in, k_end, decay = [], [], [], []
        for hh in heads:
            b3 = b[hh].reshape(n_chunks, CHUNK, dk)
            b_last = b3[:, CHUNK - 1:CHUNK, :]
            b_end = jnp.broadcast_to(b_last, b3.shape).reshape(sub, dk)
            q_in.append((qs[hh] * jnp.exp(b[hh])).astype(BF16))
            k_in.append((kk[hh] * jnp.exp(-b[hh])).astype(BF16))
            k_end.append((kk[hh] * jnp.exp(b_end - b[hh])).astype(BF16))
            decay.append(jnp.exp(b_last))
        project(1, wf_ref)
        v = [p_ref[2, rows, cols[hh]].astype(BF16) for hh in heads]
        raw = [_dot_nt(q_in[hh], k_in[hh]) for hh in heads]
        delta = [[_dot_tn(v[hh][c * CHUNK:(c + 1) * CHUNK], k_end[hh][c * CHUNK:(c + 1) * CHUNK])
                  for c in range(n_chunks)] for hh in heads]
        scores = [jnp.where(causal, raw[hh], 0.0).astype(BF16) for hh in heads]
        starts = []
        for hh in heads:
            st = st_ref[hh]
            per_chunk = []
            for c in range(n_chunks):
                per_chunk.append(st.astype(BF16))
                st = decay[hh][c] * st + delta[hh][c]
            st_ref[hh] = st
            starts.append(per_chunk)
        project(2, wi_ref)
        row0 = jnp.minimum(s, n_slabs - 1) * ds0_ref.shape[0] + sb * seqs
        _hgrn_decode_step(dp_ref.at[own], _lower_bound(dlbl_ref[...]), g, ds0_ref.at[own], do_ref, row0,
                          dsn_ref.at[own])
        o = []
        for hh in heads:
            inter = [_dot_nt(q_in[hh][c * CHUNK:(c + 1) * CHUNK], starts[hh][c]) for c in range(n_chunks)]
            o.append(_dot(scores[hh], v[hh]) + jnp.concatenate(inter, axis=0))
        for hh in heads:
            o_ref[rows, cols[hh]] = _head_norm_gate(o[hh], g, p_ref[3, rows, cols[hh]]).astype(o_ref.dtype)
        project(3, wo_ref)

    def body(p_next_ref, p_ref):
        for sb in range(n_sub):
            sub_block(p_next_ref, p_ref, sb)

    parity = lax.rem(s, 2)

    @pl.when(parity == 0)
    def _():
        body(pa_ref, pb_ref)

    @pl.when(parity == 1)
    def _():
        body(pb_ref, pa_ref)

    @pl.when(t == n_t - 1)
    def _():
        for hh in range(heads_per_step):
            s_ref[0, hh] = st_ref[hh].T


def _hgrn_prompt(hx, w_in, col0, lb_logits, onorm_g, batch, seq, heads, tb, hp, side_ws, dec_p, dec_state):
    d = hx.shape[1]
    n_t = seq // tb
    n_h = heads // hp
    dk = HEAD_DIM
    wc = hp * dk
    assert CHUNK & (CHUNK - 1) == 0 and col0 % wc == 0
    rows_blocks = batch * n_t
    n_steps = n_h * rows_blocks
    proj = lambda s: jnp.minimum(s, n_steps - 1)
    rec = lambda s: jnp.maximum(s - 1, 0)
    sides = [_slab_specs(w, n_steps, lambda s: s) for w in side_ws]

    n_dec, dec_rows = dec_p.shape[0], dec_p.shape[1]
    assert n_dec % n_steps == 0
    db = n_dec // n_steps
    dec = lambda s: jnp.minimum(s, n_steps - 1)

    def wcol(k):
        return pl.BlockSpec((d, wc), lambda s: (0, col0 // wc + k * n_h + proj(s) // rows_blocks))

    return pl.pallas_call(
        functools.partial(_hgrn_prompt_kernel, n_side=len(sides), n_slabs=n_steps, n_t=n_t),
        grid=(n_steps + 1,),
        in_specs=[pl.BlockSpec((tb, d), lambda s: (proj(s) % rows_blocks, 0)),
                  wcol(0), wcol(1), wcol(2), wcol(3),
                  pl.BlockSpec((2, 1, wc), lambda s: (0, 0, rec(s) // rows_blocks)),
                  pl.BlockSpec((1, dk), lambda s: (0, 0)),
                  pl.BlockSpec((db, dec_rows, dk), lambda s: (dec(s), 0, 0)),
                  pl.BlockSpec((2, heads, dk), lambda s: (0, 0, 0)),
                  pl.BlockSpec((db, heads, dk, dk), lambda s: (dec(s), 0, 0, 0))]
        + [side[0] for side in sides],
        out_specs=[pl.BlockSpec((tb, wc), lambda s: (rec(s) % rows_blocks, rec(s) // rows_blocks)),
                   pl.BlockSpec((1, hp, dk, dk),
                                lambda s: ((rec(s) % rows_blocks) // n_t, rec(s) // rows_blocks, 0, 0)),
                   pl.BlockSpec((n_dec, heads * dk), lambda s: (0, 0)),
                   pl.BlockSpec((db, heads, dk, dk), lambda s: (dec(s), 0, 0, 0))]
        + [side[1] for side in sides],
        out_shape=[jax.ShapeDtypeStruct((batch * seq, heads * dk), BF16),
                   jax.ShapeDtypeStruct((batch, heads, dk, dk), F32),
                   jax.ShapeDtypeStruct((n_dec, heads * dk), F32),
                   jax.ShapeDtypeStruct((n_dec, heads, dk, dk), F32)]
        + [side[2] for side in sides],
        scratch_shapes=[pltpu.VMEM((4, tb, wc), F32), pltpu.VMEM((4, tb, wc), F32),
                        pltpu.VMEM((hp, dk, dk), F32)],
        compiler_params=_params(("arbitrary",)),
        name="hgrn_prompt",
    )(hx, w_in, w_in, w_in, w_in, lb_logits.reshape(2, 1, heads * dk), onorm_g.reshape(1, dk),
      dec_p, lb_logits.reshape(2, heads, dk), dec_state, *side_ws)


def _merge_kernel(hx_ref, a_ref, o_ref, ht_ref, at_ref, ot_ref, wga_ref, wgb_ref, wa_ref, wb_ref, side_a_ref,
                  side_b_ref, mix_ref, side_a_out_ref, side_b_out_ref, *, n_full, n_slabs):
    j, i = pl.program_id(0), pl.program_id(1)
    _slab_cast(j * pl.num_programs(1) + i, n_slabs, side_a_ref, side_a_out_ref)
    _slab_cast(j * pl.num_programs(1) + i, n_slabs, side_b_ref, side_b_out_ref)

    def body(rows, is_tail):
        hx = ht_ref[...] if is_tail else hx_ref[...]
        a = at_ref[...] if is_tail else a_ref[...]
        o = ot_ref[...].astype(BF16) if is_tail else o_ref[...]
        ga = _sigmoid(_dot(hx, wga_ref[...]))
        gb = _sigmoid(_dot(hx, wgb_ref[...]))
        mix = ga * _dot(a, wa_ref[...]) + gb * _dot(o, wb_ref[...])
        mix_ref[0:rows, :] = mix.astype(mix_ref.dtype)

    _row_split(i, n_full, hx_ref.shape[0], ht_ref.shape[0], body)


def _merge(hx, a, o, hx_tail, a_tail, o_tail, w_in, w_a, w_b, col_ga, tm, tn, side_a, side_b, n_slabs):
    d = hx.shape[1]
    m = a.shape[0] + hx_tail.shape[0]
    dc, dh = a.shape[1], o.shape[1]
    tail = a_tail.shape[0]
    n_full = a.shape[0] // tm
    n_n = d // tn
    clamp = lambda j, i: (jnp.minimum(i, n_full - 1), 0)
    assert n_slabs <= n_n * (n_full + 1)
    step_of = lambda j, i: j * (n_full + 1) + i
    a_in, a_out, a_shape = _slab_specs(side_a, n_slabs, step_of)
    b_in, b_out, b_shape = _slab_specs(side_b, n_slabs, step_of)
    return pl.pallas_call(
        functools.partial(_merge_kernel, n_full=n_full, n_slabs=n_slabs),
        grid=(n_n, n_full + 1),
        in_specs=[pl.BlockSpec((tm, d), clamp),
                  pl.BlockSpec((tm, dc), clamp),
                  pl.BlockSpec((tm, dh), clamp),
                  pl.BlockSpec((tail, d), lambda j, i: (0, 0)),
                  pl.BlockSpec((tail, dc), lambda j, i: (0, 0)),
                  pl.BlockSpec((tail, dh), lambda j, i: (0, 0)),
                  pl.BlockSpec((d, tn), lambda j, i: (0, col_ga // tn + j)),
                  pl.BlockSpec((d, tn), lambda j, i: (0, col_ga // tn + n_n + j)),
                  pl.BlockSpec((dc, tn), lambda j, i: (0, j)),
                  pl.BlockSpec((dh, tn), lambda j, i: (0, j)),
                  a_in, b_in],
        out_specs=[pl.BlockSpec((tm, tn), lambda j, i: (i, j)), a_out, b_out],
        out_shape=[jax.ShapeDtypeStruct((m, d), BF16), a_shape, b_shape],
        compiler_params=_params(("arbitrary", "arbitrary")),
        name="merge",
    )(hx, a, o, hx_tail, a_tail, o_tail, w_in, w_in, w_a, w_b, side_a, side_b)


def _outproj_kernel(xp_ref, xs_ref, mix_ref, w_ref, g_ref, x1_ref, h2_ref, *, n_full):
    i = pl.program_id(0)

    def body(rows, is_tail):
        x = xs_ref[...] if is_tail else xp_ref[...]
        x1 = x + _dot(mix_ref[0:rows, :], w_ref[...])
        x1_ref[0:rows, :] = x1
        h2_ref[0:rows, :] = _rms_scale(x1, g_ref[...]).astype(h2_ref.dtype)

    _row_split(i, n_full, xp_ref.shape[0], xs_ref.shape[0], body)


def _outproj(xp, xs, mix, w_out, g, tm):
    mp, d = xp.shape
    tail = xs.shape[0]
    n_full = mp // tm
    return pl.pallas_call(
        functools.partial(_outproj_kernel, n_full=n_full),
        grid=(n_full + 1,),
        in_specs=[pl.BlockSpec((tm, d), lambda i: (jnp.minimum(i, n_full - 1), 0)),
                  pl.BlockSpec((tail, d), lambda i: (0, 0)),
                  pl.BlockSpec((tm, d), lambda i: (i, 0)),
                  pl.BlockSpec((d, d), lambda i: (0, 0)),
                  pl.BlockSpec((1, d), lambda i: (0, 0))],
        out_specs=[pl.BlockSpec((tm, d), lambda i: (i, 0)),
                   pl.BlockSpec((tm, d), lambda i: (i, 0))],
        out_shape=[jax.ShapeDtypeStruct((mp + tail, d), F32),
                   jax.ShapeDtypeStruct((mp + tail, d), BF16)],
        compiler_params=_params(("arbitrary",)),
        name="outproj",
    )(xp, xs, mix, w_out, g.reshape(1, d))


def _mlp_kernel(h2_ref, x1_ref, wup_ref, wdn_ref, g_ref, yp_ref, ys_ref, *, n_full):
    i, j = pl.program_id(0), pl.program_id(1)
    last = pl.num_programs(1) - 1

    def body(rows, is_tail):
        y_ref = ys_ref if is_tail else yp_ref

        def step(first, final):
            halves = 2 if (final and not is_tail) else 1
            for hf in range(halves):
                rs = slice(hf * rows // halves, (hf + 1) * rows // halves)
                h = jnp.maximum(_dot(h2_ref[rs, :], wup_ref[...]), 0.0)
                acc = (x1_ref[rs, :] if first else y_ref[rs, :]) + _dot((h * h).astype(BF16), wdn_ref[...])
                y_ref[rs, :] = _rms_scale(acc, g_ref[...]) if final else acc

        @pl.when(j == 0)
        def _():
            step(True, False)

        @pl.when((j > 0) & (j < last))
        def _():
            step(False, False)

        @pl.when(j == last)
        def _():
            step(False, True)

    _row_split(i, n_full, yp_ref.shape[0], ys_ref.shape[0], body)


def _mlp(h2, x1, w_up, w_down, g, tm, tf, tail):
    m, d = h2.shape
    dff = w_up.shape[1]
    assert dff // tf >= 2
    n_full = (m - tail) // tm
    return pl.pallas_call(
        functools.partial(_mlp_kernel, n_full=n_full),
        grid=(n_full + 1, dff // tf),
        in_specs=[pl.BlockSpec((tm, d), lambda i, j: (i, 0)),
                  pl.BlockSpec((tm, d), lambda i, j: (i, 0)),
                  pl.BlockSpec((d, tf), lambda i, j: (0, j)),
                  pl.BlockSpec((tf, d), lambda i, j: (j, 0)),
                  pl.BlockSpec((1, d), lambda i, j: (0, 0))],
        out_specs=[pl.BlockSpec((tm, d), lambda i, j: (jnp.minimum(i, n_full - 1), 0)),
                   pl.BlockSpec((tail, d), lambda i, j: (0, 0))],
        out_shape=[jax.ShapeDtypeStruct((m - tail, d), F32),
                   jax.ShapeDtypeStruct((tail, d), F32)],
        compiler_params=_params(("arbitrary", "arbitrary")),
        name="mlp",
    )(h2, x1, w_up, w_down, g.reshape(1, d))


def kernel(x_prompt, x_sample, state_conv, state_hgrn, norm_mix, w_in, conv_w, lb_logits, onorm_g,
           w_branch_a, w_branch_b, w_out, norm_ffn, w_up, w_down, norm_final):
    batch, seq, d = x_prompt.shape
    n_dec = x_sample.shape[0]
    depth, _, d_conv = conv_w.shape
    heads, dk = state_hgrn.shape[2], state_hgrn.shape[3]
    d_hgrn = heads * dk
    mp = batch * seq
    assert depth == 1 and x_sample.shape[1] == 1 and dk == HEAD_DIM and state_hgrn.shape[4] == dk
    assert seq % ROW_BLOCK == 0 and mp % MLP_ROW_BLOCK == 0 and ROW_BLOCK % n_dec == 0
    col_hgrn = 0
    col_ga = col_hgrn + 4 * d_hgrn
    w_in0 = w_in.reshape(w_in.shape[1:])
    w_a0 = w_branch_a.reshape(w_branch_a.shape[1:])
    w_b0 = w_branch_b.reshape(w_branch_b.shape[1:])
    w_out0 = w_out.reshape(w_out.shape[1:])
    w_up0 = w_up.reshape(w_up.shape[1:])
    w_down0 = w_down.reshape(w_down.shape[1:])

    xp = x_prompt.reshape(mp, d)
    xs = x_sample.reshape(n_dec, d)
    hx, a_p, conv_p, w_rest_b = _conv_prompt(xp, norm_mix[0], w_in0, conv_w[0], batch, seq, d_conv,
                                             ROW_BLOCK, CONV_COL_BLOCK)
    hx_s, a_s, conv_s = _conv_sample(xs, norm_mix[0], w_in0, conv_w[0], state_conv[0], d_conv, CONV_COL_BLOCK)

    p_s = _proj(hx_s, w_rest_b, col_hgrn, 4 * d_hgrn, PROJ_COL_BLOCK, dk)
    o_p, hgrn_p, o_s, hgrn_s, w_up_b, w_a_b, w_b_b = _hgrn_prompt(
        hx, w_rest_b, col_hgrn, lb_logits, onorm_g[0], batch, seq, heads, HGRN_ROWS, HGRN_HEADS, (w_up0, w_a0, w_b0),
        p_s, state_hgrn[0])

    mix, w_out_b, w_down_b = _merge(hx, a_p, o_p, hx_s, a_s, o_s, w_rest_b, w_a_b, w_b_b,
                                    col_ga, ROW_BLOCK, MERGE_COL_BLOCK, w_out0, w_down0, SIDE_SLABS // 2)
    x1, h2 = _outproj(xp, xs, mix, w_out_b, norm_ffn[0], ROW_BLOCK)
    y_p, y_s = _mlp(h2, x1, w_up_b, w_down_b, norm_final, MLP_ROW_BLOCK, MLP_FF_BLOCK, n_dec)

    return (y_p.reshape(batch, seq, d), y_s.reshape(n_dec, 1, d),
            conv_p[None], hgrn_p[None], conv_s[None], hgrn_s[None])
```

```python
import functools

import jax
import jax.numpy as jnp
from jax import lax
from jax.experimental import pallas as pl
from jax.experimental.pallas import tpu as pltpu

EPS = 1e-6
CHUNK = 64
HEAD_DIM = 128
V7X_VMEM_BYTES = 64 * 1024 * 1024
VMEM_LIMIT = V7X_VMEM_BYTES * 7 // 8
VMEM_LIMIT_MLP = V7X_VMEM_BYTES * 31 // 32

ROW_BLOCK = 512
MLP_ROW_BLOCK = 1024
MLP_FF_BLOCK = 1024
CONV_COL_BLOCK = 512
CONV_SUB_BLOCK = 256
PROJ_COL_BLOCK = 1024
MERGE_COL_BLOCK = 1024
SIDE_SLABS = 64
HGRN_ROWS = 512
HGRN_SUB_ROWS = 256
HGRN_HEADS = 4

BF16 = jnp.bfloat16
F32 = jnp.float32


def _params(semantics, vmem_limit=VMEM_LIMIT):
    return pltpu.CompilerParams(dimension_semantics=semantics, vmem_limit_bytes=vmem_limit)


def _dot(a, b):
    return jnp.dot(a, b, preferred_element_type=F32)


def _dot_nt(a, b):
    return lax.dot_general(a, b, (((1,), (1,)), ((), ())), preferred_element_type=F32)


def _dot_tn(a, b):
    return lax.dot_general(a, b, (((0,), (0,)), ((), ())), preferred_element_type=F32)


def _sigmoid(x):
    return jax.nn.sigmoid(x)


def _silu(x):
    return x * jax.nn.sigmoid(x)


def _rms_scale(x, g):
    ms = jnp.mean(x * x, axis=-1, keepdims=True)
    return x * lax.rsqrt(ms + EPS) * g


def _row_split(i, n_full, full_rows, tail_rows, body):
    @pl.when(i < n_full)
    def _():
        body(full_rows, False)

    @pl.when(i == n_full)
    def _():
        body(tail_rows, True)


def _slab_cast(step, n_slabs, src_ref, dst_ref):
    @pl.when(step < n_slabs)
    def _():
        dst_ref[...] = src_ref[...].astype(dst_ref.dtype)


def _slab_specs(w, n_slabs, step_of):
    r, c = w.shape
    assert r % n_slabs == 0

    def index(*g):
        return (jnp.minimum(step_of(*g), n_slabs - 1), 0)

    spec = pl.BlockSpec((r // n_slabs, c), index)
    return spec, spec, jax.ShapeDtypeStruct((r, c), BF16)


def _conv_prompt_kernel(x_ref, g_ref, whc_ref, wbg_ref, wcg_ref, cw_ref, side_ref,
                        hx_out_ref, a_ref, nc_ref, side_out_ref,
                        h_ref, wb_ref, carry_ref, *, n_slabs, n_t, n_c):
    s = pl.program_id(0)
    tm, tc = a_ref.shape
    j = jnp.maximum(s - 1, 0)
    t = lax.rem(j // n_c, n_t)
    _slab_cast(s, n_slabs, side_ref, side_out_ref)

    def normalise(slot):
        hx = _rms_scale(x_ref[...], g_ref[...]).astype(BF16)
        h_ref[slot] = hx
        hx_out_ref[...] = hx
        return hx

    @pl.when(s == 0)
    def _():
        normalise(0)

    @pl.when((s >= 1) & (s <= n_c))
    def _():
        for c in range(n_c):
            @pl.when(s == c + 1)
            def _(c=c):
                wb_ref[c, 0] = whc_ref[...].astype(BF16)
                wb_ref[c, 1] = wbg_ref[...].astype(BF16)
                wb_ref[c, 2] = wcg_ref[...].astype(BF16)

    cw = cw_ref[...]
    row = lax.broadcasted_iota(jnp.int32, (tm, CONV_SUB_BLOCK), 0)

    def body(c, slot):
        @pl.when((t == 0) & (c == 0))
        def _():
            carry_ref[...] = jnp.zeros_like(carry_ref)

        last_channel = c == n_c - 1
        hx_next = normalise(1 - slot) if last_channel else None
        hx = h_ref[slot]
        n_sub = tc // CONV_SUB_BLOCK
        for sub in range(n_sub):
            cs = slice(sub * CONV_SUB_BLOCK, (sub + 1) * CONV_SUB_BLOCK)
            lhs = jnp.where(s < 0, hx_next, hx) if (last_channel and sub == n_sub - 1) else hx
            hc = _dot(lhs, wb_ref[c, 0, :, cs])
            bg = _dot(lhs, wb_ref[c, 1, :, cs])
            cg = _dot(lhs, wb_ref[c, 2, :, cs])
            u = cg * hc
            c0 = carry_ref[c, 0:1, cs]
            c1 = carry_ref[c, 1:2, cs]
            u1 = jnp.where(row == 0, c1, pltpu.roll(u, 1, 0))
            u2 = jnp.where(row == 0, c0, jnp.where(row == 1, c1, pltpu.roll(u, 2, 0)))
            conv = cw[0:1, cs] * u2 + cw[1:2, cs] * u1 + cw[2:3, cs] * u
            a_ref[:, cs] = (bg * conv).astype(a_ref.dtype)
            carry_ref[c, 0:2, cs] = u[tm - 2:tm, :]

        @pl.when(t == n_t - 1)
        def _():
            nc_ref[0, :, c * tc:(c + 1) * tc] = carry_ref[c, 0:2, :]

    phase = lax.rem(j, 2 * n_c)
    for m in range(2 * n_c):
        @pl.when((s >= 1) & (phase == m))
        def _(m=m):
            body(m % n_c, m // n_c)


def _conv_prompt(x, g, w_in, conv_w, batch, seq, d_conv, tm, tc):
    mp, d = x.shape
    n_c = d_conv // tc
    n_t = seq // tm
    rows_blocks = batch * n_t
    n_steps = n_c * rows_blocks
    conv = lambda s: jnp.maximum(s - 1, 0)
    norm = lambda s: jnp.minimum(s // n_c, rows_blocks - 1)
    col_rest = 3 * d_conv
    sw = tc * n_c
    n_col = (w_in.shape[1] - col_rest) // sw
    assert n_steps == 4 * n_col and col_rest % sw == 0
    tile = lambda s: jnp.minimum(s, n_steps - 1)
    side_in = pl.BlockSpec((d // 4, sw), lambda s: (tile(s) % 4, col_rest // sw + tile(s) // 4))
    side_out = pl.BlockSpec((d // 4, sw), lambda s: (tile(s) % 4, tile(s) // 4))
    side_shape = jax.ShapeDtypeStruct((d, w_in.shape[1] - col_rest), BF16)

    def wcol(k):
        return pl.BlockSpec((d, tc), lambda s: (0, k * n_c + jnp.minimum(conv(s), n_c - 1)),
                            pipeline_mode=pl.Buffered(1))

    return pl.pallas_call(
        functools.partial(_conv_prompt_kernel, n_slabs=n_steps, n_t=n_t, n_c=n_c),
        grid=(n_steps + 1,),
        in_specs=[pl.BlockSpec((tm, d), lambda s: (norm(s), 0)),
                  pl.BlockSpec((1, d), lambda s: (0, 0)),
                  wcol(0), wcol(1), wcol(2),
                  pl.BlockSpec((3, tc), lambda s: (0, conv(s) % n_c)),
                  side_in],
        out_specs=[pl.BlockSpec((tm, d), lambda s: (norm(s), 0)),
                   pl.BlockSpec((tm, tc), lambda s: (conv(s) // n_c, conv(s) % n_c)),
                   pl.BlockSpec((1, 2, d_conv), lambda s: (conv(s) // n_c // n_t, 0, 0)),
                   side_out],
        out_shape=[jax.ShapeDtypeStruct((mp, d), BF16),
                   jax.ShapeDtypeStruct((mp, d_conv), BF16),
                   jax.ShapeDtypeStruct((batch, 2, d_conv), F32),
                   side_shape],
        scratch_shapes=[pltpu.VMEM((2, tm, d), BF16), pltpu.VMEM((n_c, 3, d, tc), BF16),
                        pltpu.VMEM((n_c, 8, tc), F32)],
        compiler_params=_params(("arbitrary",)),
        name="conv_prompt",
    )(x, g.reshape(1, d), w_in, w_in, w_in, conv_w, w_in)


def _conv_sample_kernel(x_ref, g_ref, whc_ref, wbg_ref, wcg_ref, cw_ref, st_ref, hx_ref, a_ref, nc_ref):
    hx = _rms_scale(x_ref[...], g_ref[...]).astype(BF16)
    hx_ref[...] = hx
    hc = _dot(hx, whc_ref[...].astype(BF16))
    bg = _dot(hx, wbg_ref[...].astype(BF16))
    cg = _dot(hx, wcg_ref[...].astype(BF16))
    u = cg * hc
    cw = cw_ref[...]
    s1 = st_ref[:, 1, :]
    conv = cw[0:1, :] * st_ref[:, 0, :] + cw[1:2, :] * s1 + cw[2:3, :] * u
    a_ref[...] = (bg * conv).astype(a_ref.dtype)
    nc_ref[:, 0, :] = s1
    nc_ref[:, 1, :] = u


def _conv_sample(x, g, w_in, conv_w, state, d_conv, tc):
    n, d = x.shape
    n_c = d_conv // tc
    return pl.pallas_call(
        _conv_sample_kernel,
        grid=(n_c,),
        in_specs=[pl.BlockSpec((n, d), lambda c: (0, 0)),
                  pl.BlockSpec((1, d), lambda c: (0, 0)),
                  pl.BlockSpec((d, tc), lambda c: (0, c)),
                  pl.BlockSpec((d, tc), lambda c: (0, n_c + c)),
                  pl.BlockSpec((d, tc), lambda c: (0, 2 * n_c + c)),
                  pl.BlockSpec((3, tc), lambda c: (0, c)),
                  pl.BlockSpec((n, 2, tc), lambda c: (0, 0, c))],
        out_specs=[pl.BlockSpec((n, d), lambda c: (0, 0)),
                   pl.BlockSpec((n, tc), lambda c: (0, c)),
                   pl.BlockSpec((n, 2, tc), lambda c: (0, 0, c))],
        out_shape=[jax.ShapeDtypeStruct((n, d), BF16),
                   jax.ShapeDtypeStruct((n, d_conv), BF16),
                   jax.ShapeDtypeStruct((n, 2, d_conv), F32)],
        compiler_params=_params(("arbitrary",)),
        name="conv_sample",
    )(x, g.reshape(1, d), w_in, w_in, w_in, conv_w, state)


def _proj_kernel(x_ref, w_ref, o_ref):
    res = _dot(x_ref[...], w_ref[...])
    dk = o_ref.shape[2]
    for h in range(o_ref.shape[1]):
        o_ref[:, h, :] = res[:, h * dk:(h + 1) * dk]


def _proj(x, w, col0, n, tn, dk):
    rows, d = x.shape
    return pl.pallas_call(
        _proj_kernel,
        grid=(n // tn,),
        in_specs=[pl.BlockSpec((rows, d), lambda j: (0, 0)),
                  pl.BlockSpec((d, tn), lambda j: (0, col0 // tn + j))],
        out_specs=pl.BlockSpec((rows, tn // dk, dk), lambda j: (0, j, 0)),
        out_shape=jax.ShapeDtypeStruct((rows, n // dk, dk), F32),
        compiler_params=_params(("arbitrary",)),
        name="hgrn_proj_decode",
    )(x, w)


def _lower_bound(lbl):
    e = jnp.exp(lbl - jnp.max(lbl, axis=0))
    return e[0] / jnp.sum(e, axis=0)


def _head_norm_gate(o, g, og):
    return _rms_scale(o, g) * _silu(og)


def _cumsum_rows(x, tril_bf16):
    hi = x.astype(BF16)
    r1 = x - hi.astype(F32)
    mid = r1.astype(BF16)
    lo = (r1 - mid.astype(F32)).astype(BF16)
    n = x.shape[1]
    parts = _dot(tril_bf16, jnp.concatenate([hi, mid, lo], axis=1))
    return parts[:, 0:n] + parts[:, n:2 * n] + parts[:, 2 * n:3 * n]


def _hgrn_decode_step(p_ref, lb, g, s0_ref, o_ref, row0, sn_ref):
    bb, heads, dk = s0_ref.shape[0], s0_ref.shape[1], s0_ref.shape[2]
    head_row = lax.broadcasted_iota(jnp.int32, (heads, dk), 0)

    def column(row):
        return jnp.broadcast_to(row, (dk, dk)).T

    for i in range(bb):
        p = p_ref[i]
        qs = _silu(p[0:heads])
        f = lb + (1.0 - lb) * _sigmoid(p[heads:2 * heads])
        kk = 1.0 - f
        v = p[2 * heads:3 * heads]
        og = p[3 * heads:4 * heads]
        q_in = (qs * f).astype(BF16)
        o_inter = jnp.zeros((heads, dk), F32)
        for h in range(heads):
            s = s0_ref[i, h]
            sn_ref[i, h] = column(f[h:h + 1, :]) * s + column(kk[h:h + 1, :]) * v[h:h + 1, :]
            o_inter = o_inter + jnp.where(head_row == h, _dot(q_in, s.astype(BF16)), 0.0)
        o = jnp.sum(qs * kk, axis=-1, keepdims=True) * v + o_inter
        o = _head_norm_gate(o, g, og)
        o_ref[pl.ds(row0 + i, 1), :] = jnp.concatenate([o[h:h + 1, :] for h in range(heads)], axis=1)


def _hgrn_prompt_kernel(hx_ref, wq_ref, wf_ref, wi_ref, wo_ref, lbl_ref, g_ref, dp_ref, dlbl_ref, ds0_ref, *refs,
                        n_side, n_slabs, n_t):
    side_refs, refs = refs[:n_side], refs[n_side:]
    o_ref, s_ref, do_ref, dsn_ref = refs[:4]
    side_out_refs, (pa_ref, pb_ref, st_ref) = refs[4:4 + n_side], refs[4 + n_side:]
    s = pl.program_id(0)
    for side_ref, side_out_ref in zip(side_refs, side_out_refs):
        _slab_cast(s, n_slabs, side_ref, side_out_ref)
    tb = hx_ref.shape[0]
    heads_per_step = st_ref.shape[0]
    dk = HEAD_DIM
    sub = HGRN_SUB_ROWS
    n_sub = tb // sub
    n_chunks = sub // CHUNK
    t = lax.rem(jnp.maximum(s - 1, 0), n_t)

    @pl.when(s == 0)
    def _():
        pb_ref[...] = jnp.zeros_like(pb_ref)

    @pl.when(t == 0)
    def _():
        st_ref[...] = jnp.zeros_like(st_ref)

    lb_all = _lower_bound(lbl_ref[...])
    g = g_ref[...]
    row = lax.broadcasted_iota(jnp.int32, (sub, sub), 0)
    col = lax.broadcasted_iota(jnp.int32, (sub, sub), 1)
    shift = CHUNK.bit_length() - 1
    causal = (row >= col) & (jnp.right_shift(row, shift) == jnp.right_shift(col, shift))
    tril = causal.astype(BF16)

    def sub_block(p_next_ref, p_ref, sb):
        heads = range(heads_per_step)
        cols = [slice(hh * dk, (hh + 1) * dk) for hh in heads]
        rows = slice(sb * sub, (sb + 1) * sub)
        seqs = ds0_ref.shape[0] // n_sub
        own = pl.ds(sb * seqs, seqs)

        def project(k, w_ref):
            p_next_ref[k, rows] = _dot(hx_ref[rows, :], w_ref[...])

        qs, kk, lf = [], [], []
        for hh in heads:
            lb = lb_all[:, cols[hh]]
            f = lb + (1.0 - lb) * _sigmoid(p_ref[1, rows, cols[hh]])
            qs.append(_silu(p_ref[0, rows, cols[hh]]))
            kk.append(1.0 - f)
            lf.append(jnp.log(f))
        project(0, wq_ref)
        b = [_cumsum_rows(lf[hh], tril) for hh in heads]
        q_in, k_in, k_end, decay = [], [], [], []
        for hh in heads:
            b3 = b[hh].reshape(n_chunks, CHUNK, dk)
            b_last = b3[:, CHUNK - 1:CHUNK, :]
            b_end = jnp.broadcast_to(b_last, b3.shape).reshape(sub, dk)
            q_in.append((qs[hh] * jnp.exp(b[hh])).astype(BF16))
            k_in.append((kk[hh] * jnp.exp(-b[hh])).astype(BF16))
            k_end.append((kk[hh] * jnp.exp(b_end - b[hh])).astype(BF16))
            decay.append(jnp.exp(b_last))
        project(1, wf_ref)
        v = [p_ref[2, rows, cols[hh]].astype(BF16) for hh in heads]
        raw = [_dot_nt(q_in[hh], k_in[hh]) for hh in heads]
        delta = [[_dot_tn(v[hh][c * CHUNK:(c + 1) * CHUNK], k_end[hh][c * CHUNK:(c + 1) * CHUNK])
                  for c in range(n_chunks)] for hh in heads]
        scores = [jnp.where(causal, raw[hh], 0.0).astype(BF16) for hh in heads]
        starts = []
        for hh in heads:
            st = st_ref[hh]
            per_chunk = []
            for c in range(n_chunks):
                per_chunk.append(st.astype(BF16))
                st = decay[hh][c] * st + delta[hh][c]
            st_ref[hh] = st
            starts.append(per_chunk)
        project(2, wi_ref)
        row0 = jnp.minimum(s, n_slabs - 1) * ds0_ref.shape[0] + sb * seqs
        _hgrn_decode_step(dp_ref.at[own], _lower_bound(dlbl_ref[...]), g, ds0_ref.at[own], do_ref, row0,
                          dsn_ref.at[own])
        o = []
        for hh in heads:
            inter = [_dot_nt(q_in[hh][c * CHUNK:(c + 1) * CHUNK], starts[hh][c]) for c in range(n_chunks)]
            o.append(_dot(scores[hh], v[hh]) + jnp.concatenate(inter, axis=0))
        for hh in heads:
            o_ref[rows, cols[hh]] = _head_norm_gate(o[hh], g, p_ref[3, rows, cols[hh]]).astype(o_ref.dtype)
        project(3, wo_ref)

    def body(p_next_ref, p_ref):
        for sb in range(n_sub):
            sub_block(p_next_ref, p_ref, sb)

    parity = lax.rem(s, 2)

    @pl.when(parity == 0)
    def _():
        body(pa_ref, pb_ref)

    @pl.when(parity == 1)
    def _():
        body(pb_ref, pa_ref)

    @pl.when(t == n_t - 1)
    def _():
        for hh in range(heads_per_step):
            s_ref[0, hh] = st_ref[hh].T


def _hgrn_prompt(hx, w_in, col0, lb_logits, onorm_g, batch, seq, heads, tb, hp, side_ws, dec_p, dec_state):
    d = hx.shape[1]
    n_t = seq // tb
    n_h = heads // hp
    dk = HEAD_DIM
    wc = hp * dk
    assert CHUNK & (CHUNK - 1) == 0 and col0 % wc == 0
    rows_blocks = batch * n_t
    n_steps = n_h * rows_blocks
    proj = lambda s: jnp.minimum(s, n_steps - 1)
    rec = lambda s: jnp.maximum(s - 1, 0)
    sides = [_slab_specs(w, n_steps, lambda s: s) for w in side_ws]

    n_dec, dec_rows = dec_p.shape[0], dec_p.shape[1]
    assert n_dec % n_steps == 0
    db = n_dec // n_steps
    dec = lambda s: jnp.minimum(s, n_steps - 1)

    def wcol(k):
        return pl.BlockSpec((d, wc), lambda s: (0, col0 // wc + k * n_h + proj(s) // rows_blocks))

    return pl.pallas_call(
        functools.partial(_hgrn_prompt_kernel, n_side=len(sides), n_slabs=n_steps, n_t=n_t),
        grid=(n_steps + 1,),
        in_specs=[pl.BlockSpec((tb, d), lambda s: (proj(s) % rows_blocks, 0)),
                  wcol(0), wcol(1), wcol(2), wcol(3),
                  pl.BlockSpec((2, 1, wc), lambda s: (0, 0, rec(s) // rows_blocks)),
                  pl.BlockSpec((1, dk), lambda s: (0, 0)),
                  pl.BlockSpec((db, dec_rows, dk), lambda s: (dec(s), 0, 0)),
                  pl.BlockSpec((2, heads, dk), lambda s: (0, 0, 0)),
                  pl.BlockSpec((db, heads, dk, dk), lambda s: (dec(s), 0, 0, 0))]
        + [side[0] for side in sides],
        out_specs=[pl.BlockSpec((tb, wc), lambda s: (rec(s) % rows_blocks, rec(s) // rows_blocks)),
                   pl.BlockSpec((1, hp, dk, dk),
                                lambda s: ((rec(s) % rows_blocks) // n_t, rec(s) // rows_blocks, 0, 0)),
                   pl.BlockSpec((n_dec, heads * dk), lambda s: (0, 0)),
                   pl.BlockSpec((db, heads, dk, dk), lambda s: (dec(s), 0, 0, 0))]
        + [side[1] for side in sides],
        out_shape=[jax.ShapeDtypeStruct((batch * seq, heads * dk), BF16),
                   jax.ShapeDtypeStruct((batch, heads, dk, dk), F32),
                   jax.ShapeDtypeStruct((n_dec, heads * dk), F32),
                   jax.ShapeDtypeStruct((n_dec, heads, dk, dk), F32)]
        + [side[2] for side in sides],
        scratch_shapes=[pltpu.VMEM((4, tb, wc), F32), pltpu.VMEM((4, tb, wc), F32),
                        pltpu.VMEM((hp, dk, dk), F32)],
        compiler_params=_params(("arbitrary",)),
        name="hgrn_prompt",
    )(hx, w_in, w_in, w_in, w_in, lb_logits.reshape(2, 1, heads * dk), onorm_g.reshape(1, dk),
      dec_p, lb_logits.reshape(2, heads, dk), dec_state, *side_ws)


def _merge_kernel(hx_ref, a_ref, o_ref, ht_ref, at_ref, ot_ref, wga_ref, wgb_ref, wa_ref, wb_ref, side_a_ref,
                  side_b_ref, mix_ref, side_a_out_ref, side_b_out_ref, *, n_full, n_slabs):
    j, i = pl.program_id(0), pl.program_id(1)
    _slab_cast(j * pl.num_programs(1) + i, n_slabs, side_a_ref, side_a_out_ref)
    _slab_cast(j * pl.num_programs(1) + i, n_slabs, side_b_ref, side_b_out_ref)

    def body(rows, is_tail):
        hx = ht_ref[...] if is_tail else hx_ref[...]
        a = at_ref[...] if is_tail else a_ref[...]
        o = ot_ref[...].astype(BF16) if is_tail else o_ref[...]
        ga = _sigmoid(_dot(hx, wga_ref[...]))
        gb = _sigmoid(_dot(hx, wgb_ref[...]))
        mix = ga * _dot(a, wa_ref[...]) + gb * _dot(o, wb_ref[...])
        mix_ref[0:rows, :] = mix.astype(mix_ref.dtype)

    _row_split(i, n_full, hx_ref.shape[0], ht_ref.shape[0], body)


def _merge(hx, a, o, hx_tail, a_tail, o_tail, w_in, w_a, w_b, col_ga, tm, tn, side_a, side_b, n_slabs):
    d = hx.shape[1]
    m = a.shape[0] + hx_tail.shape[0]
    dc, dh = a.shape[1], o.shape[1]
    tail = a_tail.shape[0]
    n_full = a.shape[0] // tm
    n_n = d // tn
    clamp = lambda j, i: (jnp.minimum(i, n_full - 1), 0)
    assert n_slabs <= n_n * (n_full + 1)
    step_of = lambda j, i: j * (n_full + 1) + i
    a_in, a_out, a_shape = _slab_specs(side_a, n_slabs, step_of)
    b_in, b_out, b_shape = _slab_specs(side_b, n_slabs, step_of)
    return pl.pallas_call(
        functools.partial(_merge_kernel, n_full=n_full, n_slabs=n_slabs),
        grid=(n_n, n_full + 1),
        in_specs=[pl.BlockSpec((tm, d), clamp),
                  pl.BlockSpec((tm, dc), clamp),
                  pl.BlockSpec((tm, dh), clamp),
                  pl.BlockSpec((tail, d), lambda j, i: (0, 0)),
                  pl.BlockSpec((tail, dc), lambda j, i: (0, 0)),
                  pl.BlockSpec((tail, dh), lambda j, i: (0, 0)),
                  pl.BlockSpec((d, tn), lambda j, i: (0, col_ga // tn + j)),
                  pl.BlockSpec((d, tn), lambda j, i: (0, col_ga // tn + n_n + j)),
                  pl.BlockSpec((dc, tn), lambda j, i: (0, j)),
                  pl.BlockSpec((dh, tn), lambda j, i: (0, j)),
                  a_in, b_in],
        out_specs=[pl.BlockSpec((tm, tn), lambda j, i: (i, j)), a_out, b_out],
        out_shape=[jax.ShapeDtypeStruct((m, d), BF16), a_shape, b_shape],
        compiler_params=_params(("arbitrary", "arbitrary")),
        name="merge",
    )(hx, a, o, hx_tail, a_tail, o_tail, w_in, w_in, w_a, w_b, side_a, side_b)


def _outproj_kernel(xp_ref, xs_ref, mix_ref, w_ref, g_ref, x1_ref, h2_ref, *, n_full):
    i = pl.program_id(0)

    def body(rows, is_tail):
        x = xs_ref[...] if is_tail else xp_ref[...]
        x1 = x + _dot(mix_ref[0:rows, :], w_ref[...])
        x1_ref[0:rows, :] = x1
        h2_ref[0:rows, :] = _rms_scale(x1, g_ref[...]).astype(h2_ref.dtype)

    _row_split(i, n_full, xp_ref.shape[0], xs_ref.shape[0], body)


def _outproj(xp, xs, mix, w_out, g, tm):
    mp, d = xp.shape
    tail = xs.shape[0]
    n_full = mp // tm
    return pl.pallas_call(
        functools.partial(_outproj_kernel, n_full=n_full),
        grid=(n_full + 1,),
        in_specs=[pl.BlockSpec((tm, d), lambda i: (jnp.minimum(i, n_full - 1), 0)),
                  pl.BlockSpec((tail, d), lambda i: (0, 0)),
                  pl.BlockSpec((tm, d), lambda i: (i, 0)),
                  pl.BlockSpec((d, d), lambda i: (0, 0)),
                  pl.BlockSpec((1, d), lambda i: (0, 0))],
        out_specs=[pl.BlockSpec((tm, d), lambda i: (i, 0)),
                   pl.BlockSpec((tm, d), lambda i: (i, 0))],
        out_shape=[jax.ShapeDtypeStruct((mp + tail, d), F32),
                   jax.ShapeDtypeStruct((mp + tail, d), BF16)],
        compiler_params=_params(("arbitrary",)),
        name="outproj",
    )(xp, xs, mix, w_out, g.reshape(1, d))


def _mlp_kernel(h2_ref, x1_ref, wup_ref, wdn_ref, g_ref, yp_ref, ys_ref, *, n_full):
    i, j = pl.program_id(0), pl.program_id(1)
    last = pl.num_programs(1) - 1

    def body(rows, is_tail):
        y_ref = ys_ref if is_tail else yp_ref

        def step(first, final):
            halves = 2 if (final and not is_tail) else 1
            for hf in range(halves):
                rs = slice(hf * rows // halves, (hf + 1) * rows // halves)
                h = jnp.maximum(_dot(h2_ref[rs, :], wup_ref[...]), 0.0)
                acc = (x1_ref[rs, :] if first else y_ref[rs, :]) + _dot((h * h).astype(BF16), wdn_ref[...])
                y_ref[rs, :] = _rms_scale(acc, g_ref[...]) if final else acc

        @pl.when(j == 0)
        def _():
            step(True, False)

        @pl.when((j > 0) & (j < last))
        def _():
            step(False, False)

        @pl.when(j == last)
        def _():
            step(False, True)

    _row_split(i, n_full, yp_ref.shape[0], ys_ref.shape[0], body)


def _mlp(h2, x1, w_up, w_down, g, tm, tf, tail):
    m, d = h2.shape
    dff = w_up.shape[1]
    assert dff // tf >= 2
    n_full = (m - tail) // tm
    return pl.pallas_call(
        functools.partial(_mlp_kernel, n_full=n_full),
        grid=(n_full + 1, dff // tf),
        in_specs=[pl.BlockSpec((tm, d), lambda i, j: (i, 0)),
                  pl.BlockSpec((tm, d), lambda i, j: (i, 0)),
                  pl.BlockSpec((d, tf), lambda i, j: (0, j)),
                  pl.BlockSpec((tf, d), lambda i, j: (j, 0)),
                  pl.BlockSpec((1, d), lambda i, j: (0, 0))],
        out_specs=[pl.BlockSpec((tm, d), lambda i, j: (jnp.minimum(i, n_full - 1), 0)),
                   pl.BlockSpec((tail, d), lambda i, j: (0, 0))],
        out_shape=[jax.ShapeDtypeStruct((m - tail, d), F32),
                   jax.ShapeDtypeStruct((tail, d), F32)],
        compiler_params=_params(("arbitrary", "arbitrary"), VMEM_LIMIT_MLP),
        name="mlp",
    )(h2, x1, w_up, w_down, g.reshape(1, d))


def kernel(x_prompt, x_sample, state_conv, state_hgrn, norm_mix, w_in, conv_w, lb_logits, onorm_g,
           w_branch_a, w_branch_b, w_out, norm_ffn, w_up, w_down, norm_final):
    batch, seq, d = x_prompt.shape
    n_dec = x_sample.shape[0]
    depth, _, d_conv = conv_w.shape
    heads, dk = state_hgrn.shape[2], state_hgrn.shape[3]
    d_hgrn = heads * dk
    mp = batch * seq
    assert depth == 1 and x_sample.shape[1] == 1 and dk == HEAD_DIM and state_hgrn.shape[4] == dk
    assert seq % ROW_BLOCK == 0 and mp % MLP_ROW_BLOCK == 0 and ROW_BLOCK % n_dec == 0
    col_hgrn = 0
    col_ga = col_hgrn + 4 * d_hgrn
    w_in0 = w_in.reshape(w_in.shape[1:])
    w_a0 = w_branch_a.reshape(w_branch_a.shape[1:])
    w_b0 = w_branch_b.reshape(w_branch_b.shape[1:])
    w_out0 = w_out.reshape(w_out.shape[1:])
    w_up0 = w_up.reshape(w_up.shape[1:])
    w_down0 = w_down.reshape(w_down.shape[1:])

    xp = x_prompt.reshape(mp, d)
    xs = x_sample.reshape(n_dec, d)
    hx, a_p, conv_p, w_rest_b = _conv_prompt(xp, norm_mix[0], w_in0, conv_w[0], batch, seq, d_conv,
                                             ROW_BLOCK, CONV_COL_BLOCK)
    hx_s, a_s, conv_s = _conv_sample(xs, norm_mix[0], w_in0, conv_w[0], state_conv[0], d_conv, CONV_COL_BLOCK)

    p_s = _proj(hx_s, w_rest_b, col_hgrn, 4 * d_hgrn, PROJ_COL_BLOCK, dk)
    o_p, hgrn_p, o_s, hgrn_s, w_up_b, w_a_b, w_b_b = _hgrn_prompt(
        hx, w_rest_b, col_hgrn, lb_logits, onorm_g[0], batch, seq, heads, HGRN_ROWS, HGRN_HEADS, (w_up0, w_a0, w_b0),
        p_s, state_hgrn[0])

    mix, w_out_b, w_down_b = _merge(hx, a_p, o_p, hx_s, a_s, o_s, w_rest_b, w_a_b, w_b_b,
                                    col_ga, ROW_BLOCK, MERGE_COL_BLOCK, w_out0, w_down0, SIDE_SLABS // 2)
    x1, h2 = _outproj(xp, xs, mix, w_out_b, norm_ffn[0], ROW_BLOCK)
    y_p, y_s = _mlp(h2, x1, w_up_b, w_down_b, norm_final, MLP_ROW_BLOCK, MLP_FF_BLOCK, n_dec)

    return (y_p.reshape(batch, seq, d), y_s.reshape(n_dec, 1, d),
            conv_p[None], hgrn_p[None], conv_s[None], hgrn_s[None])
```

```python
import functools

import jax
import jax.numpy as jnp
from jax import lax
from jax.experimental import pallas as pl
from jax.experimental.pallas import tpu as pltpu

EPS = 1e-6
CHUNK = 64
HEAD_DIM = 128
V7X_VMEM_BYTES = 64 * 1024 * 1024
VMEM_LIMIT = V7X_VMEM_BYTES * 7 // 8
VMEM_LIMIT_LARGE = V7X_VMEM_BYTES * 31 // 32

ROW_BLOCK = 512
MLP_ROW_BLOCK = 1024
MLP_FF_BLOCK = 1024
CONV_COL_BLOCK = 512
CONV_SUB_BLOCK = 256
PROJ_COL_BLOCK = 1024
MERGE_COL_BLOCK = 1024
OUTPROJ_ROW_BLOCK = 1024
SIDE_SLABS = 64
HGRN_ROWS = 512
HGRN_SUB_ROWS = 256
HGRN_HEADS = 4

BF16 = jnp.bfloat16
F32 = jnp.float32


def _params(semantics, vmem_limit=VMEM_LIMIT):
    return pltpu.CompilerParams(dimension_semantics=semantics, vmem_limit_bytes=vmem_limit)


def _dot(a, b):
    return jnp.dot(a, b, preferred_element_type=F32)


def _dot_nt(a, b):
    return lax.dot_general(a, b, (((1,), (1,)), ((), ())), preferred_element_type=F32)


def _dot_tn(a, b):
    return lax.dot_general(a, b, (((0,), (0,)), ((), ())), preferred_element_type=F32)


def _sigmoid(x):
    return jax.nn.sigmoid(x)


def _silu(x):
    return x * jax.nn.sigmoid(x)


def _rms_scale(x, g):
    ms = jnp.mean(x * x, axis=-1, keepdims=True)
    return x * lax.rsqrt(ms + EPS) * g


def _row_split(i, n_full, full_rows, tail_rows, body):
    @pl.when(i < n_full)
    def _():
        body(full_rows, False)

    @pl.when(i == n_full)
    def _():
        body(tail_rows, True)


def _slab_cast(step, n_slabs, src_ref, dst_ref):
    @pl.when(step < n_slabs)
    def _():
        dst_ref[...] = src_ref[...].astype(dst_ref.dtype)


def _slab_specs(w, n_slabs, step_of):
    r, c = w.shape
    assert r % n_slabs == 0

    def index(*g):
        return (jnp.minimum(step_of(*g), n_slabs - 1), 0)

    spec = pl.BlockSpec((r // n_slabs, c), index)
    return spec, spec, jax.ShapeDtypeStruct((r, c), BF16)


def _conv_prompt_kernel(x_ref, g_ref, whc_ref, wbg_ref, wcg_ref, cw_ref, side_ref,
                        hx_out_ref, a_ref, nc_ref, side_out_ref,
                        h_ref, wb_ref, carry_ref, *, n_slabs, n_t, n_c):
    s = pl.program_id(0)
    tm, tc = a_ref.shape
    j = jnp.maximum(s - 1, 0)
    t = lax.rem(j // n_c, n_t)
    _slab_cast(s, n_slabs, side_ref, side_out_ref)

    def normalise(slot):
        hx = _rms_scale(x_ref[...], g_ref[...]).astype(BF16)
        h_ref[slot] = hx
        hx_out_ref[...] = hx
        return hx

    @pl.when(s == 0)
    def _():
        normalise(0)

    @pl.when((s >= 1) & (s <= n_c))
    def _():
        for c in range(n_c):
            @pl.when(s == c + 1)
            def _(c=c):
                wb_ref[c, 0] = whc_ref[...].astype(BF16)
                wb_ref[c, 1] = wbg_ref[...].astype(BF16)
                wb_ref[c, 2] = wcg_ref[...].astype(BF16)

    cw = cw_ref[...]
    row = lax.broadcasted_iota(jnp.int32, (tm, CONV_SUB_BLOCK), 0)

    def body(c, slot):
        @pl.when((t == 0) & (c == 0))
        def _():
            carry_ref[...] = jnp.zeros_like(carry_ref)

        last_channel = c == n_c - 1
        hx_next = normalise(1 - slot) if last_channel else None
        hx = h_ref[slot]
        n_sub = tc // CONV_SUB_BLOCK
        for sub in range(n_sub):
            cs = slice(sub * CONV_SUB_BLOCK, (sub + 1) * CONV_SUB_BLOCK)
            lhs = jnp.where(s < 0, hx_next, hx) if (last_channel and sub == n_sub - 1) else hx
            hc = _dot(lhs, wb_ref[c, 0, :, cs])
            bg = _dot(lhs, wb_ref[c, 1, :, cs])
            cg = _dot(lhs, wb_ref[c, 2, :, cs])
            u = cg * hc
            c0 = carry_ref[c, 0:1, cs]
            c1 = carry_ref[c, 1:2, cs]
            u1 = jnp.where(row == 0, c1, pltpu.roll(u, 1, 0))
            u2 = jnp.where(row == 0, c0, jnp.where(row == 1, c1, pltpu.roll(u, 2, 0)))
            conv = cw[0:1, cs] * u2 + cw[1:2, cs] * u1 + cw[2:3, cs] * u
            a_ref[:, cs] = (bg * conv).astype(a_ref.dtype)
            carry_ref[c, 0:2, cs] = u[tm - 2:tm, :]

        @pl.when(t == n_t - 1)
        def _():
            nc_ref[0, :, c * tc:(c + 1) * tc] = carry_ref[c, 0:2, :]

    phase = lax.rem(j, 2 * n_c)
    for m in range(2 * n_c):
        @pl.when((s >= 1) & (phase == m))
        def _(m=m):
            body(m % n_c, m // n_c)


def _conv_prompt(x, g, w_in, conv_w, batch, seq, d_conv, tm, tc):
    mp, d = x.shape
    n_c = d_conv // tc
    n_t = seq // tm
    rows_blocks = batch * n_t
    n_steps = n_c * rows_blocks
    conv = lambda s: jnp.maximum(s - 1, 0)
    norm = lambda s: jnp.minimum(s // n_c, rows_blocks - 1)
    col_rest = 3 * d_conv
    sw = tc * n_c
    n_col = (w_in.shape[1] - col_rest) // sw
    assert n_steps == 4 * n_col and col_rest % sw == 0
    tile = lambda s: jnp.minimum(s, n_steps - 1)
    side_in = pl.BlockSpec((d // 4, sw), lambda s: (tile(s) % 4, col_rest // sw + tile(s) // 4))
    side_out = pl.BlockSpec((d // 4, sw), lambda s: (tile(s) % 4, tile(s) // 4))
    side_shape = jax.ShapeDtypeStruct((d, w_in.shape[1] - col_rest), BF16)

    def wcol(k):
        return pl.BlockSpec((d, tc), lambda s: (0, k * n_c + jnp.minimum(conv(s), n_c - 1)),
                            pipeline_mode=pl.Buffered(1))

    return pl.pallas_call(
        functools.partial(_conv_prompt_kernel, n_slabs=n_steps, n_t=n_t, n_c=n_c),
        grid=(n_steps + 1,),
        in_specs=[pl.BlockSpec((tm, d), lambda s: (norm(s), 0)),
                  pl.BlockSpec((1, d), lambda s: (0, 0)),
                  wcol(0), wcol(1), wcol(2),
                  pl.BlockSpec((3, tc), lambda s: (0, conv(s) % n_c)),
                  side_in],
        out_specs=[pl.BlockSpec((tm, d), lambda s: (norm(s), 0)),
                   pl.BlockSpec((tm, tc), lambda s: (conv(s) // n_c, conv(s) % n_c)),
                   pl.BlockSpec((1, 2, d_conv), lambda s: (conv(s) // n_c // n_t, 0, 0)),
                   side_out],
        out_shape=[jax.ShapeDtypeStruct((mp, d), BF16),
                   jax.ShapeDtypeStruct((mp, d_conv), BF16),
                   jax.ShapeDtypeStruct((batch, 2, d_conv), F32),
                   side_shape],
        scratch_shapes=[pltpu.VMEM((2, tm, d), BF16), pltpu.VMEM((n_c, 3, d, tc), BF16),
                        pltpu.VMEM((n_c, 8, tc), F32)],
        compiler_params=_params(("arbitrary",)),
        name="conv_prompt",
    )(x, g.reshape(1, d), w_in, w_in, w_in, conv_w, w_in)


def _conv_sample_kernel(x_ref, g_ref, whc_ref, wbg_ref, wcg_ref, cw_ref, st_ref, hx_ref, a_ref, nc_ref):
    hx = _rms_scale(x_ref[...], g_ref[...]).astype(BF16)
    hx_ref[...] = hx
    hc = _dot(hx, whc_ref[...].astype(BF16))
    bg = _dot(hx, wbg_ref[...].astype(BF16))
    cg = _dot(hx, wcg_ref[...].astype(BF16))
    u = cg * hc
    cw = cw_ref[...]
    s1 = st_ref[:, 1, :]
    conv = cw[0:1, :] * st_ref[:, 0, :] + cw[1:2, :] * s1 + cw[2:3, :] * u
    a_ref[...] = (bg * conv).astype(a_ref.dtype)
    nc_ref[:, 0, :] = s1
    nc_ref[:, 1, :] = u


def _conv_sample(x, g, w_in, conv_w, state, d_conv, tc):
    n, d = x.shape
    n_c = d_conv // tc
    return pl.pallas_call(
        _conv_sample_kernel,
        grid=(n_c,),
        in_specs=[pl.BlockSpec((n, d), lambda c: (0, 0)),
                  pl.BlockSpec((1, d), lambda c: (0, 0)),
                  pl.BlockSpec((d, tc), lambda c: (0, c)),
                  pl.BlockSpec((d, tc), lambda c: (0, n_c + c)),
                  pl.BlockSpec((d, tc), lambda c: (0, 2 * n_c + c)),
                  pl.BlockSpec((3, tc), lambda c: (0, c)),
                  pl.BlockSpec((n, 2, tc), lambda c: (0, 0, c))],
        out_specs=[pl.BlockSpec((n, d), lambda c: (0, 0)),
                   pl.BlockSpec((n, tc), lambda c: (0, c)),
                   pl.BlockSpec((n, 2, tc), lambda c: (0, 0, c))],
        out_shape=[jax.ShapeDtypeStruct((n, d), BF16),
                   jax.ShapeDtypeStruct((n, d_conv), BF16),
                   jax.ShapeDtypeStruct((n, 2, d_conv), F32)],
        compiler_params=_params(("arbitrary",)),
        name="conv_sample",
    )(x, g.reshape(1, d), w_in, w_in, w_in, conv_w, state)


def _proj_kernel(x_ref, w_ref, o_ref):
    res = _dot(x_ref[...], w_ref[...])
    dk = o_ref.shape[2]
    for h in range(o_ref.shape[1]):
        o_ref[:, h, :] = res[:, h * dk:(h + 1) * dk]


def _proj(x, w, col0, n, tn, dk):
    rows, d = x.shape
    return pl.pallas_call(
        _proj_kernel,
        grid=(n // tn,),
        in_specs=[pl.BlockSpec((rows, d), lambda j: (0, 0)),
                  pl.BlockSpec((d, tn), lambda j: (0, col0 // tn + j))],
        out_specs=pl.BlockSpec((rows, tn // dk, dk), lambda j: (0, j, 0)),
        out_shape=jax.ShapeDtypeStruct((rows, n // dk, dk), F32),
        compiler_params=_params(("arbitrary",)),
        name="hgrn_proj_decode",
    )(x, w)


def _lower_bound(lbl):
    e = jnp.exp(lbl - jnp.max(lbl, axis=0))
    return e[0] / jnp.sum(e, axis=0)


def _head_norm_gate(o, g, og):
    return _rms_scale(o, g) * _silu(og)


def _cumsum_rows(x, tril_bf16):
    hi = x.astype(BF16)
    r1 = x - hi.astype(F32)
    mid = r1.astype(BF16)
    lo = (r1 - mid.astype(F32)).astype(BF16)
    n = x.shape[1]
    parts = _dot(tril_bf16, jnp.concatenate([hi, mid, lo], axis=1))
    return parts[:, 0:n] + parts[:, n:2 * n] + parts[:, 2 * n:3 * n]


def _hgrn_decode_step(p_ref, lb, g, s0_ref, o_ref, row0, sn_ref):
    bb, heads, dk = s0_ref.shape[0], s0_ref.shape[1], s0_ref.shape[2]
    head_row = lax.broadcasted_iota(jnp.int32, (heads, dk), 0)

    def column(row):
        return jnp.broadcast_to(row, (dk, dk)).T

    for i in range(bb):
        p = p_ref[i]
        qs = _silu(p[0:heads])
        f = lb + (1.0 - lb) * _sigmoid(p[heads:2 * heads])
        kk = 1.0 - f
        v = p[2 * heads:3 * heads]
        og = p[3 * heads:4 * heads]
        q_in = (qs * f).astype(BF16)
        o_inter = jnp.zeros((heads, dk), F32)
        for h in range(heads):
            s = s0_ref[i, h]
            sn_ref[i, h] = column(f[h:h + 1, :]) * s + column(kk[h:h + 1, :]) * v[h:h + 1, :]
            o_inter = o_inter + jnp.where(head_row == h, _dot(q_in, s.astype(BF16)), 0.0)
        o = jnp.sum(qs * kk, axis=-1, keepdims=True) * v + o_inter
        o = _head_norm_gate(o, g, og)
        o_ref[pl.ds(row0 + i, 1), :] = jnp.concatenate([o[h:h + 1, :] for h in range(heads)], axis=1)


def _hgrn_prompt_kernel(hx_ref, wq_ref, wf_ref, wi_ref, wo_ref, lbl_ref, g_ref, dp_ref, dlbl_ref, ds0_ref, *refs,
                        n_side, n_slabs, n_t):
    side_refs, refs = refs[:n_side], refs[n_side:]
    o_ref, s_ref, do_ref, dsn_ref = refs[:4]
    side_out_refs, (pa_ref, pb_ref, st_ref) = refs[4:4 + n_side], refs[4 + n_side:]
    s = pl.program_id(0)
    for side_ref, side_out_ref in zip(side_refs, side_out_refs):
        _slab_cast(s, n_slabs, side_ref, side_out_ref)
    tb = hx_ref.shape[0]
    heads_per_step = st_ref.shape[0]
    dk = HEAD_DIM
    sub = HGRN_SUB_ROWS
    n_sub = tb // sub
    n_chunks = sub // CHUNK
    t = lax.rem(jnp.maximum(s - 1, 0), n_t)

    @pl.when(s == 0)
    def _():
        pb_ref[...] = jnp.zeros_like(pb_ref)

    @pl.when(t == 0)
    def _():
        st_ref[...] = jnp.zeros_like(st_ref)

    lb_all = _lower_bound(lbl_ref[...])
    g = g_ref[...]
    row = lax.broadcasted_iota(jnp.int32, (sub, sub), 0)
    col = lax.broadcasted_iota(jnp.int32, (sub, sub), 1)
    shift = CHUNK.bit_length() - 1
    causal = (row >= col) & (jnp.right_shift(row, shift) == jnp.right_shift(col, shift))
    tril = causal.astype(BF16)

    def sub_block(p_next_ref, p_ref, sb):
        heads = range(heads_per_step)
        cols = [slice(hh * dk, (hh + 1) * dk) for hh in heads]
        rows = slice(sb * sub, (sb + 1) * sub)
        seqs = ds0_ref.shape[0] // n_sub
        own = pl.ds(sb * seqs, seqs)

        def project(k, w_ref):
            p_next_ref[k, rows] = _dot(hx_ref[rows, :], w_ref[...])

        qs, kk, lf = [], [], []
        for hh in heads:
            lb = lb_all[:, cols[hh]]
            f = lb + (1.0 - lb) * _sigmoid(p_ref[1, rows, cols[hh]])
            qs.append(_silu(p_ref[0, rows, cols[hh]]))
            kk.append(1.0 - f)
            lf.append(jnp.log(f))
        project(0, wq_ref)
        b = [_cumsum_rows(lf[hh], tril) for hh in heads]
        q_in, k_in, k_end, decay = [], [], [], []
        for hh in heads:
            b3 = b[hh].reshape(n_chunks, CHUNK, dk)
            b_last = b3[:, CHUNK - 1:CHUNK, :]
            b_end = jnp.broadcast_to(b_last, b3.shape).reshape(sub, dk)
            q_in.append((qs[hh] * jnp.exp(b[hh])).astype(BF16))
            k_in.append((kk[hh] * jnp.exp(-b[hh])).astype(BF16))
            k_end.append((kk[hh] * jnp.exp(b_end - b[hh])).astype(BF16))
            decay.append(jnp.exp(b_last))
        project(1, wf_ref)
        v = [p_ref[2, rows, cols[hh]].astype(BF16) for hh in heads]
        raw = [_dot_nt(q_in[hh], k_in[hh]) for hh in heads]
        delta = [[_dot_tn(v[hh][c * CHUNK:(c + 1) * CHUNK], k_end[hh][c * CHUNK:(c + 1) * CHUNK])
                  for c in range(n_chunks)] for hh in heads]
        scores = [jnp.where(causal, raw[hh], 0.0).astype(BF16) for hh in heads]
        starts = []
        for hh in heads:
            st = st_ref[hh]
            per_chunk = []
            for c in range(n_chunks):
                per_chunk.append(st.astype(BF16))
                st = decay[hh][c] * st + delta[hh][c]
            st_ref[hh] = st
            starts.append(per_chunk)
        project(2, wi_ref)
        row0 = jnp.minimum(s, n_slabs - 1) * ds0_ref.shape[0] + sb * seqs
        _hgrn_decode_step(dp_ref.at[own], _lower_bound(dlbl_ref[...]), g, ds0_ref.at[own], do_ref, row0,
                          dsn_ref.at[own])
        o = []
        for hh in heads:
            inter = [_dot_nt(q_in[hh][c * CHUNK:(c + 1) * CHUNK], starts[hh][c]) for c in range(n_chunks)]
            o.append(_dot(scores[hh], v[hh]) + jnp.concatenate(inter, axis=0))
        for hh in heads:
            o_ref[rows, cols[hh]] = _head_norm_gate(o[hh], g, p_ref[3, rows, cols[hh]]).astype(o_ref.dtype)
        project(3, wo_ref)

    def body(p_next_ref, p_ref):
        for sb in range(n_sub):
            sub_block(p_next_ref, p_ref, sb)

    parity = lax.rem(s, 2)

    @pl.when(parity == 0)
    def _():
        body(pa_ref, pb_ref)

    @pl.when(parity == 1)
    def _():
        body(pb_ref, pa_ref)

    @pl.when(t == n_t - 1)
    def _():
        for hh in range(heads_per_step):
            s_ref[0, hh] = st_ref[hh].T


def _hgrn_prompt(hx, w_in, col0, lb_logits, onorm_g, batch, seq, heads, tb, hp, side_ws, dec_p, dec_state):
    d = hx.shape[1]
    n_t = seq // tb
    n_h = heads // hp
    dk = HEAD_DIM
    wc = hp * dk
    assert CHUNK & (CHUNK - 1) == 0 and col0 % wc == 0
    rows_blocks = batch * n_t
    n_steps = n_h * rows_blocks
    proj = lambda s: jnp.minimum(s, n_steps - 1)
    rec = lambda s: jnp.maximum(s - 1, 0)
    sides = [_slab_specs(w, n_steps, lambda s: s) for w in side_ws]

    n_dec, dec_rows = dec_p.shape[0], dec_p.shape[1]
    assert n_dec % n_steps == 0
    db = n_dec // n_steps
    dec = lambda s: jnp.minimum(s, n_steps - 1)

    def wcol(k):
        return pl.BlockSpec((d, wc), lambda s: (0, col0 // wc + k * n_h + proj(s) // rows_blocks))

    return pl.pallas_call(
        functools.partial(_hgrn_prompt_kernel, n_side=len(sides), n_slabs=n_steps, n_t=n_t),
        grid=(n_steps + 1,),
        in_specs=[pl.BlockSpec((tb, d), lambda s: (proj(s) % rows_blocks, 0)),
                  wcol(0), wcol(1), wcol(2), wcol(3),
                  pl.BlockSpec((2, 1, wc), lambda s: (0, 0, rec(s) // rows_blocks)),
                  pl.BlockSpec((1, dk), lambda s: (0, 0)),
                  pl.BlockSpec((db, dec_rows, dk), lambda s: (dec(s), 0, 0)),
                  pl.BlockSpec((2, heads, dk), lambda s: (0, 0, 0)),
                  pl.BlockSpec((db, heads, dk, dk), lambda s: (dec(s), 0, 0, 0))]
        + [side[0] for side in sides],
        out_specs=[pl.BlockSpec((tb, wc), lambda s: (rec(s) % rows_blocks, rec(s) // rows_blocks)),
                   pl.BlockSpec((1, hp, dk, dk),
                                lambda s: ((rec(s) % rows_blocks) // n_t, rec(s) // rows_blocks, 0, 0)),
                   pl.BlockSpec((n_dec, heads * dk), lambda s: (0, 0)),
                   pl.BlockSpec((db, heads, dk, dk), lambda s: (dec(s), 0, 0, 0))]
        + [side[1] for side in sides],
        out_shape=[jax.ShapeDtypeStruct((batch * seq, heads * dk), BF16),
                   jax.ShapeDtypeStruct((batch, heads, dk, dk), F32),
                   jax.ShapeDtypeStruct((n_dec, heads * dk), F32),
                   jax.ShapeDtypeStruct((n_dec, heads, dk, dk), F32)]
        + [side[2] for side in sides],
        scratch_shapes=[pltpu.VMEM((4, tb, wc), F32), pltpu.VMEM((4, tb, wc), F32),
                        pltpu.VMEM((hp, dk, dk), F32)],
        compiler_params=_params(("arbitrary",)),
        name="hgrn_prompt",
    )(hx, w_in, w_in, w_in, w_in, lb_logits.reshape(2, 1, heads * dk), onorm_g.reshape(1, dk),
      dec_p, lb_logits.reshape(2, heads, dk), dec_state, *side_ws)


def _merge_kernel(hx_ref, a_ref, o_ref, ht_ref, at_ref, ot_ref, wga_ref, wgb_ref, wa_ref, wb_ref, side_a_ref,
                  side_b_ref, mix_ref, side_a_out_ref, side_b_out_ref, *, n_full, n_slabs):
    j, i = pl.program_id(0), pl.program_id(1)
    _slab_cast(j * pl.num_programs(1) + i, n_slabs, side_a_ref, side_a_out_ref)
    _slab_cast(j * pl.num_programs(1) + i, n_slabs, side_b_ref, side_b_out_ref)

    def body(rows, is_tail):
        hx = ht_ref[...] if is_tail else hx_ref[...]
        a = at_ref[...] if is_tail else a_ref[...]
        o = ot_ref[...].astype(BF16) if is_tail else o_ref[...]
        ga = _sigmoid(_dot(hx, wga_ref[...]))
        gb = _sigmoid(_dot(hx, wgb_ref[...]))
        mix = ga * _dot(a, wa_ref[...]) + gb * _dot(o, wb_ref[...])
        mix_ref[0:rows, :] = mix.astype(mix_ref.dtype)

    _row_split(i, n_full, hx_ref.shape[0], ht_ref.shape[0], body)


def _merge(hx, a, o, hx_tail, a_tail, o_tail, w_in, w_a, w_b, col_ga, tm, tn, side_a, side_b, n_slabs):
    d = hx.shape[1]
    m = a.shape[0] + hx_tail.shape[0]
    dc, dh = a.shape[1], o.shape[1]
    tail = a_tail.shape[0]
    n_full = a.shape[0] // tm
    n_n = d // tn
    clamp = lambda j, i: (jnp.minimum(i, n_full - 1), 0)
    assert n_slabs <= n_n * (n_full + 1)
    step_of = lambda j, i: j * (n_full + 1) + i
    a_in, a_out, a_shape = _slab_specs(side_a, n_slabs, step_of)
    b_in, b_out, b_shape = _slab_specs(side_b, n_slabs, step_of)
    return pl.pallas_call(
        functools.partial(_merge_kernel, n_full=n_full, n_slabs=n_slabs),
        grid=(n_n, n_full + 1),
        in_specs=[pl.BlockSpec((tm, d), clamp),
                  pl.BlockSpec((tm, dc), clamp),
                  pl.BlockSpec((tm, dh), clamp),
                  pl.BlockSpec((tail, d), lambda j, i: (0, 0)),
                  pl.BlockSpec((tail, dc), lambda j, i: (0, 0)),
                  pl.BlockSpec((tail, dh), lambda j, i: (0, 0)),
                  pl.BlockSpec((d, tn), lambda j, i: (0, col_ga // tn + j)),
                  pl.BlockSpec((d, tn), lambda j, i: (0, col_ga // tn + n_n + j)),
                  pl.BlockSpec((dc, tn), lambda j, i: (0, j)),
                  pl.BlockSpec((dh, tn), lambda j, i: (0, j)),
                  a_in, b_in],
        out_specs=[pl.BlockSpec((tm, tn), lambda j, i: (i, j)), a_out, b_out],
        out_shape=[jax.ShapeDtypeStruct((m, d), BF16), a_shape, b_shape],
        compiler_params=_params(("arbitrary", "arbitrary")),
        name="merge",
    )(hx, a, o, hx_tail, a_tail, o_tail, w_in, w_in, w_a, w_b, side_a, side_b)


def _outproj_kernel(xp_ref, xs_ref, mix_ref, w_ref, g_ref, x1_ref, h2_ref, *, n_full):
    i = pl.program_id(0)

    def body(rows, is_tail):
        x = xs_ref[...] if is_tail else xp_ref[...]
        x1 = x + _dot(mix_ref[0:rows, :], w_ref[...])
        x1_ref[0:rows, :] = x1
        h2_ref[0:rows, :] = _rms_scale(x1, g_ref[...]).astype(h2_ref.dtype)

    _row_split(i, n_full, xp_ref.shape[0], xs_ref.shape[0], body)


def _outproj(xp, xs, mix, w_out, g, tm):
    mp, d = xp.shape
    tail = xs.shape[0]
    n_full = mp // tm
    return pl.pallas_call(
        functools.partial(_outproj_kernel, n_full=n_full),
        grid=(n_full + 1,),
        in_specs=[pl.BlockSpec((tm, d), lambda i: (jnp.minimum(i, n_full - 1), 0)),
                  pl.BlockSpec((tail, d), lambda i: (0, 0)),
                  pl.BlockSpec((tm, d), lambda i: (i, 0)),
                  pl.BlockSpec((d, d), lambda i: (0, 0), pipeline_mode=pl.Buffered(1)),
                  pl.BlockSpec((1, d), lambda i: (0, 0))],
        out_specs=[pl.BlockSpec((tm, d), lambda i: (i, 0)),
                   pl.BlockSpec((tm, d), lambda i: (i, 0))],
        out_shape=[jax.ShapeDtypeStruct((mp + tail, d), F32),
                   jax.ShapeDtypeStruct((mp + tail, d), BF16)],
        compiler_params=_params(("arbitrary",), VMEM_LIMIT_LARGE),
        name="outproj",
    )(xp, xs, mix, w_out, g.reshape(1, d))


def _mlp_kernel(h2_ref, x1_ref, wup_ref, wdn_ref, g_ref, yp_ref, ys_ref, *, n_full):
    i, j = pl.program_id(0), pl.program_id(1)
    last = pl.num_programs(1) - 1

    def body(rows, is_tail):
        y_ref = ys_ref if is_tail else yp_ref

        def step(first, final):
            halves = 2 if (final and not is_tail) else 1
            for hf in range(halves):
                rs = slice(hf * rows // halves, (hf + 1) * rows // halves)
                h = jnp.maximum(_dot(h2_ref[rs, :], wup_ref[...]), 0.0)
                acc = (x1_ref[rs, :] if first else y_ref[rs, :]) + _dot((h * h).astype(BF16), wdn_ref[...])
                y_ref[rs, :] = _rms_scale(acc, g_ref[...]) if final else acc

        @pl.when(j == 0)
        def _():
            step(True, False)

        @pl.when((j > 0) & (j < last))
        def _():
            step(False, False)

        @pl.when(j == last)
        def _():
            step(False, True)

    _row_split(i, n_full, yp_ref.shape[0], ys_ref.shape[0], body)


def _mlp(h2, x1, w_up, w_down, g, tm, tf, tail):
    m, d = h2.shape
    dff = w_up.shape[1]
    assert dff // tf >= 2
    n_full = (m - tail) // tm
    return pl.pallas_call(
        functools.partial(_mlp_kernel, n_full=n_full),
        grid=(n_full + 1, dff // tf),
        in_specs=[pl.BlockSpec((tm, d), lambda i, j: (i, 0)),
                  pl.BlockSpec((tm, d), lambda i, j: (i, 0)),
                  pl.BlockSpec((d, tf), lambda i, j: (0, j)),
                  pl.BlockSpec((tf, d), lambda i, j: (j, 0)),
                  pl.BlockSpec((1, d), lambda i, j: (0, 0))],
        out_specs=[pl.BlockSpec((tm, d), lambda i, j: (jnp.minimum(i, n_full - 1), 0)),
                   pl.BlockSpec((tail, d), lambda i, j: (0, 0))],
        out_shape=[jax.ShapeDtypeStruct((m - tail, d), F32),
                   jax.ShapeDtypeStruct((tail, d), F32)],
        compiler_params=_params(("arbitrary", "arbitrary"), VMEM_LIMIT_LARGE),
        name="mlp",
    )(h2, x1, w_up, w_down, g.reshape(1, d))


def kernel(x_prompt, x_sample, state_conv, state_hgrn, norm_mix, w_in, conv_w, lb_logits, onorm_g,
           w_branch_a, w_branch_b, w_out, norm_ffn, w_up, w_down, norm_final):
    batch, seq, d = x_prompt.shape
    n_dec = x_sample.shape[0]
    depth, _, d_conv = conv_w.shape
    heads, dk = state_hgrn.shape[2], state_hgrn.shape[3]
    d_hgrn = heads * dk
    mp = batch * seq
    assert depth == 1 and x_sample.shape[1] == 1 and dk == HEAD_DIM and state_hgrn.shape[4] == dk
    assert seq % ROW_BLOCK == 0 and mp % MLP_ROW_BLOCK == 0 and ROW_BLOCK % n_dec == 0
    col_hgrn = 0
    col_ga = col_hgrn + 4 * d_hgrn
    w_in0 = w_in.reshape(w_in.shape[1:])
    w_a0 = w_branch_a.reshape(w_branch_a.shape[1:])
    w_b0 = w_branch_b.reshape(w_branch_b.shape[1:])
    w_out0 = w_out.reshape(w_out.shape[1:])
    w_up0 = w_up.reshape(w_up.shape[1:])
    w_down0 = w_down.reshape(w_down.shape[1:])

    xp = x_prompt.reshape(mp, d)
    xs = x_sample.reshape(n_dec, d)
    hx, a_p, conv_p, w_rest_b = _conv_prompt(xp, norm_mix[0], w_in0, conv_w[0], batch, seq, d_conv,
                                             ROW_BLOCK, CONV_COL_BLOCK)
    hx_s, a_s, conv_s = _conv_sample(xs, norm_mix[0], w_in0, conv_w[0], state_conv[0], d_conv, CONV_COL_BLOCK)

    p_s = _proj(hx_s, w_rest_b, col_hgrn, 4 * d_hgrn, PROJ_COL_BLOCK, dk)
    o_p, hgrn_p, o_s, hgrn_s, w_up_b, w_a_b, w_b_b = _hgrn_prompt(
        hx, w_rest_b, col_hgrn, lb_logits, onorm_g[0], batch, seq, heads, HGRN_ROWS, HGRN_HEADS, (w_up0, w_a0, w_b0),
        p_s, state_hgrn[0])

    mix, w_out_b, w_down_b = _merge(hx, a_p, o_p, hx_s, a_s, o_s, w_rest_b, w_a_b, w_b_b,
                                    col_ga, ROW_BLOCK, MERGE_COL_BLOCK, w_out0, w_down0, SIDE_SLABS // 2)
    x1, h2 = _outproj(xp, xs, mix, w_out_b, norm_ffn[0], OUTPROJ_ROW_BLOCK)
    y_p, y_s = _mlp(h2, x1, w_up_b, w_down_b, norm_final, MLP_ROW_BLOCK, MLP_FF_BLOCK, n_dec)

    return (y_p.reshape(batch, seq, d), y_s.reshape(n_dec, 1, d),
            conv_p[None], hgrn_p[None], conv_s[None], hgrn_s[None])
```

```python
import functools

import jax
import jax.numpy as jnp
from jax import lax
from jax.experimental import pallas as pl
from jax.experimental.pallas import tpu as pltpu

EPS = 1e-6
CHUNK = 64
HEAD_DIM = 128
V7X_VMEM_BYTES = 64 * 1024 * 1024
VMEM_LIMIT = V7X_VMEM_BYTES * 7 // 8
VMEM_LIMIT_MLP = V7X_VMEM_BYTES * 31 // 32

ROW_BLOCK = 512
MLP_ROW_BLOCK = 1024
MLP_FF_BLOCK = 1024
CONV_COL_BLOCK = 512
CONV_SUB_BLOCK = 256
PROJ_COL_BLOCK = 1024
MERGE_COL_BLOCK = 2048
MERGE_SUB_BLOCK = 1024
SIDE_SLABS = 64
HGRN_ROWS = 512
HGRN_SUB_ROWS = 256
HGRN_HEADS = 4

BF16 = jnp.bfloat16
F32 = jnp.float32


def _params(semantics, vmem_limit=VMEM_LIMIT):
    return pltpu.CompilerParams(dimension_semantics=semantics, vmem_limit_bytes=vmem_limit)


def _dot(a, b):
    return jnp.dot(a, b, preferred_element_type=F32)


def _dot_nt(a, b):
    return lax.dot_general(a, b, (((1,), (1,)), ((), ())), preferred_element_type=F32)


def _dot_tn(a, b):
    return lax.dot_general(a, b, (((0,), (0,)), ((), ())), preferred_element_type=F32)


def _sigmoid(x):
    return jax.nn.sigmoid(x)


def _silu(x):
    return x * jax.nn.sigmoid(x)


def _rms_scale(x, g):
    ms = jnp.mean(x * x, axis=-1, keepdims=True)
    return x * lax.rsqrt(ms + EPS) * g


def _row_split(i, n_full, full_rows, tail_rows, body):
    @pl.when(i < n_full)
    def _():
        body(full_rows, False)

    @pl.when(i == n_full)
    def _():
        body(tail_rows, True)


def _slab_cast(step, n_slabs, src_ref, dst_ref):
    @pl.when(step < n_slabs)
    def _():
        dst_ref[...] = src_ref[...].astype(dst_ref.dtype)


def _slab_specs(w, n_slabs, step_of):
    r, c = w.shape
    assert r % n_slabs == 0

    def index(*g):
        return (jnp.minimum(step_of(*g), n_slabs - 1), 0)

    spec = pl.BlockSpec((r // n_slabs, c), index)
    return spec, spec, jax.ShapeDtypeStruct((r, c), BF16)


def _conv_prompt_kernel(x_ref, g_ref, whc_ref, wbg_ref, wcg_ref, cw_ref, side_ref,
                        hx_out_ref, a_ref, nc_ref, side_out_ref,
                        h_ref, wb_ref, carry_ref, *, n_slabs, n_t, n_c):
    s = pl.program_id(0)
    tm, tc = a_ref.shape
    j = jnp.maximum(s - 1, 0)
    t = lax.rem(j // n_c, n_t)
    _slab_cast(s, n_slabs, side_ref, side_out_ref)

    def normalise(slot):
        hx = _rms_scale(x_ref[...], g_ref[...]).astype(BF16)
        h_ref[slot] = hx
        hx_out_ref[...] = hx
        return hx

    @pl.when(s == 0)
    def _():
        normalise(0)

    @pl.when((s >= 1) & (s <= n_c))
    def _():
        for c in range(n_c):
            @pl.when(s == c + 1)
            def _(c=c):
                wb_ref[c, 0] = whc_ref[...].astype(BF16)
                wb_ref[c, 1] = wbg_ref[...].astype(BF16)
                wb_ref[c, 2] = wcg_ref[...].astype(BF16)

    cw = cw_ref[...]
    row = lax.broadcasted_iota(jnp.int32, (tm, CONV_SUB_BLOCK), 0)

    def body(c, slot):
        @pl.when((t == 0) & (c == 0))
        def _():
            carry_ref[...] = jnp.zeros_like(carry_ref)

        last_channel = c == n_c - 1
        hx_next = normalise(1 - slot) if last_channel else None
        hx = h_ref[slot]
        n_sub = tc // CONV_SUB_BLOCK
        for sub in range(n_sub):
            cs = slice(sub * CONV_SUB_BLOCK, (sub + 1) * CONV_SUB_BLOCK)
            lhs = jnp.where(s < 0, hx_next, hx) if (last_channel and sub == n_sub - 1) else hx
            hc = _dot(lhs, wb_ref[c, 0, :, cs])
            bg = _dot(lhs, wb_ref[c, 1, :, cs])
            cg = _dot(lhs, wb_ref[c, 2, :, cs])
            u = cg * hc
            c0 = carry_ref[c, 0:1, cs]
            c1 = carry_ref[c, 1:2, cs]
            u1 = jnp.where(row == 0, c1, pltpu.roll(u, 1, 0))
            u2 = jnp.where(row == 0, c0, jnp.where(row == 1, c1, pltpu.roll(u, 2, 0)))
            conv = cw[0:1, cs] * u2 + cw[1:2, cs] * u1 + cw[2:3, cs] * u
            a_ref[:, cs] = (bg * conv).astype(a_ref.dtype)
            carry_ref[c, 0:2, cs] = u[tm - 2:tm, :]

        @pl.when(t == n_t - 1)
        def _():
            nc_ref[0, :, c * tc:(c + 1) * tc] = carry_ref[c, 0:2, :]

    phase = lax.rem(j, 2 * n_c)
    for m in range(2 * n_c):
        @pl.when((s >= 1) & (phase == m))
        def _(m=m):
            body(m % n_c, m // n_c)


def _conv_prompt(x, g, w_in, conv_w, batch, seq, d_conv, tm, tc):
    mp, d = x.shape
    n_c = d_conv // tc
    n_t = seq // tm
    rows_blocks = batch * n_t
    n_steps = n_c * rows_blocks
    conv = lambda s: jnp.maximum(s - 1, 0)
    norm = lambda s: jnp.minimum(s // n_c, rows_blocks - 1)
    col_rest = 3 * d_conv
    sw = tc * n_c
    n_col = (w_in.shape[1] - col_rest) // sw
    assert n_steps == 4 * n_col and col_rest % sw == 0
    tile = lambda s: jnp.minimum(s, n_steps - 1)
    side_in = pl.BlockSpec((d // 4, sw), lambda s: (tile(s) % 4, col_rest // sw + tile(s) // 4))
    side_out = pl.BlockSpec((d // 4, sw), lambda s: (tile(s) % 4, tile(s) // 4))
    side_shape = jax.ShapeDtypeStruct((d, w_in.shape[1] - col_rest), BF16)

    def wcol(k):
        return pl.BlockSpec((d, tc), lambda s: (0, k * n_c + jnp.minimum(conv(s), n_c - 1)),
                            pipeline_mode=pl.Buffered(1))

    return pl.pallas_call(
        functools.partial(_conv_prompt_kernel, n_slabs=n_steps, n_t=n_t, n_c=n_c),
        grid=(n_steps + 1,),
        in_specs=[pl.BlockSpec((tm, d), lambda s: (norm(s), 0)),
                  pl.BlockSpec((1, d), lambda s: (0, 0)),
                  wcol(0), wcol(1), wcol(2),
                  pl.BlockSpec((3, tc), lambda s: (0, conv(s) % n_c)),
                  side_in],
        out_specs=[pl.BlockSpec((tm, d), lambda s: (norm(s), 0)),
                   pl.BlockSpec((tm, tc), lambda s: (conv(s) // n_c, conv(s) % n_c)),
                   pl.BlockSpec((1, 2, d_conv), lambda s: (conv(s) // n_c // n_t, 0, 0)),
                   side_out],
        out_shape=[jax.ShapeDtypeStruct((mp, d), BF16),
                   jax.ShapeDtypeStruct((mp, d_conv), BF16),
                   jax.ShapeDtypeStruct((batch, 2, d_conv), F32),
                   side_shape],
        scratch_shapes=[pltpu.VMEM((2, tm, d), BF16), pltpu.VMEM((n_c, 3, d, tc), BF16),
                        pltpu.VMEM((n_c, 8, tc), F32)],
        compiler_params=_params(("arbitrary",)),
        name="conv_prompt",
    )(x, g.reshape(1, d), w_in, w_in, w_in, conv_w, w_in)


def _conv_sample_kernel(x_ref, g_ref, whc_ref, wbg_ref, wcg_ref, cw_ref, st_ref, hx_ref, a_ref, nc_ref):
    hx = _rms_scale(x_ref[...], g_ref[...]).astype(BF16)
    hx_ref[...] = hx
    hc = _dot(hx, whc_ref[...].astype(BF16))
    bg = _dot(hx, wbg_ref[...].astype(BF16))
    cg = _dot(hx, wcg_ref[...].astype(BF16))
    u = cg * hc
    cw = cw_ref[...]
    s1 = st_ref[:, 1, :]
    conv = cw[0:1, :] * st_ref[:, 0, :] + cw[1:2, :] * s1 + cw[2:3, :] * u
    a_ref[...] = (bg * conv).astype(a_ref.dtype)
    nc_ref[:, 0, :] = s1
    nc_ref[:, 1, :] = u


def _conv_sample(x, g, w_in, conv_w, state, d_conv, tc):
    n, d = x.shape
    n_c = d_conv // tc
    return pl.pallas_call(
        _conv_sample_kernel,
        grid=(n_c,),
        in_specs=[pl.BlockSpec((n, d), lambda c: (0, 0)),
                  pl.BlockSpec((1, d), lambda c: (0, 0)),
                  pl.BlockSpec((d, tc), lambda c: (0, c)),
                  pl.BlockSpec((d, tc), lambda c: (0, n_c + c)),
                  pl.BlockSpec((d, tc), lambda c: (0, 2 * n_c + c)),
                  pl.BlockSpec((3, tc), lambda c: (0, c)),
                  pl.BlockSpec((n, 2, tc), lambda c: (0, 0, c))],
        out_specs=[pl.BlockSpec((n, d), lambda c: (0, 0)),
                   pl.BlockSpec((n, tc), lambda c: (0, c)),
                   pl.BlockSpec((n, 2, tc), lambda c: (0, 0, c))],
        out_shape=[jax.ShapeDtypeStruct((n, d), BF16),
                   jax.ShapeDtypeStruct((n, d_conv), BF16),
                   jax.ShapeDtypeStruct((n, 2, d_conv), F32)],
        compiler_params=_params(("arbitrary",)),
        name="conv_sample",
    )(x, g.reshape(1, d), w_in, w_in, w_in, conv_w, state)


def _proj_kernel(x_ref, w_ref, o_ref):
    res = _dot(x_ref[...], w_ref[...])
    dk = o_ref.shape[2]
    for h in range(o_ref.shape[1]):
        o_ref[:, h, :] = res[:, h * dk:(h + 1) * dk]


def _proj(x, w, col0, n, tn, dk):
    rows, d = x.shape
    return pl.pallas_call(
        _proj_kernel,
        grid=(n // tn,),
        in_specs=[pl.BlockSpec((rows, d), lambda j: (0, 0)),
                  pl.BlockSpec((d, tn), lambda j: (0, col0 // tn + j))],
        out_specs=pl.BlockSpec((rows, tn // dk, dk), lambda j: (0, j, 0)),
        out_shape=jax.ShapeDtypeStruct((rows, n // dk, dk), F32),
        compiler_params=_params(("arbitrary",)),
        name="hgrn_proj_decode",
    )(x, w)


def _lower_bound(lbl):
    e = jnp.exp(lbl - jnp.max(lbl, axis=0))
    return e[0] / jnp.sum(e, axis=0)


def _head_norm_gate(o, g, og):
    return _rms_scale(o, g) * _silu(og)


def _cumsum_rows(x, tril_bf16):
    hi = x.astype(BF16)
    r1 = x - hi.astype(F32)
    mid = r1.astype(BF16)
    lo = (r1 - mid.astype(F32)).astype(BF16)
    n = x.shape[1]
    parts = _dot(tril_bf16, jnp.concatenate([hi, mid, lo], axis=1))
    return parts[:, 0:n] + parts[:, n:2 * n] + parts[:, 2 * n:3 * n]


def _hgrn_decode_step(p_ref, lb, g, s0_ref, o_ref, row0, sn_ref):
    bb, heads, dk = s0_ref.shape[0], s0_ref.shape[1], s0_ref.shape[2]
    head_row = lax.broadcasted_iota(jnp.int32, (heads, dk), 0)

    def column(row):
        return jnp.broadcast_to(row, (dk, dk)).T

    for i in range(bb):
        p = p_ref[i]
        qs = _silu(p[0:heads])
        f = lb + (1.0 - lb) * _sigmoid(p[heads:2 * heads])
        kk = 1.0 - f
        v = p[2 * heads:3 * heads]
        og = p[3 * heads:4 * heads]
        q_in = (qs * f).astype(BF16)
        o_inter = jnp.zeros((heads, dk), F32)
        for h in range(heads):
            s = s0_ref[i, h]
            sn_ref[i, h] = column(f[h:h + 1, :]) * s + column(kk[h:h + 1, :]) * v[h:h + 1, :]
            o_inter = o_inter + jnp.where(head_row == h, _dot(q_in, s.astype(BF16)), 0.0)
        o = jnp.sum(qs * kk, axis=-1, keepdims=True) * v + o_inter
        o = _head_norm_gate(o, g, og)
        o_ref[pl.ds(row0 + i, 1), :] = jnp.concatenate([o[h:h + 1, :] for h in range(heads)], axis=1)


def _hgrn_prompt_kernel(hx_ref, wq_ref, wf_ref, wi_ref, wo_ref, lbl_ref, g_ref, dp_ref, dlbl_ref, ds0_ref, *refs,
                        n_side, n_slabs, n_t):
    side_refs, refs = refs[:n_side], refs[n_side:]
    o_ref, s_ref, do_ref, dsn_ref = refs[:4]
    side_out_refs, (pa_ref, pb_ref, st_ref) = refs[4:4 + n_side], refs[4 + n_side:]
    s = pl.program_id(0)
    for side_ref, side_out_ref in zip(side_refs, side_out_refs):
        _slab_cast(s, n_slabs, side_ref, side_out_ref)
    tb = hx_ref.shape[0]
    heads_per_step = st_ref.shape[0]
    dk = HEAD_DIM
    sub = HGRN_SUB_ROWS
    n_sub = tb // sub
    n_chunks = sub // CHUNK
    t = lax.rem(jnp.maximum(s - 1, 0), n_t)

    @pl.when(s == 0)
    def _():
        pb_ref[...] = jnp.zeros_like(pb_ref)

    @pl.when(t == 0)
    def _():
        st_ref[...] = jnp.zeros_like(st_ref)

    lb_all = _lower_bound(lbl_ref[...])
    g = g_ref[...]
    row = lax.broadcasted_iota(jnp.int32, (sub, sub), 0)
    col = lax.broadcasted_iota(jnp.int32, (sub, sub), 1)
    shift = CHUNK.bit_length() - 1
    causal = (row >= col) & (jnp.right_shift(row, shift) == jnp.right_shift(col, shift))
    tril = causal.astype(BF16)

    def sub_block(p_next_ref, p_ref, sb):
        heads = range(heads_per_step)
        cols = [slice(hh * dk, (hh + 1) * dk) for hh in heads]
        rows = slice(sb * sub, (sb + 1) * sub)
        seqs = ds0_ref.shape[0] // n_sub
        own = pl.ds(sb * seqs, seqs)

        def project(k, w_ref):
            p_next_ref[k, rows] = _dot(hx_ref[rows, :], w_ref[...])

        qs, kk, lf = [], [], []
        for hh in heads:
            lb = lb_all[:, cols[hh]]
            f = lb + (1.0 - lb) * _sigmoid(p_ref[1, rows, cols[hh]])
            qs.append(_silu(p_ref[0, rows, cols[hh]]))
            kk.append(1.0 - f)
            lf.append(jnp.log(f))
        project(0, wq_ref)
        b = [_cumsum_rows(lf[hh], tril) for hh in heads]
        q_in, k_in, k_end, decay = [], [], [], []
        for hh in heads:
            b3 = b[hh].reshape(n_chunks, CHUNK, dk)
            b_last = b3[:, CHUNK - 1:CHUNK, :]
            b_end = jnp.broadcast_to(b_last, b3.shape).reshape(sub, dk)
            q_in.append((qs[hh] * jnp.exp(b[hh])).astype(BF16))
            k_in.append((kk[hh] * jnp.exp(-b[hh])).astype(BF16))
            k_end.append((kk[hh] * jnp.exp(b_end - b[hh])).astype(BF16))
            decay.append(jnp.exp(b_last))
        project(1, wf_ref)
        v = [p_ref[2, rows, cols[hh]].astype(BF16) for hh in heads]
        raw = [_dot_nt(q_in[hh], k_in[hh]) for hh in heads]
        delta = [[_dot_tn(v[hh][c * CHUNK:(c + 1) * CHUNK], k_end[hh][c * CHUNK:(c + 1) * CHUNK])
                  for c in range(n_chunks)] for hh in heads]
        scores = [jnp.where(causal, raw[hh], 0.0).astype(BF16) for hh in heads]
        starts = []
        for hh in heads:
            st = st_ref[hh]
            per_chunk = []
            for c in range(n_chunks):
                per_chunk.append(st.astype(BF16))
                st = decay[hh][c] * st + delta[hh][c]
            st_ref[hh] = st
            starts.append(per_chunk)
        project(2, wi_ref)
        row0 = jnp.minimum(s, n_slabs - 1) * ds0_ref.shape[0] + sb * seqs
        _hgrn_decode_step(dp_ref.at[own], _lower_bound(dlbl_ref[...]), g, ds0_ref.at[own], do_ref, row0,
                          dsn_ref.at[own])
        o = []
        for hh in heads:
            inter = [_dot_nt(q_in[hh][c * CHUNK:(c + 1) * CHUNK], starts[hh][c]) for c in range(n_chunks)]
            o.append(_dot(scores[hh], v[hh]) + jnp.concatenate(inter, axis=0))
        for hh in heads:
            o_ref[rows, cols[hh]] = _head_norm_gate(o[hh], g, p_ref[3, rows, cols[hh]]).astype(o_ref.dtype)
        project(3, wo_ref)

    def body(p_next_ref, p_ref):
        for sb in range(n_sub):
            sub_block(p_next_ref, p_ref, sb)

    parity = lax.rem(s, 2)

    @pl.when(parity == 0)
    def _():
        body(pa_ref, pb_ref)

    @pl.when(parity == 1)
    def _():
        body(pb_ref, pa_ref)

    @pl.when(t == n_t - 1)
    def _():
        for hh in range(heads_per_step):
            s_ref[0, hh] = st_ref[hh].T


def _hgrn_prompt(hx, w_in, col0, lb_logits, onorm_g, batch, seq, heads, tb, hp, side_ws, dec_p, dec_state):
    d = hx.shape[1]
    n_t = seq // tb
    n_h = heads // hp
    dk = HEAD_DIM
    wc = hp * dk
    assert CHUNK & (CHUNK - 1) == 0 and col0 % wc == 0
    rows_blocks = batch * n_t
    n_steps = n_h * rows_blocks
    proj = lambda s: jnp.minimum(s, n_steps - 1)
    rec = lambda s: jnp.maximum(s - 1, 0)
    sides = [_slab_specs(w, n_steps, lambda s: s) for w in side_ws]

    n_dec, dec_rows = dec_p.shape[0], dec_p.shape[1]
    assert n_dec % n_steps == 0
    db = n_dec // n_steps
    dec = lambda s: jnp.minimum(s, n_steps - 1)

    def wcol(k):
        return pl.BlockSpec((d, wc), lambda s: (0, col0 // wc + k * n_h + proj(s) // rows_blocks))

    return pl.pallas_call(
        functools.partial(_hgrn_prompt_kernel, n_side=len(sides), n_slabs=n_steps, n_t=n_t),
        grid=(n_steps + 1,),
        in_specs=[pl.BlockSpec((tb, d), lambda s: (proj(s) % rows_blocks, 0)),
                  wcol(0), wcol(1), wcol(2), wcol(3),
                  pl.BlockSpec((2, 1, wc), lambda s: (0, 0, rec(s) // rows_blocks)),
                  pl.BlockSpec((1, dk), lambda s: (0, 0)),
                  pl.BlockSpec((db, dec_rows, dk), lambda s: (dec(s), 0, 0)),
                  pl.BlockSpec((2, heads, dk), lambda s: (0, 0, 0)),
                  pl.BlockSpec((db, heads, dk, dk), lambda s: (dec(s), 0, 0, 0))]
        + [side[0] for side in sides],
        out_specs=[pl.BlockSpec((tb, wc), lambda s: (rec(s) % rows_blocks, rec(s) // rows_blocks)),
                   pl.BlockSpec((1, hp, dk, dk),
                                lambda s: ((rec(s) % rows_blocks) // n_t, rec(s) // rows_blocks, 0, 0)),
                   pl.BlockSpec((n_dec, heads * dk), lambda s: (0, 0)),
                   pl.BlockSpec((db, heads, dk, dk), lambda s: (dec(s), 0, 0, 0))]
        + [side[1] for side in sides],
        out_shape=[jax.ShapeDtypeStruct((batch * seq, heads * dk), BF16),
                   jax.ShapeDtypeStruct((batch, heads, dk, dk), F32),
                   jax.ShapeDtypeStruct((n_dec, heads * dk), F32),
                   jax.ShapeDtypeStruct((n_dec, heads, dk, dk), F32)]
        + [side[2] for side in sides],
        scratch_shapes=[pltpu.VMEM((4, tb, wc), F32), pltpu.VMEM((4, tb, wc), F32),
                        pltpu.VMEM((hp, dk, dk), F32)],
        compiler_params=_params(("arbitrary",)),
        name="hgrn_prompt",
    )(hx, w_in, w_in, w_in, w_in, lb_logits.reshape(2, 1, heads * dk), onorm_g.reshape(1, dk),
      dec_p, lb_logits.reshape(2, heads, dk), dec_state, *side_ws)


def _merge_kernel(hx_ref, a_ref, o_ref, ht_ref, at_ref, ot_ref, wga_ref, wgb_ref, wa_ref, wb_ref, side_a_ref,
                  side_b_ref, mix_ref, side_a_out_ref, side_b_out_ref, *, n_full, n_slabs):
    j, i = pl.program_id(0), pl.program_id(1)
    _slab_cast(j * pl.num_programs(1) + i, n_slabs, side_a_ref, side_a_out_ref)
    _slab_cast(j * pl.num_programs(1) + i, n_slabs, side_b_ref, side_b_out_ref)

    def body(rows, is_tail):
        hx = ht_ref[...] if is_tail else hx_ref[...]
        a = at_ref[...] if is_tail else a_ref[...]
        o = ot_ref[...].astype(BF16) if is_tail else o_ref[...]
        for sub in range(mix_ref.shape[1] // MERGE_SUB_BLOCK):
            cs = slice(sub * MERGE_SUB_BLOCK, (sub + 1) * MERGE_SUB_BLOCK)
            ga = _sigmoid(_dot(hx, wga_ref[:, cs]))
            gb = _sigmoid(_dot(hx, wgb_ref[:, cs]))
            mix = ga * _dot(a, wa_ref[:, cs]) + gb * _dot(o, wb_ref[:, cs])
            mix_ref[0:rows, cs] = mix.astype(mix_ref.dtype)

    _row_split(i, n_full, hx_ref.shape[0], ht_ref.shape[0], body)


def _merge(hx, a, o, hx_tail, a_tail, o_tail, w_in, w_a, w_b, col_ga, tm, tn, side_a, side_b, n_slabs):
    d = hx.shape[1]
    m = a.shape[0] + hx_tail.shape[0]
    dc, dh = a.shape[1], o.shape[1]
    tail = a_tail.shape[0]
    n_full = a.shape[0] // tm
    n_n = d // tn
    clamp = lambda j, i: (jnp.minimum(i, n_full - 1), 0)
    assert n_slabs <= n_n * (n_full + 1)
    step_of = lambda j, i: j * (n_full + 1) + i
    weights = pl.Buffered(1) if n_n == 1 else None
    a_in, a_out, a_shape = _slab_specs(side_a, n_slabs, step_of)
    b_in, b_out, b_shape = _slab_specs(side_b, n_slabs, step_of)
    return pl.pallas_call(
        functools.partial(_merge_kernel, n_full=n_full, n_slabs=n_slabs),
        grid=(n_n, n_full + 1),
        in_specs=[pl.BlockSpec((tm, d), clamp),
                  pl.BlockSpec((tm, dc), clamp),
                  pl.BlockSpec((tm, dh), clamp),
                  pl.BlockSpec((tail, d), lambda j, i: (0, 0)),
                  pl.BlockSpec((tail, dc), lambda j, i: (0, 0)),
                  pl.BlockSpec((tail, dh), lambda j, i: (0, 0)),
                  pl.BlockSpec((d, tn), lambda j, i: (0, col_ga // tn + j), pipeline_mode=weights),
                  pl.BlockSpec((d, tn), lambda j, i: (0, col_ga // tn + n_n + j), pipeline_mode=weights),
                  pl.BlockSpec((dc, tn), lambda j, i: (0, j), pipeline_mode=weights),
                  pl.BlockSpec((dh, tn), lambda j, i: (0, j), pipeline_mode=weights),
                  a_in, b_in],
        out_specs=[pl.BlockSpec((tm, tn), lambda j, i: (i, j)), a_out, b_out],
        out_shape=[jax.ShapeDtypeStruct((m, d), BF16), a_shape, b_shape],
        compiler_params=_params(("arbitrary", "arbitrary"), VMEM_LIMIT_MLP),
        name="merge",
    )(hx, a, o, hx_tail, a_tail, o_tail, w_in, w_in, w_a, w_b, side_a, side_b)


def _outproj_kernel(xp_ref, xs_ref, mix_ref, w_ref, g_ref, x1_ref, h2_ref, *, n_full):
    i = pl.program_id(0)

    def body(rows, is_tail):
        x = xs_ref[...] if is_tail else xp_ref[...]
        x1 = x + _dot(mix_ref[0:rows, :], w_ref[...])
        x1_ref[0:rows, :] = x1
        h2_ref[0:rows, :] = _rms_scale(x1, g_ref[...]).astype(h2_ref.dtype)

    _row_split(i, n_full, xp_ref.shape[0], xs_ref.shape[0], body)


def _outproj(xp, xs, mix, w_out, g, tm):
    mp, d = xp.shape
    tail = xs.shape[0]
    n_full = mp // tm
    return pl.pallas_call(
        functools.partial(_outproj_kernel, n_full=n_full),
        grid=(n_full + 1,),
        in_specs=[pl.BlockSpec((tm, d), lambda i: (jnp.minimum(i, n_full - 1), 0)),
                  pl.BlockSpec((tail, d), lambda i: (0, 0)),
                  pl.BlockSpec((tm, d), lambda i: (i, 0)),
                  pl.BlockSpec((d, d), lambda i: (0, 0)),
                  pl.BlockSpec((1, d), lambda i: (0, 0))],
        out_specs=[pl.BlockSpec((tm, d), lambda i: (i, 0)),
                   pl.BlockSpec((tm, d), lambda i: (i, 0))],
        out_shape=[jax.ShapeDtypeStruct((mp + tail, d), F32),
                   jax.ShapeDtypeStruct((mp + tail, d), BF16)],
        compiler_params=_params(("arbitrary",)),
        name="outproj",
    )(xp, xs, mix, w_out, g.reshape(1, d))


def _mlp_kernel(h2_ref, x1_ref, wup_ref, wdn_ref, g_ref, yp_ref, ys_ref, *, n_full):
    i, j = pl.program_id(0), pl.program_id(1)
    last = pl.num_programs(1) - 1

    def body(rows, is_tail):
        y_ref = ys_ref if is_tail else yp_ref

        def step(first, final):
            halves = 2 if (final and not is_tail) else 1
            for hf in range(halves):
                rs = slice(hf * rows // halves, (hf + 1) * rows // halves)
                h = jnp.maximum(_dot(h2_ref[rs, :], wup_ref[...]), 0.0)
                acc = (x1_ref[rs, :] if first else y_ref[rs, :]) + _dot((h * h).astype(BF16), wdn_ref[...])
                y_ref[rs, :] = _rms_scale(acc, g_ref[...]) if final else acc

        @pl.when(j == 0)
        def _():
            step(True, False)

        @pl.when((j > 0) & (j < last))
        def _():
            step(False, False)

        @pl.when(j == last)
        def _():
            step(False, True)

    _row_split(i, n_full, yp_ref.shape[0], ys_ref.shape[0], body)


def _mlp(h2, x1, w_up, w_down, g, tm, tf, tail):
    m, d = h2.shape
    dff = w_up.shape[1]
    assert dff // tf >= 2
    n_full = (m - tail) // tm
    return pl.pallas_call(
        functools.partial(_mlp_kernel, n_full=n_full),
        grid=(n_full + 1, dff // tf),
        in_specs=[pl.BlockSpec((tm, d), lambda i, j: (i, 0)),
                  pl.BlockSpec((tm, d), lambda i, j: (i, 0)),
                  pl.BlockSpec((d, tf), lambda i, j: (0, j)),
                  pl.BlockSpec((tf, d), lambda i, j: (j, 0)),
                  pl.BlockSpec((1, d), lambda i, j: (0, 0))],
        out_specs=[pl.BlockSpec((tm, d), lambda i, j: (jnp.minimum(i, n_full - 1), 0)),
                   pl.BlockSpec((tail, d), lambda i, j: (0, 0))],
        out_shape=[jax.ShapeDtypeStruct((m - tail, d), F32),
                   jax.ShapeDtypeStruct((tail, d), F32)],
        compiler_params=_params(("arbitrary", "arbitrary"), VMEM_LIMIT_MLP),
        name="mlp",
    )(h2, x1, w_up, w_down, g.reshape(1, d))


def kernel(x_prompt, x_sample, state_conv, state_hgrn, norm_mix, w_in, conv_w, lb_logits, onorm_g,
           w_branch_a, w_branch_b, w_out, norm_ffn, w_up, w_down, norm_final):
    batch, seq, d = x_prompt.shape
    n_dec = x_sample.shape[0]
    depth, _, d_conv = conv_w.shape
    heads, dk = state_hgrn.shape[2], state_hgrn.shape[3]
    d_hgrn = heads * dk
    mp = batch * seq
    assert depth == 1 and x_sample.shape[1] == 1 and dk == HEAD_DIM and state_hgrn.shape[4] == dk
    assert seq % ROW_BLOCK == 0 and mp % MLP_ROW_BLOCK == 0 and ROW_BLOCK % n_dec == 0
    col_hgrn = 0
    col_ga = col_hgrn + 4 * d_hgrn
    w_in0 = w_in.reshape(w_in.shape[1:])
    w_a0 = w_branch_a.reshape(w_branch_a.shape[1:])
    w_b0 = w_branch_b.reshape(w_branch_b.shape[1:])
    w_out0 = w_out.reshape(w_out.shape[1:])
    w_up0 = w_up.reshape(w_up.shape[1:])
    w_down0 = w_down.reshape(w_down.shape[1:])

    xp = x_prompt.reshape(mp, d)
    xs = x_sample.reshape(n_dec, d)
    hx, a_p, conv_p, w_rest_b = _conv_prompt(xp, norm_mix[0], w_in0, conv_w[0], batch, seq, d_conv,
                                             ROW_BLOCK, CONV_COL_BLOCK)
    hx_s, a_s, conv_s = _conv_sample(xs, norm_mix[0], w_in0, conv_w[0], state_conv[0], d_conv, CONV_COL_BLOCK)

    p_s = _proj(hx_s, w_rest_b, col_hgrn, 4 * d_hgrn, PROJ_COL_BLOCK, dk)
    o_p, hgrn_p, o_s, hgrn_s, w_up_b, w_a_b, w_b_b = _hgrn_prompt(
        hx, w_rest_b, col_hgrn, lb_logits, onorm_g[0], batch, seq, heads, HGRN_ROWS, HGRN_HEADS, (w_up0, w_a0, w_b0),
        p_s, state_hgrn[0])

    mix, w_out_b, w_down_b = _merge(hx, a_p, o_p, hx_s, a_s, o_s, w_rest_b, w_a_b, w_b_b,
                                    col_ga, ROW_BLOCK, MERGE_COL_BLOCK, w_out0, w_down0, SIDE_SLABS // 4)
    x1, h2 = _outproj(xp, xs, mix, w_out_b, norm_ffn[0], ROW_BLOCK)
    y_p, y_s = _mlp(h2, x1, w_up_b, w_down_b, norm_final, MLP_ROW_BLOCK, MLP_FF_BLOCK, n_dec)

    return (y_p.reshape(batch, seq, d), y_s.reshape(n_dec, 1, d),
            conv_p[None], hgrn_p[None], conv_s[None], hgrn_s[None])
```

```python
import functools

import jax
import jax.numpy as jnp
from jax import lax
from jax.experimental import pallas as pl
from jax.experimental.pallas import tpu as pltpu

EPS = 1e-6
CHUNK = 64
HEAD_DIM = 128
V7X_VMEM_BYTES = 64 * 1024 * 1024
VMEM_LIMIT = V7X_VMEM_BYTES * 7 // 8
VMEM_LIMIT_MLP = V7X_VMEM_BYTES * 31 // 32

ROW_BLOCK = 512
MLP_ROW_BLOCK = 1024
MLP_FF_BLOCK = 1024
CONV_COL_BLOCK = 512
CONV_SUB_BLOCK = 256
PROJ_COL_BLOCK = 1024
MERGE_COL_BLOCK = 2048
MERGE_SUB_BLOCK = 1024
SIDE_SLABS = 64
HGRN_ROWS = 512
HGRN_SUB_ROWS = 256
HGRN_HEADS = 4

BF16 = jnp.bfloat16
F32 = jnp.float32


def _params(semantics, vmem_limit=VMEM_LIMIT):
    return pltpu.CompilerParams(dimension_semantics=semantics, vmem_limit_bytes=vmem_limit)


def _dot(a, b):
    return jnp.dot(a, b, preferred_element_type=F32)


def _dot_nt(a, b):
    return lax.dot_general(a, b, (((1,), (1,)), ((), ())), preferred_element_type=F32)


def _dot_tn(a, b):
    return lax.dot_general(a, b, (((0,), (0,)), ((), ())), preferred_element_type=F32)


def _sigmoid(x):
    return jax.nn.sigmoid(x)


def _silu(x):
    return x * jax.nn.sigmoid(x)


def _rms_scale(x, g):
    ms = jnp.mean(x * x, axis=-1, keepdims=True)
    return x * lax.rsqrt(ms + EPS) * g


def _row_split(i, n_full, full_rows, tail_rows, body):
    @pl.when(i < n_full)
    def _():
        body(full_rows, False)

    @pl.when(i == n_full)
    def _():
        body(tail_rows, True)


def _slab_cast(step, n_slabs, src_ref, dst_ref):
    @pl.when(step < n_slabs)
    def _():
        dst_ref[...] = src_ref[...].astype(dst_ref.dtype)


def _slab_specs(w, n_slabs, step_of):
    r, c = w.shape
    assert r % n_slabs == 0

    def index(*g):
        return (jnp.minimum(step_of(*g), n_slabs - 1), 0)

    spec = pl.BlockSpec((r // n_slabs, c), index)
    return spec, spec, jax.ShapeDtypeStruct((r, c), BF16)


def _conv_prompt_kernel(x_ref, g_ref, whc_ref, wbg_ref, wcg_ref, cw_ref, side_ref,
                        hx_out_ref, a_ref, nc_ref, side_out_ref,
                        h_ref, wb_ref, carry_ref, *, n_slabs, n_t, n_c):
    s = pl.program_id(0)
    tm, tc = a_ref.shape
    j = jnp.maximum(s - 1, 0)
    t = lax.rem(j // n_c, n_t)
    _slab_cast(s, n_slabs, side_ref, side_out_ref)

    def normalise(slot):
        hx = _rms_scale(x_ref[...], g_ref[...]).astype(BF16)
        h_ref[slot] = hx
        hx_out_ref[...] = hx
        return hx

    @pl.when(s == 0)
    def _():
        normalise(0)

    @pl.when((s >= 1) & (s <= n_c))
    def _():
        for c in range(n_c):
            @pl.when(s == c + 1)
            def _(c=c):
                wb_ref[c, 0] = whc_ref[...].astype(BF16)
                wb_ref[c, 1] = wbg_ref[...].astype(BF16)
                wb_ref[c, 2] = wcg_ref[...].astype(BF16)

    cw = cw_ref[...]
    row = lax.broadcasted_iota(jnp.int32, (tm, CONV_SUB_BLOCK), 0)

    def body(c, slot):
        @pl.when((t == 0) & (c == 0))
        def _():
            carry_ref[...] = jnp.zeros_like(carry_ref)

        last_channel = c == n_c - 1
        hx_next = normalise(1 - slot) if last_channel else None
        hx = h_ref[slot]
        n_sub = tc // CONV_SUB_BLOCK
        for sub in range(n_sub):
            cs = slice(sub * CONV_SUB_BLOCK, (sub + 1) * CONV_SUB_BLOCK)
            lhs = jnp.where(s < 0, hx_next, hx) if (last_channel and sub == n_sub - 1) else hx
            hc = _dot(lhs, wb_ref[c, 0, :, cs])
            bg = _dot(lhs, wb_ref[c, 1, :, cs])
            cg = _dot(lhs, wb_ref[c, 2, :, cs])
            u = cg * hc
            c0 = carry_ref[c, 0:1, cs]
            c1 = carry_ref[c, 1:2, cs]
            u1 = jnp.where(row == 0, c1, pltpu.roll(u, 1, 0))
            u2 = jnp.where(row == 0, c0, jnp.where(row == 1, c1, pltpu.roll(u, 2, 0)))
            conv = cw[0:1, cs] * u2 + cw[1:2, cs] * u1 + cw[2:3, cs] * u
            a_ref[:, cs] = (bg * conv).astype(a_ref.dtype)
            carry_ref[c, 0:2, cs] = u[tm - 2:tm, :]

        @pl.when(t == n_t - 1)
        def _():
            nc_ref[0, :, c * tc:(c + 1) * tc] = carry_ref[c, 0:2, :]

    phase = lax.rem(j, 2 * n_c)
    for m in range(2 * n_c):
        @pl.when((s >= 1) & (phase == m))
        def _(m=m):
            body(m % n_c, m // n_c)


def _conv_prompt(x, g, w_in, conv_w, batch, seq, d_conv, tm, tc):
    mp, d = x.shape
    n_c = d_conv // tc
    n_t = seq // tm
    rows_blocks = batch * n_t
    n_steps = n_c * rows_blocks
    conv = lambda s: jnp.maximum(s - 1, 0)
    norm = lambda s: jnp.minimum(s // n_c, rows_blocks - 1)
    col_rest = 3 * d_conv
    sw = tc * n_c
    n_col = (w_in.shape[1] - col_rest) // sw
    assert n_steps == 4 * n_col and col_rest % sw == 0
    tile = lambda s: jnp.minimum(s, n_steps - 1)
    side_in = pl.BlockSpec((d // 4, sw), lambda s: (tile(s) % 4, col_rest // sw + tile(s) // 4))
    side_out = pl.BlockSpec((d // 4, sw), lambda s: (tile(s) % 4, tile(s) // 4))
    side_shape = jax.ShapeDtypeStruct((d, w_in.shape[1] - col_rest), BF16)

    def wcol(k):
        return pl.BlockSpec((d, tc), lambda s: (0, k * n_c + jnp.minimum(conv(s), n_c - 1)),
                            pipeline_mode=pl.Buffered(1))

    return pl.pallas_call(
        functools.partial(_conv_prompt_kernel, n_slabs=n_steps, n_t=n_t, n_c=n_c),
        grid=(n_steps + 1,),
        in_specs=[pl.BlockSpec((tm, d), lambda s: (norm(s), 0)),
                  pl.BlockSpec((1, d), lambda s: (0, 0)),
                  wcol(0), wcol(1), wcol(2),
                  pl.BlockSpec((3, tc), lambda s: (0, conv(s) % n_c)),
                  side_in],
        out_specs=[pl.BlockSpec((tm, d), lambda s: (norm(s), 0)),
                   pl.BlockSpec((tm, tc), lambda s: (conv(s) // n_c, conv(s) % n_c)),
                   pl.BlockSpec((1, 2, d_conv), lambda s: (conv(s) // n_c // n_t, 0, 0)),
                   side_out],
        out_shape=[jax.ShapeDtypeStruct((mp, d), BF16),
                   jax.ShapeDtypeStruct((mp, d_conv), BF16),
                   jax.ShapeDtypeStruct((batch, 2, d_conv), F32),
                   side_shape],
        scratch_shapes=[pltpu.VMEM((2, tm, d), BF16), pltpu.VMEM((n_c, 3, d, tc), BF16),
                        pltpu.VMEM((n_c, 8, tc), F32)],
        compiler_params=_params(("arbitrary",)),
        name="conv_prompt",
    )(x, g.reshape(1, d), w_in, w_in, w_in, conv_w, w_in)


def _conv_sample_kernel(x_ref, g_ref, whc_ref, wbg_ref, wcg_ref, cw_ref, st_ref, hx_ref, a_ref, nc_ref):
    hx = _rms_scale(x_ref[...], g_ref[...]).astype(BF16)
    hx_ref[...] = hx
    hc = _dot(hx, whc_ref[...].astype(BF16))
    bg = _dot(hx, wbg_ref[...].astype(BF16))
    cg = _dot(hx, wcg_ref[...].astype(BF16))
    u = cg * hc
    cw = cw_ref[...]
    s1 = st_ref[:, 1, :]
    conv = cw[0:1, :] * st_ref[:, 0, :] + cw[1:2, :] * s1 + cw[2:3, :] * u
    a_ref[...] = (bg * conv).astype(a_ref.dtype)
    nc_ref[:, 0, :] = s1
    nc_ref[:, 1, :] = u


def _conv_sample(x, g, w_in, conv_w, state, d_conv, tc):
    n, d = x.shape
    n_c = d_conv // tc
    return pl.pallas_call(
        _conv_sample_kernel,
        grid=(n_c,),
        in_specs=[pl.BlockSpec((n, d), lambda c: (0, 0)),
                  pl.BlockSpec((1, d), lambda c: (0, 0)),
                  pl.BlockSpec((d, tc), lambda c: (0, c)),
                  pl.BlockSpec((d, tc), lambda c: (0, n_c + c)),
                  pl.BlockSpec((d, tc), lambda c: (0, 2 * n_c + c)),
                  pl.BlockSpec((3, tc), lambda c: (0, c)),
                  pl.BlockSpec((n, 2, tc), lambda c: (0, 0, c))],
        out_specs=[pl.BlockSpec((n, d), lambda c: (0, 0)),
                   pl.BlockSpec((n, tc), lambda c: (0, c)),
                   pl.BlockSpec((n, 2, tc), lambda c: (0, 0, c))],
        out_shape=[jax.ShapeDtypeStruct((n, d), BF16),
                   jax.ShapeDtypeStruct((n, d_conv), BF16),
                   jax.ShapeDtypeStruct((n, 2, d_conv), F32)],
        compiler_params=_params(("arbitrary",)),
        name="conv_sample",
    )(x, g.reshape(1, d), w_in, w_in, w_in, conv_w, state)


def _proj_kernel(x_ref, w_ref, o_ref):
    res = _dot(x_ref[...], w_ref[...])
    dk = o_ref.shape[2]
    for h in range(o_ref.shape[1]):
        o_ref[:, h, :] = res[:, h * dk:(h + 1) * dk]


def _proj(x, w, col0, n, tn, dk):
    rows, d = x.shape
    return pl.pallas_call(
        _proj_kernel,
        grid=(n // tn,),
        in_specs=[pl.BlockSpec((rows, d), lambda j: (0, 0)),
                  pl.BlockSpec((d, tn), lambda j: (0, col0 // tn + j))],
        out_specs=pl.BlockSpec((rows, tn // dk, dk), lambda j: (0, j, 0)),
        out_shape=jax.ShapeDtypeStruct((rows, n // dk, dk), F32),
        compiler_params=_params(("arbitrary",)),
        name="hgrn_proj_decode",
    )(x, w)


def _lower_bound(lbl):
    e = jnp.exp(lbl - jnp.max(lbl, axis=0))
    return e[0] / jnp.sum(e, axis=0)


def _head_norm_gate(o, g, og):
    return _rms_scale(o, g) * _silu(og)


def _cumsum_rows(x, tril_bf16):
    hi = x.astype(BF16)
    r1 = x - hi.astype(F32)
    mid = r1.astype(BF16)
    lo = (r1 - mid.astype(F32)).astype(BF16)
    n = x.shape[1]
    parts = _dot(tril_bf16, jnp.concatenate([hi, mid, lo], axis=1))
    return parts[:, 0:n] + parts[:, n:2 * n] + parts[:, 2 * n:3 * n]


def _hgrn_decode_step(p_ref, lb, g, s0_ref, o_ref, row0, sn_ref):
    bb, heads, dk = s0_ref.shape[0], s0_ref.shape[1], s0_ref.shape[2]
    head_row = lax.broadcasted_iota(jnp.int32, (heads, dk), 0)

    def column(row):
        return jnp.broadcast_to(row, (dk, dk)).T

    for i in range(bb):
        p = p_ref[i]
        qs = _silu(p[0:heads])
        f = lb + (1.0 - lb) * _sigmoid(p[heads:2 * heads])
        kk = 1.0 - f
        v = p[2 * heads:3 * heads]
        og = p[3 * heads:4 * heads]
        q_in = (qs * f).astype(BF16)
        o_inter = jnp.zeros((heads, dk), F32)
        for h in range(heads):
            s = s0_ref[i, h]
            sn_ref[i, h] = column(f[h:h + 1, :]) * s + column(kk[h:h + 1, :]) * v[h:h + 1, :]
            o_inter = o_inter + jnp.where(head_row == h, _dot(q_in, s.astype(BF16)), 0.0)
        o = jnp.sum(qs * kk, axis=-1, keepdims=True) * v + o_inter
        o = _head_norm_gate(o, g, og)
        o_ref[pl.ds(row0 + i, 1), :] = jnp.concatenate([o[h:h + 1, :] for h in range(heads)], axis=1)


def _hgrn_prompt_kernel(hx_ref, wq_ref, wf_ref, wi_ref, wo_ref, lbl_ref, g_ref, dp_ref, dlbl_ref, ds0_ref, *refs,
                        n_side, n_slabs, n_t):
    side_refs, refs = refs[:n_side], refs[n_side:]
    o_ref, s_ref, do_ref, dsn_ref = refs[:4]
    side_out_refs, (pa_ref, pb_ref, st_ref) = refs[4:4 + n_side], refs[4 + n_side:]
    s = pl.program_id(0)
    for side_ref, side_out_ref in zip(side_refs, side_out_refs):
        _slab_cast(s, n_slabs, side_ref, side_out_ref)
    tb = hx_ref.shape[0]
    heads_per_step = st_ref.shape[0]
    dk = HEAD_DIM
    sub = HGRN_SUB_ROWS
    n_sub = tb // sub
    n_chunks = sub // CHUNK
    t = lax.rem(jnp.maximum(s - 1, 0), n_t)

    @pl.when(s == 0)
    def _():
        pb_ref[...] = jnp.zeros_like(pb_ref)

    @pl.when(t == 0)
    def _():
        st_ref[...] = jnp.zeros_like(st_ref)

    lb_all = _lower_bound(lbl_ref[...])
    g = g_ref[...]
    row = lax.broadcasted_iota(jnp.int32, (sub, sub), 0)
    col = lax.broadcasted_iota(jnp.int32, (sub, sub), 1)
    shift = CHUNK.bit_length() - 1
    causal = (row >= col) & (jnp.right_shift(row, shift) == jnp.right_shift(col, shift))
    tril = causal.astype(BF16)

    def sub_block(p_next_ref, p_ref, sb):
        heads = range(heads_per_step)
        cols = [slice(hh * dk, (hh + 1) * dk) for hh in heads]
        rows = slice(sb * sub, (sb + 1) * sub)
        seqs = ds0_ref.shape[0] // n_sub
        own = pl.ds(sb * seqs, seqs)

        def project(k, w_ref):
            p_next_ref[k, rows] = _dot(hx_ref[rows, :], w_ref[...])

        qs, kk, lf = [], [], []
        for hh in heads:
            lb = lb_all[:, cols[hh]]
            f = lb + (1.0 - lb) * _sigmoid(p_ref[1, rows, cols[hh]])
            qs.append(_silu(p_ref[0, rows, cols[hh]]))
            kk.append(1.0 - f)
            lf.append(jnp.log(f))
        project(0, wq_ref)
        b = [_cumsum_rows(lf[hh], tril) for hh in heads]
        q_in, k_in, k_end, decay = [], [], [], []
        for hh in heads:
            b3 = b[hh].reshape(n_chunks, CHUNK, dk)
            b_last = b3[:, CHUNK - 1:CHUNK, :]
            b_end = jnp.broadcast_to(b_last, b3.shape).reshape(sub, dk)
            q_in.append((qs[hh] * jnp.exp(b[hh])).astype(BF16))
            k_in.append((kk[hh] * jnp.exp(-b[hh])).astype(BF16))
            k_end.append((kk[hh] * jnp.exp(b_end - b[hh])).astype(BF16))
            decay.append(jnp.exp(b_last))
        project(1, wf_ref)
        v = [p_ref[2, rows, cols[hh]].astype(BF16) for hh in heads]
        raw = [_dot_nt(q_in[hh], k_in[hh]) for hh in heads]
        delta = [[_dot_tn(v[hh][c * CHUNK:(c + 1) * CHUNK], k_end[hh][c * CHUNK:(c + 1) * CHUNK])
                  for c in range(n_chunks)] for hh in heads]
        scores = [jnp.where(causal, raw[hh], 0.0).astype(BF16) for hh in heads]
        starts = []
        for hh in heads:
            st = st_ref[hh]
            per_chunk = []
            for c in range(n_chunks):
                per_chunk.append(st.astype(BF16))
                st = decay[hh][c] * st + delta[hh][c]
            st_ref[hh] = st
            starts.append(per_chunk)
        project(2, wi_ref)
        row0 = jnp.minimum(s, n_slabs - 1) * ds0_ref.shape[0] + sb * seqs
        _hgrn_decode_step(dp_ref.at[own], _lower_bound(dlbl_ref[...]), g, ds0_ref.at[own], do_ref, row0,
                          dsn_ref.at[own])
        o = []
        for hh in heads:
            inter = [_dot_nt(q_in[hh][c * CHUNK:(c + 1) * CHUNK], starts[hh][c]) for c in range(n_chunks)]
            o.append(_dot(scores[hh], v[hh]) + jnp.concatenate(inter, axis=0))
        for hh in heads:
            o_ref[rows, cols[hh]] = _head_norm_gate(o[hh], g, p_ref[3, rows, cols[hh]]).astype(o_ref.dtype)
        project(3, wo_ref)

    def body(p_next_ref, p_ref):
        for sb in range(n_sub):
            sub_block(p_next_ref, p_ref, sb)

    parity = lax.rem(s, 2)

    @pl.when(parity == 0)
    def _():
        body(pa_ref, pb_ref)

    @pl.when(parity == 1)
    def _():
        body(pb_ref, pa_ref)

    @pl.when(t == n_t - 1)
    def _():
        for hh in range(heads_per_step):
            s_ref[0, hh] = st_ref[hh].T


def _hgrn_prompt(hx, w_in, col0, lb_logits, onorm_g, batch, seq, heads, tb, hp, side_ws, dec_p, dec_state):
    d = hx.shape[1]
    n_t = seq // tb
    n_h = heads // hp
    dk = HEAD_DIM
    wc = hp * dk
    assert CHUNK & (CHUNK - 1) == 0 and col0 % wc == 0
    rows_blocks = batch * n_t
    n_steps = n_h * rows_blocks
    proj = lambda s: jnp.minimum(s, n_steps - 1)
    rec = lambda s: jnp.maximum(s - 1, 0)
    sides = [_slab_specs(w, n_steps, lambda s: s) for w in side_ws]

    n_dec, dec_rows = dec_p.shape[0], dec_p.shape[1]
    assert n_dec % n_steps == 0
    db = n_dec // n_steps
    dec = lambda s: jnp.minimum(s, n_steps - 1)

    def wcol(k):
        return pl.BlockSpec((d, wc), lambda s: (0, col0 // wc + k * n_h + proj(s) // rows_blocks))

    return pl.pallas_call(
        functools.partial(_hgrn_prompt_kernel, n_side=len(sides), n_slabs=n_steps, n_t=n_t),
        grid=(n_steps + 1,),
        in_specs=[pl.BlockSpec((tb, d), lambda s: (proj(s) % rows_blocks, 0)),
                  wcol(0), wcol(1), wcol(2), wcol(3),
                  pl.BlockSpec((2, 1, wc), lambda s: (0, 0, rec(s) // rows_blocks)),
                  pl.BlockSpec((1, dk), lambda s: (0, 0)),
                  pl.BlockSpec((db, dec_rows, dk), lambda s: (dec(s), 0, 0)),
                  pl.BlockSpec((2, heads, dk), lambda s: (0, 0, 0)),
                  pl.BlockSpec((db, heads, dk, dk), lambda s: (dec(s), 0, 0, 0))]
        + [side[0] for side in sides],
        out_specs=[pl.BlockSpec((tb, wc), lambda s: (rec(s) % rows_blocks, rec(s) // rows_blocks)),
                   pl.BlockSpec((1, hp, dk, dk),
                                lambda s: ((rec(s) % rows_blocks) // n_t, rec(s) // rows_blocks, 0, 0)),
                   pl.BlockSpec((n_dec, heads * dk), lambda s: (0, 0)),
                   pl.BlockSpec((db, heads, dk, dk), lambda s: (dec(s), 0, 0, 0))]
        + [side[1] for side in sides],
        out_shape=[jax.ShapeDtypeStruct((batch * seq, heads * dk), BF16),
                   jax.ShapeDtypeStruct((batch, heads, dk, dk), F32),
                   jax.ShapeDtypeStruct((n_dec, heads * dk), F32),
                   jax.ShapeDtypeStruct((n_dec, heads, dk, dk), F32)]
        + [side[2] for side in sides],
        scratch_shapes=[pltpu.VMEM((4, tb, wc), F32), pltpu.VMEM((4, tb, wc), F32),
                        pltpu.VMEM((hp, dk, dk), F32)],
        compiler_params=_params(("arbitrary",)),
        name="hgrn_prompt",
    )(hx, w_in, w_in, w_in, w_in, lb_logits.reshape(2, 1, heads * dk), onorm_g.reshape(1, dk),
      dec_p, lb_logits.reshape(2, heads, dk), dec_state, *side_ws)


def _merge_kernel(hx_ref, a_ref, o_ref, ht_ref, at_ref, ot_ref, wga_ref, wgb_ref, wa_ref, wb_ref, side_a_ref,
                  side_b_ref, mix_ref, side_a_out_ref, side_b_out_ref, *, n_full, n_slabs):
    j, i = pl.program_id(0), pl.program_id(1)
    _slab_cast(j * pl.num_programs(1) + i, n_slabs, side_a_ref, side_a_out_ref)
    _slab_cast(j * pl.num_programs(1) + i, n_slabs, side_b_ref, side_b_out_ref)

    def body(rows, is_tail):
        hx = ht_ref[...] if is_tail else hx_ref[...]
        a = at_ref[...] if is_tail else a_ref[...]
        o = ot_ref[...].astype(BF16) if is_tail else o_ref[...]
        for sub in range(mix_ref.shape[1] // MERGE_SUB_BLOCK):
            cs = slice(sub * MERGE_SUB_BLOCK, (sub + 1) * MERGE_SUB_BLOCK)
            ga = _sigmoid(_dot(hx, wga_ref[:, cs]))
            gb = _sigmoid(_dot(hx, wgb_ref[:, cs]))
            mix = ga * _dot(a, wa_ref[:, cs]) + gb * _dot(o, wb_ref[:, cs])
            mix_ref[0:rows, cs] = mix.astype(mix_ref.dtype)

    _row_split(i, n_full, hx_ref.shape[0], ht_ref.shape[0], body)


def _merge(hx, a, o, hx_tail, a_tail, o_tail, w_in, w_a, w_b, col_ga, tm, tn, side_a, side_b, n_slabs):
    d = hx.shape[1]
    m = a.shape[0] + hx_tail.shape[0]
    dc, dh = a.shape[1], o.shape[1]
    tail = a_tail.shape[0]
    n_full = a.shape[0] // tm
    n_n = d // tn
    clamp = lambda j, i: (jnp.minimum(i, n_full - 1), 0)
    assert n_slabs <= n_n * (n_full + 1)
    step_of = lambda j, i: j * (n_full + 1) + i
    weights = pl.Buffered(1) if n_n == 1 else None
    a_in, a_out, a_shape = _slab_specs(side_a, n_slabs, step_of)
    b_in, b_out, b_shape = _slab_specs(side_b, n_slabs, step_of)
    return pl.pallas_call(
        functools.partial(_merge_kernel, n_full=n_full, n_slabs=n_slabs),
        grid=(n_n, n_full + 1),
        in_specs=[pl.BlockSpec((tm, d), clamp),
                  pl.BlockSpec((tm, dc), clamp),
                  pl.BlockSpec((tm, dh), clamp),
                  pl.BlockSpec((tail, d), lambda j, i: (0, 0)),
                  pl.BlockSpec((tail, dc), lambda j, i: (0, 0)),
                  pl.BlockSpec((tail, dh), lambda j, i: (0, 0)),
                  pl.BlockSpec((d, tn), lambda j, i: (0, col_ga // tn + j), pipeline_mode=weights),
                  pl.BlockSpec((d, tn), lambda j, i: (0, col_ga // tn + n_n + j), pipeline_mode=weights),
                  pl.BlockSpec((dc, tn), lambda j, i: (0, j), pipeline_mode=weights),
                  pl.BlockSpec((dh, tn), lambda j, i: (0, j), pipeline_mode=weights),
                  a_in, b_in],
        out_specs=[pl.BlockSpec((tm, tn), lambda j, i: (i, j)), a_out, b_out],
        out_shape=[jax.ShapeDtypeStruct((m, d), BF16), a_shape, b_shape],
        compiler_params=_params(("arbitrary", "arbitrary"), VMEM_LIMIT_MLP),
        name="merge",
    )(hx, a, o, hx_tail, a_tail, o_tail, w_in, w_in, w_a, w_b, side_a, side_b)


def _mlp_kernel(mix_ref, xp_hbm_ref, xs_hbm_ref, wout_ref, gf_ref, wup_ref, wdn_ref, g_ref, yp_ref, ys_ref,
                h2_ref, sem_ref, *, n_full):
    i, j = pl.program_id(0), pl.program_id(1)
    last = pl.num_programs(1) - 1

    def body(rows, is_tail):
        y_ref = ys_ref if is_tail else yp_ref
        src = xs_hbm_ref if is_tail else xp_hbm_ref.at[pl.ds(pl.multiple_of(i * rows, rows), rows), :]
        residual = pltpu.make_async_copy(src, y_ref, sem_ref)

        def out_projection():
            halves = 1 if is_tail else 2
            half = rows // halves
            residual.start()
            first = _dot(mix_ref[0:half, :], wout_ref[...])
            residual.wait()
            for hf in range(halves):
                rs = slice(hf * half, (hf + 1) * half)
                proj = first if hf == 0 else _dot(mix_ref[rs, :], wout_ref[...])
                x1 = y_ref[rs, :] + proj
                y_ref[rs, :] = x1
                h2_ref[rs, :] = _rms_scale(x1, gf_ref[...]).astype(h2_ref.dtype)

        def step(final):
            halves = 2 if (final and not is_tail) else 1
            for hf in range(halves):
                rs = slice(hf * rows // halves, (hf + 1) * rows // halves)
                h = jnp.maximum(_dot(h2_ref[rs, :], wup_ref[...]), 0.0)
                acc = y_ref[rs, :] + _dot((h * h).astype(BF16), wdn_ref[...])
                y_ref[rs, :] = _rms_scale(acc, g_ref[...]) if final else acc

        @pl.when(j == 0)
        def _():
            out_projection()
            step(False)

        @pl.when((j > 0) & (j < last))
        def _():
            step(False)

        @pl.when(j == last)
        def _():
            step(True)

    _row_split(i, n_full, yp_ref.shape[0], ys_ref.shape[0], body)


def _mlp(mix, xp, xs, w_out, g_ffn, w_up, w_down, g, tm, tf):
    m, d = mix.shape
    tail = xs.shape[0]
    dff = w_up.shape[1]
    assert dff // tf >= 2
    n_full = (m - tail) // tm
    return pl.pallas_call(
        functools.partial(_mlp_kernel, n_full=n_full),
        grid=(n_full + 1, dff // tf),
        in_specs=[pl.BlockSpec((tm, d), lambda i, j: (i, 0)),
                  pl.BlockSpec(memory_space=pl.ANY),
                  pl.BlockSpec(memory_space=pl.ANY),
                  pl.BlockSpec((d, d), lambda i, j: (0, 0), pipeline_mode=pl.Buffered(1)),
                  pl.BlockSpec((1, d), lambda i, j: (0, 0)),
                  pl.BlockSpec((d, tf), lambda i, j: (0, j)),
                  pl.BlockSpec((tf, d), lambda i, j: (j, 0)),
                  pl.BlockSpec((1, d), lambda i, j: (0, 0))],
        out_specs=[pl.BlockSpec((tm, d), lambda i, j: (jnp.minimum(i, n_full - 1), 0)),
                   pl.BlockSpec((tail, d), lambda i, j: (0, 0))],
        out_shape=[jax.ShapeDtypeStruct((m - tail, d), F32),
                   jax.ShapeDtypeStruct((tail, d), F32)],
        scratch_shapes=[pltpu.VMEM((tm, d), BF16), pltpu.SemaphoreType.DMA],
        compiler_params=_params(("arbitrary", "arbitrary"), VMEM_LIMIT_MLP),
        name="mlp",
    )(mix, xp, xs, w_out, g_ffn.reshape(1, d), w_up, w_down, g.reshape(1, d))


def kernel(x_prompt, x_sample, state_conv, state_hgrn, norm_mix, w_in, conv_w, lb_logits, onorm_g,
           w_branch_a, w_branch_b, w_out, norm_ffn, w_up, w_down, norm_final):
    batch, seq, d = x_prompt.shape
    n_dec = x_sample.shape[0]
    depth, _, d_conv = conv_w.shape
    heads, dk = state_hgrn.shape[2], state_hgrn.shape[3]
    d_hgrn = heads * dk
    mp = batch * seq
    assert depth == 1 and x_sample.shape[1] == 1 and dk == HEAD_DIM and state_hgrn.shape[4] == dk
    assert seq % ROW_BLOCK == 0 and mp % MLP_ROW_BLOCK == 0 and ROW_BLOCK % n_dec == 0
    col_hgrn = 0
    col_ga = col_hgrn + 4 * d_hgrn
    w_in0 = w_in.reshape(w_in.shape[1:])
    w_a0 = w_branch_a.reshape(w_branch_a.shape[1:])
    w_b0 = w_branch_b.reshape(w_branch_b.shape[1:])
    w_out0 = w_out.reshape(w_out.shape[1:])
    w_up0 = w_up.reshape(w_up.shape[1:])
    w_down0 = w_down.reshape(w_down.shape[1:])

    xp = x_prompt.reshape(mp, d)
    xs = x_sample.reshape(n_dec, d)
    hx, a_p, conv_p, w_rest_b = _conv_prompt(xp, norm_mix[0], w_in0, conv_w[0], batch, seq, d_conv,
                                             ROW_BLOCK, CONV_COL_BLOCK)
    hx_s, a_s, conv_s = _conv_sample(xs, norm_mix[0], w_in0, conv_w[0], state_conv[0], d_conv, CONV_COL_BLOCK)

    p_s = _proj(hx_s, w_rest_b, col_hgrn, 4 * d_hgrn, PROJ_COL_BLOCK, dk)
    o_p, hgrn_p, o_s, hgrn_s, w_up_b, w_a_b, w_b_b = _hgrn_prompt(
        hx, w_rest_b, col_hgrn, lb_logits, onorm_g[0], batch, seq, heads, HGRN_ROWS, HGRN_HEADS, (w_up0, w_a0, w_b0),
        p_s, state_hgrn[0])

    mix, w_out_b, w_down_b = _merge(hx, a_p, o_p, hx_s, a_s, o_s, w_rest_b, w_a_b, w_b_b,
                                    col_ga, ROW_BLOCK, MERGE_COL_BLOCK, w_out0, w_down0, SIDE_SLABS // 4)
    y_p, y_s = _mlp(mix, xp, xs, w_out_b, norm_ffn[0], w_up_b, w_down_b, norm_final, MLP_ROW_BLOCK, MLP_FF_BLOCK)

    return (y_p.reshape(batch, seq, d), y_s.reshape(n_dec, 1, d),
            conv_p[None], hgrn_p[None], conv_s[None], hgrn_s[None])
```

```python
import functools

import jax
import jax.numpy as jnp
from jax import lax
from jax.experimental import pallas as pl
from jax.experimental.pallas import tpu as pltpu

EPS = 1e-6
CHUNK = 64
HEAD_DIM = 128
V7X_VMEM_BYTES = 64 * 1024 * 1024
VMEM_LIMIT = V7X_VMEM_BYTES * 7 // 8
VMEM_LIMIT_MLP = V7X_VMEM_BYTES * 31 // 32

ROW_BLOCK = 512
MLP_ROW_BLOCK = 512
MLP_FF_BLOCK = 2048
CONV_COL_BLOCK = 512
CONV_SUB_BLOCK = 256
PROJ_COL_BLOCK = 1024
MERGE_COL_BLOCK = 2048
MERGE_SUB_BLOCK = 1024
SIDE_SLABS = 64
HGRN_ROWS = 512
HGRN_SUB_ROWS = 256
HGRN_HEADS = 4

BF16 = jnp.bfloat16
F32 = jnp.float32


def _params(semantics, vmem_limit=VMEM_LIMIT):
    return pltpu.CompilerParams(dimension_semantics=semantics, vmem_limit_bytes=vmem_limit)


def _dot(a, b):
    return jnp.dot(a, b, preferred_element_type=F32)


def _dot_nt(a, b):
    return lax.dot_general(a, b, (((1,), (1,)), ((), ())), preferred_element_type=F32)


def _dot_tn(a, b):
    return lax.dot_general(a, b, (((0,), (0,)), ((), ())), preferred_element_type=F32)


def _sigmoid(x):
    return jax.nn.sigmoid(x)


def _silu(x):
    return x * jax.nn.sigmoid(x)


def _rms_scale(x, g):
    ms = jnp.mean(x * x, axis=-1, keepdims=True)
    return x * lax.rsqrt(ms + EPS) * g


def _row_split(i, n_full, full_rows, tail_rows, body):
    @pl.when(i < n_full)
    def _():
        body(full_rows, False)

    @pl.when(i == n_full)
    def _():
        body(tail_rows, True)


def _slab_cast(step, n_slabs, src_ref, dst_ref):
    @pl.when(step < n_slabs)
    def _():
        dst_ref[...] = src_ref[...].astype(dst_ref.dtype)


def _slab_specs(w, n_slabs, step_of):
    r, c = w.shape
    assert r % n_slabs == 0

    def index(*g):
        return (jnp.minimum(step_of(*g), n_slabs - 1), 0)

    spec = pl.BlockSpec((r // n_slabs, c), index)
    return spec, spec, jax.ShapeDtypeStruct((r, c), BF16)


def _conv_prompt_kernel(x_ref, g_ref, whc_ref, wbg_ref, wcg_ref, cw_ref, side_ref,
                        hx_out_ref, a_ref, nc_ref, side_out_ref,
                        h_ref, wb_ref, carry_ref, *, n_slabs, n_t, n_c):
    s = pl.program_id(0)
    tm, tc = a_ref.shape
    j = jnp.maximum(s - 1, 0)
    t = lax.rem(j // n_c, n_t)
    _slab_cast(s, n_slabs, side_ref, side_out_ref)

    def normalise(slot):
        hx = _rms_scale(x_ref[...], g_ref[...]).astype(BF16)
        h_ref[slot] = hx
        hx_out_ref[...] = hx
        return hx

    @pl.when(s == 0)
    def _():
        normalise(0)

    @pl.when((s >= 1) & (s <= n_c))
    def _():
        for c in range(n_c):
            @pl.when(s == c + 1)
            def _(c=c):
                wb_ref[c, 0] = whc_ref[...].astype(BF16)
                wb_ref[c, 1] = wbg_ref[...].astype(BF16)
                wb_ref[c, 2] = wcg_ref[...].astype(BF16)

    cw = cw_ref[...]
    row = lax.broadcasted_iota(jnp.int32, (tm, CONV_SUB_BLOCK), 0)

    def body(c, slot):
        @pl.when((t == 0) & (c == 0))
        def _():
            carry_ref[...] = jnp.zeros_like(carry_ref)

        last_channel = c == n_c - 1
        hx_next = normalise(1 - slot) if last_channel else None
        hx = h_ref[slot]
        n_sub = tc // CONV_SUB_BLOCK
        for sub in range(n_sub):
            cs = slice(sub * CONV_SUB_BLOCK, (sub + 1) * CONV_SUB_BLOCK)
            lhs = jnp.where(s < 0, hx_next, hx) if (last_channel and sub == n_sub - 1) else hx
            hc = _dot(lhs, wb_ref[c, 0, :, cs])
            bg = _dot(lhs, wb_ref[c, 1, :, cs])
            cg = _dot(lhs, wb_ref[c, 2, :, cs])
            u = cg * hc
            c0 = carry_ref[c, 0:1, cs]
            c1 = carry_ref[c, 1:2, cs]
            u1 = jnp.where(row == 0, c1, pltpu.roll(u, 1, 0))
            u2 = jnp.where(row == 0, c0, jnp.where(row == 1, c1, pltpu.roll(u, 2, 0)))
            conv = cw[0:1, cs] * u2 + cw[1:2, cs] * u1 + cw[2:3, cs] * u
            a_ref[:, cs] = (bg * conv).astype(a_ref.dtype)
            carry_ref[c, 0:2, cs] = u[tm - 2:tm, :]

        @pl.when(t == n_t - 1)
        def _():
            nc_ref[0, :, c * tc:(c + 1) * tc] = carry_ref[c, 0:2, :]

    phase = lax.rem(j, 2 * n_c)
    for m in range(2 * n_c):
        @pl.when((s >= 1) & (phase == m))
        def _(m=m):
            body(m % n_c, m // n_c)


def _conv_prompt(x, g, w_in, conv_w, batch, seq, d_conv, tm, tc):
    mp, d = x.shape
    n_c = d_conv // tc
    n_t = seq // tm
    rows_blocks = batch * n_t
    n_steps = n_c * rows_blocks
    conv = lambda s: jnp.maximum(s - 1, 0)
    norm = lambda s: jnp.minimum(s // n_c, rows_blocks - 1)
    col_rest = 3 * d_conv
    sw = tc * n_c
    n_col = (w_in.shape[1] - col_rest) // sw
    assert n_steps == 4 * n_col and col_rest % sw == 0
    tile = lambda s: jnp.minimum(s, n_steps - 1)
    side_in = pl.BlockSpec((d // 4, sw), lambda s: (tile(s) % 4, col_rest // sw + tile(s) // 4))
    side_out = pl.BlockSpec((d // 4, sw), lambda s: (tile(s) % 4, tile(s) // 4))
    side_shape = jax.ShapeDtypeStruct((d, w_in.shape[1] - col_rest), BF16)

    def wcol(k):
        return pl.BlockSpec((d, tc), lambda s: (0, k * n_c + jnp.minimum(conv(s), n_c - 1)),
                            pipeline_mode=pl.Buffered(1))

    return pl.pallas_call(
        functools.partial(_conv_prompt_kernel, n_slabs=n_steps, n_t=n_t, n_c=n_c),
        grid=(n_steps + 1,),
        in_specs=[pl.BlockSpec((tm, d), lambda s: (norm(s), 0)),
                  pl.BlockSpec((1, d), lambda s: (0, 0)),
                  wcol(0), wcol(1), wcol(2),
                  pl.BlockSpec((3, tc), lambda s: (0, conv(s) % n_c)),
                  side_in],
        out_specs=[pl.BlockSpec((tm, d), lambda s: (norm(s), 0)),
                   pl.BlockSpec((tm, tc), lambda s: (conv(s) // n_c, conv(s) % n_c)),
                   pl.BlockSpec((1, 2, d_conv), lambda s: (conv(s) // n_c // n_t, 0, 0)),
                   side_out],
        out_shape=[jax.ShapeDtypeStruct((mp, d), BF16),
                   jax.ShapeDtypeStruct((mp, d_conv), BF16),
                   jax.ShapeDtypeStruct((batch, 2, d_conv), F32),
                   side_shape],
        scratch_shapes=[pltpu.VMEM((2, tm, d), BF16), pltpu.VMEM((n_c, 3, d, tc), BF16),
                        pltpu.VMEM((n_c, 8, tc), F32)],
        compiler_params=_params(("arbitrary",)),
        name="conv_prompt",
    )(x, g.reshape(1, d), w_in, w_in, w_in, conv_w, w_in)


def _conv_sample_kernel(x_ref, g_ref, whc_ref, wbg_ref, wcg_ref, cw_ref, st_ref, hx_ref, a_ref, nc_ref):
    hx = _rms_scale(x_ref[...], g_ref[...]).astype(BF16)
    hx_ref[...] = hx
    hc = _dot(hx, whc_ref[...].astype(BF16))
    bg = _dot(hx, wbg_ref[...].astype(BF16))
    cg = _dot(hx, wcg_ref[...].astype(BF16))
    u = cg * hc
    cw = cw_ref[...]
    s1 = st_ref[:, 1, :]
    conv = cw[0:1, :] * st_ref[:, 0, :] + cw[1:2, :] * s1 + cw[2:3, :] * u
    a_ref[...] = (bg * conv).astype(a_ref.dtype)
    nc_ref[:, 0, :] = s1
    nc_ref[:, 1, :] = u


def _conv_sample(x, g, w_in, conv_w, state, d_conv, tc):
    n, d = x.shape
    n_c = d_conv // tc
    return pl.pallas_call(
        _conv_sample_kernel,
        grid=(n_c,),
        in_specs=[pl.BlockSpec((n, d), lambda c: (0, 0)),
                  pl.BlockSpec((1, d), lambda c: (0, 0)),
                  pl.BlockSpec((d, tc), lambda c: (0, c)),
                  pl.BlockSpec((d, tc), lambda c: (0, n_c + c)),
                  pl.BlockSpec((d, tc), lambda c: (0, 2 * n_c + c)),
                  pl.BlockSpec((3, tc), lambda c: (0, c)),
                  pl.BlockSpec((n, 2, tc), lambda c: (0, 0, c))],
        out_specs=[pl.BlockSpec((n, d), lambda c: (0, 0)),
                   pl.BlockSpec((n, tc), lambda c: (0, c)),
                   pl.BlockSpec((n, 2, tc), lambda c: (0, 0, c))],
        out_shape=[jax.ShapeDtypeStruct((n, d), BF16),
                   jax.ShapeDtypeStruct((n, d_conv), BF16),
                   jax.ShapeDtypeStruct((n, 2, d_conv), F32)],
        compiler_params=_params(("arbitrary",)),
        name="conv_sample",
    )(x, g.reshape(1, d), w_in, w_in, w_in, conv_w, state)


def _proj_kernel(x_ref, w_ref, o_ref):
    res = _dot(x_ref[...], w_ref[...])
    dk = o_ref.shape[2]
    for h in range(o_ref.shape[1]):
        o_ref[:, h, :] = res[:, h * dk:(h + 1) * dk]


def _proj(x, w, col0, n, tn, dk):
    rows, d = x.shape
    return pl.pallas_call(
        _proj_kernel,
        grid=(n // tn,),
        in_specs=[pl.BlockSpec((rows, d), lambda j: (0, 0)),
                  pl.BlockSpec((d, tn), lambda j: (0, col0 // tn + j))],
        out_specs=pl.BlockSpec((rows, tn // dk, dk), lambda j: (0, j, 0)),
        out_shape=jax.ShapeDtypeStruct((rows, n // dk, dk), F32),
        compiler_params=_params(("arbitrary",)),
        name="hgrn_proj_decode",
    )(x, w)


def _lower_bound(lbl):
    e = jnp.exp(lbl - jnp.max(lbl, axis=0))
    return e[0] / jnp.sum(e, axis=0)


def _head_norm_gate(o, g, og):
    return _rms_scale(o, g) * _silu(og)


def _cumsum_rows(x, tril_bf16):
    hi = x.astype(BF16)
    r1 = x - hi.astype(F32)
    mid = r1.astype(BF16)
    lo = (r1 - mid.astype(F32)).astype(BF16)
    n = x.shape[1]
    parts = _dot(tril_bf16, jnp.concatenate([hi, mid, lo], axis=1))
    return parts[:, 0:n] + parts[:, n:2 * n] + parts[:, 2 * n:3 * n]


def _hgrn_decode_step(p_ref, lb, g, s0_ref, o_ref, row0, sn_ref):
    bb, heads, dk = s0_ref.shape[0], s0_ref.shape[1], s0_ref.shape[2]
    head_row = lax.broadcasted_iota(jnp.int32, (heads, dk), 0)

    def column(row):
        return jnp.broadcast_to(row, (dk, dk)).T

    for i in range(bb):
        p = p_ref[i]
        qs = _silu(p[0:heads])
        f = lb + (1.0 - lb) * _sigmoid(p[heads:2 * heads])
        kk = 1.0 - f
        v = p[2 * heads:3 * heads]
        og = p[3 * heads:4 * heads]
        q_in = (qs * f).astype(BF16)
        o_inter = jnp.zeros((heads, dk), F32)
        for h in range(heads):
            s = s0_ref[i, h]
            sn_ref[i, h] = column(f[h:h + 1, :]) * s + column(kk[h:h + 1, :]) * v[h:h + 1, :]
            o_inter = o_inter + jnp.where(head_row == h, _dot(q_in, s.astype(BF16)), 0.0)
        o = jnp.sum(qs * kk, axis=-1, keepdims=True) * v + o_inter
        o = _head_norm_gate(o, g, og)
        o_ref[pl.ds(row0 + i, 1), :] = jnp.concatenate([o[h:h + 1, :] for h in range(heads)], axis=1)


def _hgrn_prompt_kernel(hx_ref, wq_ref, wf_ref, wi_ref, wo_ref, lbl_ref, g_ref, dp_ref, dlbl_ref, ds0_ref, *refs,
                        n_side, n_slabs, n_t):
    side_refs, refs = refs[:n_side], refs[n_side:]
    o_ref, s_ref, do_ref, dsn_ref = refs[:4]
    side_out_refs, (pa_ref, pb_ref, st_ref) = refs[4:4 + n_side], refs[4 + n_side:]
    s = pl.program_id(0)
    for side_ref, side_out_ref in zip(side_refs, side_out_refs):
        _slab_cast(s, n_slabs, side_ref, side_out_ref)
    tb = hx_ref.shape[0]
    heads_per_step = st_ref.shape[0]
    dk = HEAD_DIM
    sub = HGRN_SUB_ROWS
    n_sub = tb // sub
    n_chunks = sub // CHUNK
    t = lax.rem(jnp.maximum(s - 1, 0), n_t)

    @pl.when(s == 0)
    def _():
        pb_ref[...] = jnp.zeros_like(pb_ref)

    @pl.when(t == 0)
    def _():
        st_ref[...] = jnp.zeros_like(st_ref)

    lb_all = _lower_bound(lbl_ref[...])
    g = g_ref[...]
    row = lax.broadcasted_iota(jnp.int32, (sub, sub), 0)
    col = lax.broadcasted_iota(jnp.int32, (sub, sub), 1)
    shift = CHUNK.bit_length() - 1
    causal = (row >= col) & (jnp.right_shift(row, shift) == jnp.right_shift(col, shift))
    tril = causal.astype(BF16)

    def sub_block(p_next_ref, p_ref, sb):
        heads = range(heads_per_step)
        cols = [slice(hh * dk, (hh + 1) * dk) for hh in heads]
        rows = slice(sb * sub, (sb + 1) * sub)
        seqs = ds0_ref.shape[0] // n_sub
        own = pl.ds(sb * seqs, seqs)

        def project(k, w_ref):
            p_next_ref[k, rows] = _dot(hx_ref[rows, :], w_ref[...])

        qs, kk, lf = [], [], []
        for hh in heads:
            lb = lb_all[:, cols[hh]]
            f = lb + (1.0 - lb) * _sigmoid(p_ref[1, rows, cols[hh]])
            qs.append(_silu(p_ref[0, rows, cols[hh]]))
            kk.append(1.0 - f)
            lf.append(jnp.log(f))
        project(0, wq_ref)
        b = [_cumsum_rows(lf[hh], tril) for hh in heads]
        q_in, k_in, k_end, decay = [], [], [], []
        for hh in heads:
            b3 = b[hh].reshape(n_chunks, CHUNK, dk)
            b_last = b3[:, CHUNK - 1:CHUNK, :]
            b_end = jnp.broadcast_to(b_last, b3.shape).reshape(sub, dk)
            q_in.append((qs[hh] * jnp.exp(b[hh])).astype(BF16))
            k_in.append((kk[hh] * jnp.exp(-b[hh])).astype(BF16))
            k_end.append((kk[hh] * jnp.exp(b_end - b[hh])).astype(BF16))
            decay.append(jnp.exp(b_last))
        project(1, wf_ref)
        v = [p_ref[2, rows, cols[hh]].astype(BF16) for hh in heads]
        raw = [_dot_nt(q_in[hh], k_in[hh]) for hh in heads]
        delta = [[_dot_tn(v[hh][c * CHUNK:(c + 1) * CHUNK], k_end[hh][c * CHUNK:(c + 1) * CHUNK])
                  for c in range(n_chunks)] for hh in heads]
        scores = [jnp.where(causal, raw[hh], 0.0).astype(BF16) for hh in heads]
        starts = []
        for hh in heads:
            st = st_ref[hh]
            per_chunk = []
            for c in range(n_chunks):
                per_chunk.append(st.astype(BF16))
                st = decay[hh][c] * st + delta[hh][c]
            st_ref[hh] = st
            starts.append(per_chunk)
        project(2, wi_ref)
        row0 = jnp.minimum(s, n_slabs - 1) * ds0_ref.shape[0] + sb * seqs
        _hgrn_decode_step(dp_ref.at[own], _lower_bound(dlbl_ref[...]), g, ds0_ref.at[own], do_ref, row0,
                          dsn_ref.at[own])
        o = []
        for hh in heads:
            inter = [_dot_nt(q_in[hh][c * CHUNK:(c + 1) * CHUNK], starts[hh][c]) for c in range(n_chunks)]
            o.append(_dot(scores[hh], v[hh]) + jnp.concatenate(inter, axis=0))
        for hh in heads:
            o_ref[rows, cols[hh]] = _head_norm_gate(o[hh], g, p_ref[3, rows, cols[hh]]).astype(o_ref.dtype)
        project(3, wo_ref)

    def body(p_next_ref, p_ref):
        for sb in range(n_sub):
            sub_block(p_next_ref, p_ref, sb)

    parity = lax.rem(s, 2)

    @pl.when(parity == 0)
    def _():
        body(pa_ref, pb_ref)

    @pl.when(parity == 1)
    def _():
        body(pb_ref, pa_ref)

    @pl.when(t == n_t - 1)
    def _():
        for hh in range(heads_per_step):
            s_ref[0, hh] = st_ref[hh].T


def _hgrn_prompt(hx, w_in, col0, lb_logits, onorm_g, batch, seq, heads, tb, hp, side_ws, dec_p, dec_state):
    d = hx.shape[1]
    n_t = seq // tb
    n_h = heads // hp
    dk = HEAD_DIM
    wc = hp * dk
    assert CHUNK & (CHUNK - 1) == 0 and col0 % wc == 0
    rows_blocks = batch * n_t
    n_steps = n_h * rows_blocks
    proj = lambda s: jnp.minimum(s, n_steps - 1)
    rec = lambda s: jnp.maximum(s - 1, 0)
    sides = [_slab_specs(w, n_steps, lambda s: s) for w in side_ws]

    n_dec, dec_rows = dec_p.shape[0], dec_p.shape[1]
    assert n_dec % n_steps == 0
    db = n_dec // n_steps
    dec = lambda s: jnp.minimum(s, n_steps - 1)

    def wcol(k):
        return pl.BlockSpec((d, wc), lambda s: (0, col0 // wc + k * n_h + proj(s) // rows_blocks))

    return pl.pallas_call(
        functools.partial(_hgrn_prompt_kernel, n_side=len(sides), n_slabs=n_steps, n_t=n_t),
        grid=(n_steps + 1,),
        in_specs=[pl.BlockSpec((tb, d), lambda s: (proj(s) % rows_blocks, 0)),
                  wcol(0), wcol(1), wcol(2), wcol(3),
                  pl.BlockSpec((2, 1, wc), lambda s: (0, 0, rec(s) // rows_blocks)),
                  pl.BlockSpec((1, dk), lambda s: (0, 0)),
                  pl.BlockSpec((db, dec_rows, dk), lambda s: (dec(s), 0, 0)),
                  pl.BlockSpec((2, heads, dk), lambda s: (0, 0, 0)),
                  pl.BlockSpec((db, heads, dk, dk), lambda s: (dec(s), 0, 0, 0))]
        + [side[0] for side in sides],
        out_specs=[pl.BlockSpec((tb, wc), lambda s: (rec(s) % rows_blocks, rec(s) // rows_blocks)),
                   pl.BlockSpec((1, hp, dk, dk),
                                lambda s: ((rec(s) % rows_blocks) // n_t, rec(s) // rows_blocks, 0, 0)),
                   pl.BlockSpec((n_dec, heads * dk), lambda s: (0, 0)),
                   pl.BlockSpec((db, heads, dk, dk), lambda s: (dec(s), 0, 0, 0))]
        + [side[1] for side in sides],
        out_shape=[jax.ShapeDtypeStruct((batch * seq, heads * dk), BF16),
                   jax.ShapeDtypeStruct((batch, heads, dk, dk), F32),
                   jax.ShapeDtypeStruct((n_dec, heads * dk), F32),
                   jax.ShapeDtypeStruct((n_dec, heads, dk, dk), F32)]
        + [side[2] for side in sides],
        scratch_shapes=[pltpu.VMEM((4, tb, wc), F32), pltpu.VMEM((4, tb, wc), F32),
                        pltpu.VMEM((hp, dk, dk), F32)],
        compiler_params=_params(("arbitrary",)),
        name="hgrn_prompt",
    )(hx, w_in, w_in, w_in, w_in, lb_logits.reshape(2, 1, heads * dk), onorm_g.reshape(1, dk),
      dec_p, lb_logits.reshape(2, heads, dk), dec_state, *side_ws)


def _merge_kernel(hx_ref, a_ref, o_ref, ht_ref, at_ref, ot_ref, wga_ref, wgb_ref, wa_ref, wb_ref, side_a_ref,
                  side_b_ref, mix_ref, side_a_out_ref, side_b_out_ref, *, n_full, n_slabs):
    j, i = pl.program_id(0), pl.program_id(1)
    _slab_cast(j * pl.num_programs(1) + i, n_slabs, side_a_ref, side_a_out_ref)
    _slab_cast(j * pl.num_programs(1) + i, n_slabs, side_b_ref, side_b_out_ref)

    def body(rows, is_tail):
        hx = ht_ref[...] if is_tail else hx_ref[...]
        a = at_ref[...] if is_tail else a_ref[...]
        o = ot_ref[...].astype(BF16) if is_tail else o_ref[...]
        for sub in range(mix_ref.shape[1] // MERGE_SUB_BLOCK):
            cs = slice(sub * MERGE_SUB_BLOCK, (sub + 1) * MERGE_SUB_BLOCK)
            ga = _sigmoid(_dot(hx, wga_ref[:, cs]))
            gb = _sigmoid(_dot(hx, wgb_ref[:, cs]))
            mix = ga * _dot(a, wa_ref[:, cs]) + gb * _dot(o, wb_ref[:, cs])
            mix_ref[0:rows, cs] = mix.astype(mix_ref.dtype)

    _row_split(i, n_full, hx_ref.shape[0], ht_ref.shape[0], body)


def _merge(hx, a, o, hx_tail, a_tail, o_tail, w_in, w_a, w_b, col_ga, tm, tn, side_a, side_b, n_slabs):
    d = hx.shape[1]
    m = a.shape[0] + hx_tail.shape[0]
    dc, dh = a.shape[1], o.shape[1]
    tail = a_tail.shape[0]
    n_full = a.shape[0] // tm
    n_n = d // tn
    clamp = lambda j, i: (jnp.minimum(i, n_full - 1), 0)
    assert n_slabs <= n_n * (n_full + 1)
    step_of = lambda j, i: j * (n_full + 1) + i
    weights = pl.Buffered(1) if n_n == 1 else None
    a_in, a_out, a_shape = _slab_specs(side_a, n_slabs, step_of)
    b_in, b_out, b_shape = _slab_specs(side_b, n_slabs, step_of)
    return pl.pallas_call(
        functools.partial(_merge_kernel, n_full=n_full, n_slabs=n_slabs),
        grid=(n_n, n_full + 1),
        in_specs=[pl.BlockSpec((tm, d), clamp),
                  pl.BlockSpec((tm, dc), clamp),
                  pl.BlockSpec((tm, dh), clamp),
                  pl.BlockSpec((tail, d), lambda j, i: (0, 0)),
                  pl.BlockSpec((tail, dc), lambda j, i: (0, 0)),
                  pl.BlockSpec((tail, dh), lambda j, i: (0, 0)),
                  pl.BlockSpec((d, tn), lambda j, i: (0, col_ga // tn + j), pipeline_mode=weights),
                  pl.BlockSpec((d, tn), lambda j, i: (0, col_ga // tn + n_n + j), pipeline_mode=weights),
                  pl.BlockSpec((dc, tn), lambda j, i: (0, j), pipeline_mode=weights),
                  pl.BlockSpec((dh, tn), lambda j, i: (0, j), pipeline_mode=weights),
                  a_in, b_in],
        out_specs=[pl.BlockSpec((tm, tn), lambda j, i: (i, j)), a_out, b_out],
        out_shape=[jax.ShapeDtypeStruct((m, d), BF16), a_shape, b_shape],
        compiler_params=_params(("arbitrary", "arbitrary"), VMEM_LIMIT_MLP),
        name="merge",
    )(hx, a, o, hx_tail, a_tail, o_tail, w_in, w_in, w_a, w_b, side_a, side_b)


def _outproj_kernel(xp_ref, xs_ref, mix_ref, w_ref, g_ref, x1_ref, h2_ref, *, n_full):
    i = pl.program_id(0)

    def body(rows, is_tail):
        x = xs_ref[...] if is_tail else xp_ref[...]
        x1 = x + _dot(mix_ref[0:rows, :], w_ref[...])
        x1_ref[0:rows, :] = x1
        h2_ref[0:rows, :] = _rms_scale(x1, g_ref[...]).astype(h2_ref.dtype)

    _row_split(i, n_full, xp_ref.shape[0], xs_ref.shape[0], body)


def _outproj(xp, xs, mix, w_out, g, tm):
    mp, d = xp.shape
    tail = xs.shape[0]
    n_full = mp // tm
    return pl.pallas_call(
        functools.partial(_outproj_kernel, n_full=n_full),
        grid=(n_full + 1,),
        in_specs=[pl.BlockSpec((tm, d), lambda i: (jnp.minimum(i, n_full - 1), 0)),
                  pl.BlockSpec((tail, d), lambda i: (0, 0)),
                  pl.BlockSpec((tm, d), lambda i: (i, 0)),
                  pl.BlockSpec((d, d), lambda i: (0, 0)),
                  pl.BlockSpec((1, d), lambda i: (0, 0))],
        out_specs=[pl.BlockSpec((tm, d), lambda i: (i, 0)),
                   pl.BlockSpec((tm, d), lambda i: (i, 0))],
        out_shape=[jax.ShapeDtypeStruct((mp + tail, d), F32),
                   jax.ShapeDtypeStruct((mp + tail, d), BF16)],
        compiler_params=_params(("arbitrary",)),
        name="outproj",
    )(xp, xs, mix, w_out, g.reshape(1, d))


def _mlp_kernel(h2_ref, x1_ref, wup_ref, wdn_ref, g_ref, yp_ref, ys_ref, *, n_full):
    i, j = pl.program_id(0), pl.program_id(1)
    last = pl.num_programs(1) - 1

    def body(rows, is_tail):
        y_ref = ys_ref if is_tail else yp_ref

        def step(first, final):
            halves = 2 if (final and not is_tail) else 1
            for hf in range(halves):
                rs = slice(hf * rows // halves, (hf + 1) * rows // halves)
                h = jnp.maximum(_dot(h2_ref[rs, :], wup_ref[...]), 0.0)
                acc = (x1_ref[rs, :] if first else y_ref[rs, :]) + _dot((h * h).astype(BF16), wdn_ref[...])
                y_ref[rs, :] = _rms_scale(acc, g_ref[...]) if final else acc

        @pl.when(j == 0)
        def _():
            step(True, False)

        @pl.when((j > 0) & (j < last))
        def _():
            step(False, False)

        @pl.when(j == last)
        def _():
            step(False, True)

    _row_split(i, n_full, yp_ref.shape[0], ys_ref.shape[0], body)


def _mlp(h2, x1, w_up, w_down, g, tm, tf, tail):
    m, d = h2.shape
    dff = w_up.shape[1]
    assert dff // tf >= 2
    n_full = (m - tail) // tm
    return pl.pallas_call(
        functools.partial(_mlp_kernel, n_full=n_full),
        grid=(n_full + 1, dff // tf),
        in_specs=[pl.BlockSpec((tm, d), lambda i, j: (i, 0)),
                  pl.BlockSpec((tm, d), lambda i, j: (i, 0)),
                  pl.BlockSpec((d, tf), lambda i, j: (0, j)),
                  pl.BlockSpec((tf, d), lambda i, j: (j, 0)),
                  pl.BlockSpec((1, d), lambda i, j: (0, 0))],
        out_specs=[pl.BlockSpec((tm, d), lambda i, j: (jnp.minimum(i, n_full - 1), 0)),
                   pl.BlockSpec((tail, d), lambda i, j: (0, 0))],
        out_shape=[jax.ShapeDtypeStruct((m - tail, d), F32),
                   jax.ShapeDtypeStruct((tail, d), F32)],
        compiler_params=_params(("arbitrary", "arbitrary"), VMEM_LIMIT_MLP),
        name="mlp",
    )(h2, x1, w_up, w_down, g.reshape(1, d))


def kernel(x_prompt, x_sample, state_conv, state_hgrn, norm_mix, w_in, conv_w, lb_logits, onorm_g,
           w_branch_a, w_branch_b, w_out, norm_ffn, w_up, w_down, norm_final):
    batch, seq, d = x_prompt.shape
    n_dec = x_sample.shape[0]
    depth, _, d_conv = conv_w.shape
    heads, dk = state_hgrn.shape[2], state_hgrn.shape[3]
    d_hgrn = heads * dk
    mp = batch * seq
    assert depth == 1 and x_sample.shape[1] == 1 and dk == HEAD_DIM and state_hgrn.shape[4] == dk
    assert seq % ROW_BLOCK == 0 and mp % MLP_ROW_BLOCK == 0 and ROW_BLOCK % n_dec == 0
    col_hgrn = 0
    col_ga = col_hgrn + 4 * d_hgrn
    w_in0 = w_in.reshape(w_in.shape[1:])
    w_a0 = w_branch_a.reshape(w_branch_a.shape[1:])
    w_b0 = w_branch_b.reshape(w_branch_b.shape[1:])
    w_out0 = w_out.reshape(w_out.shape[1:])
    w_up0 = w_up.reshape(w_up.shape[1:])
    w_down0 = w_down.reshape(w_down.shape[1:])

    xp = x_prompt.reshape(mp, d)
    xs = x_sample.reshape(n_dec, d)
    hx, a_p, conv_p, w_rest_b = _conv_prompt(xp, norm_mix[0], w_in0, conv_w[0], batch, seq, d_conv,
                                             ROW_BLOCK, CONV_COL_BLOCK)
    hx_s, a_s, conv_s = _conv_sample(xs, norm_mix[0], w_in0, conv_w[0], state_conv[0], d_conv, CONV_COL_BLOCK)

    p_s = _proj(hx_s, w_rest_b, col_hgrn, 4 * d_hgrn, PROJ_COL_BLOCK, dk)
    o_p, hgrn_p, o_s, hgrn_s, w_up_b, w_a_b, w_b_b = _hgrn_prompt(
        hx, w_rest_b, col_hgrn, lb_logits, onorm_g[0], batch, seq, heads, HGRN_ROWS, HGRN_HEADS, (w_up0, w_a0, w_b0),
        p_s, state_hgrn[0])

    mix, w_out_b, w_down_b = _merge(hx, a_p, o_p, hx_s, a_s, o_s, w_rest_b, w_a_b, w_b_b,
                                    col_ga, ROW_BLOCK, MERGE_COL_BLOCK, w_out0, w_down0, SIDE_SLABS // 4)
    x1, h2 = _outproj(xp, xs, mix, w_out_b, norm_ffn[0], ROW_BLOCK)
    y_p, y_s = _mlp(h2, x1, w_up_b, w_down_b, norm_final, MLP_ROW_BLOCK, MLP_FF_BLOCK, n_dec)

    return (y_p.reshape(batch, seq, d), y_s.reshape(n_dec, 1, d),
            conv_p[None], hgrn_p[None], conv_s[None], hgrn_s[None])
```

```python
import functools

import jax
import jax.numpy as jnp
from jax import lax
from jax.experimental import pallas as pl
from jax.experimental.pallas import tpu as pltpu

EPS = 1e-6
CHUNK = 64
HEAD_DIM = 128
V7X_VMEM_BYTES = 64 * 1024 * 1024
VMEM_LIMIT = V7X_VMEM_BYTES * 7 // 8
VMEM_LIMIT_MLP = V7X_VMEM_BYTES * 31 // 32

ROW_BLOCK = 512
MLP_ROW_BLOCK = 512
MLP_FF_BLOCK = 2048
CONV_COL_BLOCK = 512
CONV_SUB_BLOCK = 256
PROJ_COL_BLOCK = 1024
MERGE_COL_BLOCK = 2048
MERGE_SUB_BLOCK = 1024
SIDE_SLABS = 64
HGRN_ROWS = 512
HGRN_SUB_ROWS = 256
HGRN_HEADS = 4

BF16 = jnp.bfloat16
F32 = jnp.float32


def _params(semantics, vmem_limit=VMEM_LIMIT):
    return pltpu.CompilerParams(dimension_semantics=semantics, vmem_limit_bytes=vmem_limit)


def _dot(a, b):
    return jnp.dot(a, b, preferred_element_type=F32)


def _dot_nt(a, b):
    return lax.dot_general(a, b, (((1,), (1,)), ((), ())), preferred_element_type=F32)


def _dot_tn(a, b):
    return lax.dot_general(a, b, (((0,), (0,)), ((), ())), preferred_element_type=F32)


def _sigmoid(x):
    return jax.nn.sigmoid(x)


def _silu(x):
    return x * jax.nn.sigmoid(x)


def _rms_scale(x, g):
    ms = jnp.mean(x * x, axis=-1, keepdims=True)
    return x * lax.rsqrt(ms + EPS) * g


def _row_split(i, n_full, full_rows, tail_rows, body):
    @pl.when(i < n_full)
    def _():
        body(full_rows, False)

    @pl.when(i == n_full)
    def _():
        body(tail_rows, True)


def _slab_cast(step, n_slabs, src_ref, dst_ref):
    @pl.when(step < n_slabs)
    def _():
        dst_ref[...] = src_ref[...].astype(dst_ref.dtype)


def _slab_specs(w, n_slabs, step_of):
    r, c = w.shape
    assert r % n_slabs == 0

    def index(*g):
        return (jnp.minimum(step_of(*g), n_slabs - 1), 0)

    spec = pl.BlockSpec((r // n_slabs, c), index)
    return spec, spec, jax.ShapeDtypeStruct((r, c), BF16)


def _conv_prompt_kernel(x_ref, g_ref, whc_ref, wbg_ref, wcg_ref, cw_ref, side_ref,
                        hx_out_ref, a_ref, nc_ref, side_out_ref,
                        h_ref, wb_ref, carry_ref, *, n_slabs, n_t, n_c):
    s = pl.program_id(0)
    tm, tc = a_ref.shape
    j = jnp.maximum(s - 1, 0)
    t = lax.rem(j // n_c, n_t)
    _slab_cast(s, n_slabs, side_ref, side_out_ref)

    def normalise(slot):
        hx = _rms_scale(x_ref[...], g_ref[...]).astype(BF16)
        h_ref[slot] = hx
        hx_out_ref[...] = hx
        return hx

    @pl.when(s == 0)
    def _():
        normalise(0)

    @pl.when((s >= 1) & (s <= n_c))
    def _():
        for c in range(n_c):
            @pl.when(s == c + 1)
            def _(c=c):
                wb_ref[c, 0] = whc_ref[...].astype(BF16)
                wb_ref[c, 1] = wbg_ref[...].astype(BF16)
                wb_ref[c, 2] = wcg_ref[...].astype(BF16)

    cw = cw_ref[...]
    row = lax.broadcasted_iota(jnp.int32, (tm, CONV_SUB_BLOCK), 0)

    def body(c, slot):
        @pl.when((t == 0) & (c == 0))
        def _():
            carry_ref[...] = jnp.zeros_like(carry_ref)

        last_channel = c == n_c - 1
        hx_next = normalise(1 - slot) if last_channel else None
        hx = h_ref[slot]
        n_sub = tc // CONV_SUB_BLOCK
        for sub in range(n_sub):
            cs = slice(sub * CONV_SUB_BLOCK, (sub + 1) * CONV_SUB_BLOCK)
            lhs = jnp.where(s < 0, hx_next, hx) if (last_channel and sub == n_sub - 1) else hx
            hc = _dot(lhs, wb_ref[c, 0, :, cs])
            bg = _dot(lhs, wb_ref[c, 1, :, cs])
            cg = _dot(lhs, wb_ref[c, 2, :, cs])
            u = cg * hc
            c0 = carry_ref[c, 0:1, cs]
            c1 = carry_ref[c, 1:2, cs]
            u1 = jnp.where(row == 0, c1, pltpu.roll(u, 1, 0))
            u2 = jnp.where(row == 0, c0, jnp.where(row == 1, c1, pltpu.roll(u, 2, 0)))
            conv = cw[0:1, cs] * u2 + cw[1:2, cs] * u1 + cw[2:3, cs] * u
            a_ref[:, cs] = (bg * conv).astype(a_ref.dtype)
            carry_ref[c, 0:2, cs] = u[tm - 2:tm, :]

        @pl.when(t == n_t - 1)
        def _():
            nc_ref[0, :, c * tc:(c + 1) * tc] = carry_ref[c, 0:2, :]

    phase = lax.rem(j, 2 * n_c)
    for m in range(2 * n_c):
        @pl.when((s >= 1) & (phase == m))
        def _(m=m):
            body(m % n_c, m // n_c)


def _conv_prompt(x, g, w_in, conv_w, batch, seq, d_conv, tm, tc):
    mp, d = x.shape
    n_c = d_conv // tc
    n_t = seq // tm
    rows_blocks = batch * n_t
    n_steps = n_c * rows_blocks
    conv = lambda s: jnp.maximum(s - 1, 0)
    norm = lambda s: jnp.minimum(s // n_c, rows_blocks - 1)
    col_rest = 3 * d_conv
    sw = tc * n_c
    n_col = (w_in.shape[1] - col_rest) // sw
    assert n_steps == 4 * n_col and col_rest % sw == 0
    tile = lambda s: jnp.minimum(s, n_steps - 1)
    side_in = pl.BlockSpec((d // 4, sw), lambda s: (tile(s) % 4, col_rest // sw + tile(s) // 4))
    side_out = pl.BlockSpec((d // 4, sw), lambda s: (tile(s) % 4, tile(s) // 4))
    side_shape = jax.ShapeDtypeStruct((d, w_in.shape[1] - col_rest), BF16)

    def wcol(k):
        return pl.BlockSpec((d, tc), lambda s: (0, k * n_c + jnp.minimum(conv(s), n_c - 1)),
                            pipeline_mode=pl.Buffered(1))

    return pl.pallas_call(
        functools.partial(_conv_prompt_kernel, n_slabs=n_steps, n_t=n_t, n_c=n_c),
        grid=(n_steps + 1,),
        in_specs=[pl.BlockSpec((tm, d), lambda s: (norm(s), 0)),
                  pl.BlockSpec((1, d), lambda s: (0, 0)),
                  wcol(0), wcol(1), wcol(2),
                  pl.BlockSpec((3, tc), lambda s: (0, conv(s) % n_c)),
                  side_in],
        out_specs=[pl.BlockSpec((tm, d), lambda s: (norm(s), 0)),
                   pl.BlockSpec((tm, tc), lambda s: (conv(s) // n_c, conv(s) % n_c)),
                   pl.BlockSpec((1, 2, d_conv), lambda s: (conv(s) // n_c // n_t, 0, 0)),
                   side_out],
        out_shape=[jax.ShapeDtypeStruct((mp, d), BF16),
                   jax.ShapeDtypeStruct((mp, d_conv), BF16),
                   jax.ShapeDtypeStruct((batch, 2, d_conv), F32),
                   side_shape],
        scratch_shapes=[pltpu.VMEM((2, tm, d), BF16), pltpu.VMEM((n_c, 3, d, tc), BF16),
                        pltpu.VMEM((n_c, 8, tc), F32)],
        compiler_params=_params(("arbitrary",)),
        name="conv_prompt",
    )(x, g.reshape(1, d), w_in, w_in, w_in, conv_w, w_in)


def _conv_sample_kernel(x_ref, g_ref, whc_ref, wbg_ref, wcg_ref, cw_ref, st_ref, hx_ref, a_ref, nc_ref):
    hx = _rms_scale(x_ref[...], g_ref[...]).astype(BF16)
    hx_ref[...] = hx
    hc = _dot(hx, whc_ref[...].astype(BF16))
    bg = _dot(hx, wbg_ref[...].astype(BF16))
    cg = _dot(hx, wcg_ref[...].astype(BF16))
    u = cg * hc
    cw = cw_ref[...]
    s1 = st_ref[:, 1, :]
    conv = cw[0:1, :] * st_ref[:, 0, :] + cw[1:2, :] * s1 + cw[2:3, :] * u
    a_ref[...] = (bg * conv).astype(a_ref.dtype)
    nc_ref[:, 0, :] = s1
    nc_ref[:, 1, :] = u


def _conv_sample(x, g, w_in, conv_w, state, d_conv, tc):
    n, d = x.shape
    n_c = d_conv // tc
    return pl.pallas_call(
        _conv_sample_kernel,
        grid=(n_c,),
        in_specs=[pl.BlockSpec((n, d), lambda c: (0, 0)),
                  pl.BlockSpec((1, d), lambda c: (0, 0)),
                  pl.BlockSpec((d, tc), lambda c: (0, c)),
                  pl.BlockSpec((d, tc), lambda c: (0, n_c + c)),
                  pl.BlockSpec((d, tc), lambda c: (0, 2 * n_c + c)),
                  pl.BlockSpec((3, tc), lambda c: (0, c)),
                  pl.BlockSpec((n, 2, tc), lambda c: (0, 0, c))],
        out_specs=[pl.BlockSpec((n, d), lambda c: (0, 0)),
                   pl.BlockSpec((n, tc), lambda c: (0, c)),
                   pl.BlockSpec((n, 2, tc), lambda c: (0, 0, c))],
        out_shape=[jax.ShapeDtypeStruct((n, d), BF16),
                   jax.ShapeDtypeStruct((n, d_conv), BF16),
                   jax.ShapeDtypeStruct((n, 2, d_conv), F32)],
        compiler_params=pltpu.CompilerParams(dimension_semantics=("arbitrary",), vmem_limit_bytes=VMEM_LIMIT,
                                             allow_input_fusion=[True] + [False] * 6),
        name="conv_sample",
    )(x, g.reshape(1, d), w_in, w_in, w_in, conv_w, state)


def _proj_kernel(x_ref, w_ref, o_ref):
    res = _dot(x_ref[...], w_ref[...])
    dk = o_ref.shape[2]
    for h in range(o_ref.shape[1]):
        o_ref[:, h, :] = res[:, h * dk:(h + 1) * dk]


def _proj(x, w, col0, n, tn, dk):
    rows, d = x.shape
    return pl.pallas_call(
        _proj_kernel,
        grid=(n // tn,),
        in_specs=[pl.BlockSpec((rows, d), lambda j: (0, 0)),
                  pl.BlockSpec((d, tn), lambda j: (0, col0 // tn + j))],
        out_specs=pl.BlockSpec((rows, tn // dk, dk), lambda j: (0, j, 0)),
        out_shape=jax.ShapeDtypeStruct((rows, n // dk, dk), F32),
        compiler_params=_params(("arbitrary",)),
        name="hgrn_proj_decode",
    )(x, w)


def _lower_bound(lbl):
    e = jnp.exp(lbl - jnp.max(lbl, axis=0))
    return e[0] / jnp.sum(e, axis=0)


def _head_norm_gate(o, g, og):
    return _rms_scale(o, g) * _silu(og)


def _cumsum_rows(x, tril_bf16):
    hi = x.astype(BF16)
    r1 = x - hi.astype(F32)
    mid = r1.astype(BF16)
    lo = (r1 - mid.astype(F32)).astype(BF16)
    n = x.shape[1]
    parts = _dot(tril_bf16, jnp.concatenate([hi, mid, lo], axis=1))
    return parts[:, 0:n] + parts[:, n:2 * n] + parts[:, 2 * n:3 * n]


def _hgrn_decode_step(p_ref, lb, g, s0_ref, o_ref, row0, sn_ref):
    bb, heads, dk = s0_ref.shape[0], s0_ref.shape[1], s0_ref.shape[2]
    head_row = lax.broadcasted_iota(jnp.int32, (heads, dk), 0)

    def column(row):
        return jnp.broadcast_to(row, (dk, dk)).T

    for i in range(bb):
        p = p_ref[i]
        qs = _silu(p[0:heads])
        f = lb + (1.0 - lb) * _sigmoid(p[heads:2 * heads])
        kk = 1.0 - f
        v = p[2 * heads:3 * heads]
        og = p[3 * heads:4 * heads]
        q_in = (qs * f).astype(BF16)
        o_inter = jnp.zeros((heads, dk), F32)
        for h in range(heads):
            s = s0_ref[i, h]
            sn_ref[i, h] = column(f[h:h + 1, :]) * s + column(kk[h:h + 1, :]) * v[h:h + 1, :]
            o_inter = o_inter + jnp.where(head_row == h, _dot(q_in, s.astype(BF16)), 0.0)
        o = jnp.sum(qs * kk, axis=-1, keepdims=True) * v + o_inter
        o = _head_norm_gate(o, g, og)
        o_ref[pl.ds(row0 + i, 1), :] = jnp.concatenate([o[h:h + 1, :] for h in range(heads)], axis=1)


def _hgrn_prompt_kernel(hx_ref, wq_ref, wf_ref, wi_ref, wo_ref, lbl_ref, g_ref, dp_ref, dlbl_ref, ds0_ref, *refs,
                        n_side, n_slabs, n_t):
    side_refs, refs = refs[:n_side], refs[n_side:]
    o_ref, s_ref, do_ref, dsn_ref = refs[:4]
    side_out_refs, (pa_ref, pb_ref, st_ref) = refs[4:4 + n_side], refs[4 + n_side:]
    s = pl.program_id(0)
    for side_ref, side_out_ref in zip(side_refs, side_out_refs):
        _slab_cast(s, n_slabs, side_ref, side_out_ref)
    tb = hx_ref.shape[0]
    heads_per_step = st_ref.shape[0]
    dk = HEAD_DIM
    sub = HGRN_SUB_ROWS
    n_sub = tb // sub
    n_chunks = sub // CHUNK
    t = lax.rem(jnp.maximum(s - 1, 0), n_t)

    @pl.when(s == 0)
    def _():
        pb_ref[...] = jnp.zeros_like(pb_ref)

    @pl.when(t == 0)
    def _():
        st_ref[...] = jnp.zeros_like(st_ref)

    lb_all = _lower_bound(lbl_ref[...])
    g = g_ref[...]
    row = lax.broadcasted_iota(jnp.int32, (sub, sub), 0)
    col = lax.broadcasted_iota(jnp.int32, (sub, sub), 1)
    shift = CHUNK.bit_length() - 1
    causal = (row >= col) & (jnp.right_shift(row, shift) == jnp.right_shift(col, shift))
    tril = causal.astype(BF16)

    def sub_block(p_next_ref, p_ref, sb):
        heads = range(heads_per_step)
        cols = [slice(hh * dk, (hh + 1) * dk) for hh in heads]
        rows = slice(sb * sub, (sb + 1) * sub)
        seqs = ds0_ref.shape[0] // n_sub
        own = pl.ds(sb * seqs, seqs)

        def project(k, w_ref):
            p_next_ref[k, rows] = _dot(hx_ref[rows, :], w_ref[...])

        qs, kk, lf = [], [], []
        for hh in heads:
            lb = lb_all[:, cols[hh]]
            f = lb + (1.0 - lb) * _sigmoid(p_ref[1, rows, cols[hh]])
            qs.append(_silu(p_ref[0, rows, cols[hh]]))
            kk.append(1.0 - f)
            lf.append(jnp.log(f))
        project(0, wq_ref)
        b = [_cumsum_rows(lf[hh], tril) for hh in heads]
        q_in, k_in, k_end, decay = [], [], [], []
        for hh in heads:
            b3 = b[hh].reshape(n_chunks, CHUNK, dk)
            b_last = b3[:, CHUNK - 1:CHUNK, :]
            b_end = jnp.broadcast_to(b_last, b3.shape).reshape(sub, dk)
            q_in.append((qs[hh] * jnp.exp(b[hh])).astype(BF16))
            k_in.append((kk[hh] * jnp.exp(-b[hh])).astype(BF16))
            k_end.append((kk[hh] * jnp.exp(b_end - b[hh])).astype(BF16))
            decay.append(jnp.exp(b_last))
        project(1, wf_ref)
        v = [p_ref[2, rows, cols[hh]].astype(BF16) for hh in heads]
        raw = [_dot_nt(q_in[hh], k_in[hh]) for hh in heads]
        delta = [[_dot_tn(v[hh][c * CHUNK:(c + 1) * CHUNK], k_end[hh][c * CHUNK:(c + 1) * CHUNK])
                  for c in range(n_chunks)] for hh in heads]
        scores = [jnp.where(causal, raw[hh], 0.0).astype(BF16) for hh in heads]
        starts = []
        for hh in heads:
            st = st_ref[hh]
            per_chunk = []
            for c in range(n_chunks):
                per_chunk.append(st.astype(BF16))
                st = decay[hh][c] * st + delta[hh][c]
            st_ref[hh] = st
            starts.append(per_chunk)
        project(2, wi_ref)
        row0 = jnp.minimum(s, n_slabs - 1) * ds0_ref.shape[0] + sb * seqs
        _hgrn_decode_step(dp_ref.at[own], _lower_bound(dlbl_ref[...]), g, ds0_ref.at[own], do_ref, row0,
                          dsn_ref.at[own])
        o = []
        for hh in heads:
            inter = [_dot_nt(q_in[hh][c * CHUNK:(c + 1) * CHUNK], starts[hh][c]) for c in range(n_chunks)]
            o.append(_dot(scores[hh], v[hh]) + jnp.concatenate(inter, axis=0))
        for hh in heads:
            o_ref[rows, cols[hh]] = _head_norm_gate(o[hh], g, p_ref[3, rows, cols[hh]]).astype(o_ref.dtype)
        project(3, wo_ref)

    def body(p_next_ref, p_ref):
        for sb in range(n_sub):
            sub_block(p_next_ref, p_ref, sb)

    parity = lax.rem(s, 2)

    @pl.when(parity == 0)
    def _():
        body(pa_ref, pb_ref)

    @pl.when(parity == 1)
    def _():
        body(pb_ref, pa_ref)

    @pl.when(t == n_t - 1)
    def _():
        for hh in range(heads_per_step):
            s_ref[0, hh] = st_ref[hh].T


def _hgrn_prompt(hx, w_in, col0, lb_logits, onorm_g, batch, seq, heads, tb, hp, side_ws, dec_p, dec_state):
    d = hx.shape[1]
    n_t = seq // tb
    n_h = heads // hp
    dk = HEAD_DIM
    wc = hp * dk
    assert CHUNK & (CHUNK - 1) == 0 and col0 % wc == 0
    rows_blocks = batch * n_t
    n_steps = n_h * rows_blocks
    proj = lambda s: jnp.minimum(s, n_steps - 1)
    rec = lambda s: jnp.maximum(s - 1, 0)
    sides = [_slab_specs(w, n_steps, lambda s: s) for w in side_ws]

    n_dec, dec_rows = dec_p.shape[0], dec_p.shape[1]
    assert n_dec % n_steps == 0
    db = n_dec // n_steps
    dec = lambda s: jnp.minimum(s, n_steps - 1)

    def wcol(k):
        return pl.BlockSpec((d, wc), lambda s: (0, col0 // wc + k * n_h + proj(s) // rows_blocks))

    return pl.pallas_call(
        functools.partial(_hgrn_prompt_kernel, n_side=len(sides), n_slabs=n_steps, n_t=n_t),
        grid=(n_steps + 1,),
        in_specs=[pl.BlockSpec((tb, d), lambda s: (proj(s) % rows_blocks, 0)),
                  wcol(0), wcol(1), wcol(2), wcol(3),
                  pl.BlockSpec((2, 1, wc), lambda s: (0, 0, rec(s) // rows_blocks)),
                  pl.BlockSpec((1, dk), lambda s: (0, 0)),
                  pl.BlockSpec((db, dec_rows, dk), lambda s: (dec(s), 0, 0)),
                  pl.BlockSpec((2, heads, dk), lambda s: (0, 0, 0)),
                  pl.BlockSpec((db, heads, dk, dk), lambda s: (dec(s), 0, 0, 0))]
        + [side[0] for side in sides],
        out_specs=[pl.BlockSpec((tb, wc), lambda s: (rec(s) % rows_blocks, rec(s) // rows_blocks)),
                   pl.BlockSpec((1, hp, dk, dk),
                                lambda s: ((rec(s) % rows_blocks) // n_t, rec(s) // rows_blocks, 0, 0)),
                   pl.BlockSpec((n_dec, heads * dk), lambda s: (0, 0)),
                   pl.BlockSpec((db, heads, dk, dk), lambda s: (dec(s), 0, 0, 0))]
        + [side[1] for side in sides],
        out_shape=[jax.ShapeDtypeStruct((batch * seq, heads * dk), BF16),
                   jax.ShapeDtypeStruct((batch, heads, dk, dk), F32),
                   jax.ShapeDtypeStruct((n_dec, heads * dk), F32),
                   jax.ShapeDtypeStruct((n_dec, heads, dk, dk), F32)]
        + [side[2] for side in sides],
        scratch_shapes=[pltpu.VMEM((4, tb, wc), F32), pltpu.VMEM((4, tb, wc), F32),
                        pltpu.VMEM((hp, dk, dk), F32)],
        compiler_params=_params(("arbitrary",)),
        name="hgrn_prompt",
    )(hx, w_in, w_in, w_in, w_in, lb_logits.reshape(2, 1, heads * dk), onorm_g.reshape(1, dk),
      dec_p, lb_logits.reshape(2, heads, dk), dec_state, *side_ws)


def _merge_kernel(hx_ref, a_ref, o_ref, ht_ref, at_ref, ot_ref, wga_ref, wgb_ref, wa_ref, wb_ref, side_a_ref,
                  side_b_ref, mix_ref, side_a_out_ref, side_b_out_ref, *, n_full, n_slabs):
    j, i = pl.program_id(0), pl.program_id(1)
    _slab_cast(j * pl.num_programs(1) + i, n_slabs, side_a_ref, side_a_out_ref)
    _slab_cast(j * pl.num_programs(1) + i, n_slabs, side_b_ref, side_b_out_ref)

    def body(rows, is_tail):
        hx = ht_ref[...] if is_tail else hx_ref[...]
        a = at_ref[...] if is_tail else a_ref[...]
        o = ot_ref[...].astype(BF16) if is_tail else o_ref[...]
        for sub in range(mix_ref.shape[1] // MERGE_SUB_BLOCK):
            cs = slice(sub * MERGE_SUB_BLOCK, (sub + 1) * MERGE_SUB_BLOCK)
            ga = _sigmoid(_dot(hx, wga_ref[:, cs]))
            gb = _sigmoid(_dot(hx, wgb_ref[:, cs]))
            mix = ga * _dot(a, wa_ref[:, cs]) + gb * _dot(o, wb_ref[:, cs])
            mix_ref[0:rows, cs] = mix.astype(mix_ref.dtype)

    _row_split(i, n_full, hx_ref.shape[0], ht_ref.shape[0], body)


def _merge(hx, a, o, hx_tail, a_tail, o_tail, w_in, w_a, w_b, col_ga, tm, tn, side_a, side_b, n_slabs):
    d = hx.shape[1]
    m = a.shape[0] + hx_tail.shape[0]
    dc, dh = a.shape[1], o.shape[1]
    tail = a_tail.shape[0]
    n_full = a.shape[0] // tm
    n_n = d // tn
    clamp = lambda j, i: (jnp.minimum(i, n_full - 1), 0)
    assert n_slabs <= n_n * (n_full + 1)
    step_of = lambda j, i: j * (n_full + 1) + i
    weights = pl.Buffered(1) if n_n == 1 else None
    a_in, a_out, a_shape = _slab_specs(side_a, n_slabs, step_of)
    b_in, b_out, b_shape = _slab_specs(side_b, n_slabs, step_of)
    return pl.pallas_call(
        functools.partial(_merge_kernel, n_full=n_full, n_slabs=n_slabs),
        grid=(n_n, n_full + 1),
        in_specs=[pl.BlockSpec((tm, d), clamp),
                  pl.BlockSpec((tm, dc), clamp),
                  pl.BlockSpec((tm, dh), clamp),
                  pl.BlockSpec((tail, d), lambda j, i: (0, 0)),
                  pl.BlockSpec((tail, dc), lambda j, i: (0, 0)),
                  pl.BlockSpec((tail, dh), lambda j, i: (0, 0)),
                  pl.BlockSpec((d, tn), lambda j, i: (0, col_ga // tn + j), pipeline_mode=weights),
                  pl.BlockSpec((d, tn), lambda j, i: (0, col_ga // tn + n_n + j), pipeline_mode=weights),
                  pl.BlockSpec((dc, tn), lambda j, i: (0, j), pipeline_mode=weights),
                  pl.BlockSpec((dh, tn), lambda j, i: (0, j), pipeline_mode=weights),
                  a_in, b_in],
        out_specs=[pl.BlockSpec((tm, tn), lambda j, i: (i, j)), a_out, b_out],
        out_shape=[jax.ShapeDtypeStruct((m, d), BF16), a_shape, b_shape],
        compiler_params=_params(("arbitrary", "arbitrary"), VMEM_LIMIT_MLP),
        name="merge",
    )(hx, a, o, hx_tail, a_tail, o_tail, w_in, w_in, w_a, w_b, side_a, side_b)


def _outproj_kernel(xp_ref, xs_ref, mix_ref, w_ref, g_ref, x1_ref, h2_ref, *, n_full):
    i = pl.program_id(0)

    def body(rows, is_tail):
        x = xs_ref[...] if is_tail else xp_ref[...]
        x1 = x + _dot(mix_ref[0:rows, :], w_ref[...])
        x1_ref[0:rows, :] = x1
        h2_ref[0:rows, :] = _rms_scale(x1, g_ref[...]).astype(h2_ref.dtype)

    _row_split(i, n_full, xp_ref.shape[0], xs_ref.shape[0], body)


def _outproj(xp, xs, mix, w_out, g, tm):
    mp, d = xp.shape
    tail = xs.shape[0]
    n_full = mp // tm
    return pl.pallas_call(
        functools.partial(_outproj_kernel, n_full=n_full),
        grid=(n_full + 1,),
        in_specs=[pl.BlockSpec((tm, d), lambda i: (jnp.minimum(i, n_full - 1), 0)),
                  pl.BlockSpec((tail, d), lambda i: (0, 0)),
                  pl.BlockSpec((tm, d), lambda i: (i, 0)),
                  pl.BlockSpec((d, d), lambda i: (0, 0)),
                  pl.BlockSpec((1, d), lambda i: (0, 0))],
        out_specs=[pl.BlockSpec((tm, d), lambda i: (i, 0)),
                   pl.BlockSpec((tm, d), lambda i: (i, 0))],
        out_shape=[jax.ShapeDtypeStruct((mp + tail, d), F32),
                   jax.ShapeDtypeStruct((mp + tail, d), BF16)],
        compiler_params=pltpu.CompilerParams(dimension_semantics=("arbitrary",), vmem_limit_bytes=VMEM_LIMIT,
                                             allow_input_fusion=[False, True, False, False, False]),
        name="outproj",
    )(xp, xs, mix, w_out, g.reshape(1, d))


def _mlp_kernel(h2_ref, x1_ref, wup_ref, wdn_ref, g_ref, yp_ref, ys_ref, *, n_full):
    i, j = pl.program_id(0), pl.program_id(1)
    last = pl.num_programs(1) - 1

    def body(rows, is_tail):
        y_ref = ys_ref if is_tail else yp_ref

        def step(first, final):
            halves = 2 if (final and not is_tail) else 1
            for hf in range(halves):
                rs = slice(hf * rows // halves, (hf + 1) * rows // halves)
                h = jnp.maximum(_dot(h2_ref[rs, :], wup_ref[...]), 0.0)
                acc = (x1_ref[rs, :] if first else y_ref[rs, :]) + _dot((h * h).astype(BF16), wdn_ref[...])
                y_ref[rs, :] = _rms_scale(acc, g_ref[...]) if final else acc

        @pl.when(j == 0)
        def _():
            step(True, False)

        @pl.when((j > 0) & (j < last))
        def _():
            step(False, False)

        @pl.when(j == last)
        def _():
            step(False, True)

    _row_split(i, n_full, yp_ref.shape[0], ys_ref.shape[0], body)


def _mlp(h2, x1, w_up, w_down, g, tm, tf, tail):
    m, d = h2.shape
    dff = w_up.shape[1]
    assert dff // tf >= 2
    n_full = (m - tail) // tm
    return pl.pallas_call(
        functools.partial(_mlp_kernel, n_full=n_full),
        grid=(n_full + 1, dff // tf),
        in_specs=[pl.BlockSpec((tm, d), lambda i, j: (i, 0)),
                  pl.BlockSpec((tm, d), lambda i, j: (i, 0)),
                  pl.BlockSpec((d, tf), lambda i, j: (0, j)),
                  pl.BlockSpec((tf, d), lambda i, j: (j, 0)),
                  pl.BlockSpec((1, d), lambda i, j: (0, 0))],
        out_specs=[pl.BlockSpec((tm, d), lambda i, j: (jnp.minimum(i, n_full - 1), 0)),
                   pl.BlockSpec((tail, d), lambda i, j: (0, 0))],
        out_shape=[jax.ShapeDtypeStruct((m - tail, d), F32),
                   jax.ShapeDtypeStruct((tail, d), F32)],
        compiler_params=_params(("arbitrary", "arbitrary"), VMEM_LIMIT_MLP),
        name="mlp",
    )(h2, x1, w_up, w_down, g.reshape(1, d))


def kernel(x_prompt, x_sample, state_conv, state_hgrn, norm_mix, w_in, conv_w, lb_logits, onorm_g,
           w_branch_a, w_branch_b, w_out, norm_ffn, w_up, w_down, norm_final):
    batch, seq, d = x_prompt.shape
    n_dec = x_sample.shape[0]
    depth, _, d_conv = conv_w.shape
    heads, dk = state_hgrn.shape[2], state_hgrn.shape[3]
    d_hgrn = heads * dk
    mp = batch * seq
    assert depth == 1 and x_sample.shape[1] == 1 and dk == HEAD_DIM and state_hgrn.shape[4] == dk
    assert seq % ROW_BLOCK == 0 and mp % MLP_ROW_BLOCK == 0 and ROW_BLOCK % n_dec == 0
    col_hgrn = 0
    col_ga = col_hgrn + 4 * d_hgrn
    w_in0 = w_in.reshape(w_in.shape[1:])
    w_a0 = w_branch_a.reshape(w_branch_a.shape[1:])
    w_b0 = w_branch_b.reshape(w_branch_b.shape[1:])
    w_out0 = w_out.reshape(w_out.shape[1:])
    w_up0 = w_up.reshape(w_up.shape[1:])
    w_down0 = w_down.reshape(w_down.shape[1:])

    xp = x_prompt.reshape(mp, d)
    xs = x_sample.reshape(n_dec, d)
    hx, a_p, conv_p, w_rest_b = _conv_prompt(xp, norm_mix[0], w_in0, conv_w[0], batch, seq, d_conv,
                                             ROW_BLOCK, CONV_COL_BLOCK)
    hx_s, a_s, conv_s = _conv_sample(xs, norm_mix[0], w_in0, conv_w[0], state_conv[0], d_conv, CONV_COL_BLOCK)

    p_s = _proj(hx_s, w_rest_b, col_hgrn, 4 * d_hgrn, PROJ_COL_BLOCK, dk)
    o_p, hgrn_p, o_s, hgrn_s, w_up_b, w_a_b, w_b_b = _hgrn_prompt(
        hx, w_rest_b, col_hgrn, lb_logits, onorm_g[0], batch, seq, heads, HGRN_ROWS, HGRN_HEADS, (w_up0, w_a0, w_b0),
        p_s, state_hgrn[0])

    mix, w_out_b, w_down_b = _merge(hx, a_p, o_p, hx_s, a_s, o_s, w_rest_b, w_a_b, w_b_b,
                                    col_ga, ROW_BLOCK, MERGE_COL_BLOCK, w_out0, w_down0, SIDE_SLABS // 4)
    x1, h2 = _outproj(xp, xs, mix, w_out_b, norm_ffn[0], ROW_BLOCK)
    y_p, y_s = _mlp(h2, x1, w_up_b, w_down_b, norm_final, MLP_ROW_BLOCK, MLP_FF_BLOCK, n_dec)

    return (y_p.reshape(batch, seq, d), y_s.reshape(n_dec, 1, d),
            conv_p[None], hgrn_p[None], conv_s[None], hgrn_s[None])
```

```python
import functools

import jax
import jax.numpy as jnp
from jax import lax
from jax.experimental import pallas as pl
from jax.experimental.pallas import tpu as pltpu

EPS = 1e-6
CHUNK = 64
HEAD_DIM = 128
V7X_VMEM_BYTES = 64 * 1024 * 1024
VMEM_LIMIT = V7X_VMEM_BYTES * 7 // 8
VMEM_LIMIT_MLP = V7X_VMEM_BYTES * 31 // 32

ROW_BLOCK = 512
MLP_ROW_BLOCK = 512
MLP_FF_BLOCK = 2048
CONV_COL_BLOCK = 512
CONV_SUB_BLOCK = 256
PROJ_COL_BLOCK = 1024
MERGE_COL_BLOCK = 2048
MERGE_SUB_BLOCK = 1024
SIDE_SLABS = 64
HGRN_ROWS = 512
HGRN_SUB_ROWS = 256
HGRN_HEADS = 4

BF16 = jnp.bfloat16
F32 = jnp.float32


def _params(semantics, vmem_limit=VMEM_LIMIT):
    return pltpu.CompilerParams(dimension_semantics=semantics, vmem_limit_bytes=vmem_limit)


def _dot(a, b):
    return jnp.dot(a, b, preferred_element_type=F32)


def _dot_nt(a, b):
    return lax.dot_general(a, b, (((1,), (1,)), ((), ())), preferred_element_type=F32)


def _dot_tn(a, b):
    return lax.dot_general(a, b, (((0,), (0,)), ((), ())), preferred_element_type=F32)


def _sigmoid(x):
    return jax.nn.sigmoid(x)


def _silu(x):
    return x * jax.nn.sigmoid(x)


def _rms_scale(x, g):
    ms = jnp.mean(x * x, axis=-1, keepdims=True)
    return x * lax.rsqrt(ms + EPS) * g


def _row_split(i, n_full, full_rows, tail_rows, body):
    @pl.when(i < n_full)
    def _():
        body(full_rows, False)

    @pl.when(i == n_full)
    def _():
        body(tail_rows, True)


def _slab_cast(step, n_slabs, src_ref, dst_ref):
    @pl.when(step < n_slabs)
    def _():
        dst_ref[...] = src_ref[...].astype(dst_ref.dtype)


def _slab_specs(w, n_slabs, step_of):
    r, c = w.shape
    assert r % n_slabs == 0

    def index(*g):
        return (jnp.minimum(step_of(*g), n_slabs - 1), 0)

    spec = pl.BlockSpec((r // n_slabs, c), index)
    return spec, spec, jax.ShapeDtypeStruct((r, c), BF16)


def _conv_prompt_kernel(x_ref, g_ref, whc_ref, wbg_ref, wcg_ref, cw_ref, side_ref,
                        hx_out_ref, a_ref, nc_ref, side_out_ref,
                        h_ref, wb_ref, carry_ref, *, n_slabs, n_t, n_c):
    s = pl.program_id(0)
    tm, tc = a_ref.shape
    j = jnp.maximum(s - 1, 0)
    t = lax.rem(j // n_c, n_t)
    _slab_cast(s, n_slabs, side_ref, side_out_ref)

    def normalise(slot):
        hx = _rms_scale(x_ref[...], g_ref[...]).astype(BF16)
        h_ref[slot] = hx
        hx_out_ref[...] = hx
        return hx

    @pl.when(s == 0)
    def _():
        normalise(0)

    @pl.when((s >= 1) & (s <= n_c))
    def _():
        for c in range(n_c):
            @pl.when(s == c + 1)
            def _(c=c):
                wb_ref[c, 0] = whc_ref[...].astype(BF16)
                wb_ref[c, 1] = wbg_ref[...].astype(BF16)
                wb_ref[c, 2] = wcg_ref[...].astype(BF16)

    cw = cw_ref[...]
    row = lax.broadcasted_iota(jnp.int32, (tm, CONV_SUB_BLOCK), 0)

    def body(c, slot):
        @pl.when((t == 0) & (c == 0))
        def _():
            carry_ref[...] = jnp.zeros_like(carry_ref)

        last_channel = c == n_c - 1
        hx_next = normalise(1 - slot) if last_channel else None
        hx = h_ref[slot]
        n_sub = tc // CONV_SUB_BLOCK
        for sub in range(n_sub):
            cs = slice(sub * CONV_SUB_BLOCK, (sub + 1) * CONV_SUB_BLOCK)
            lhs = jnp.where(s < 0, hx_next, hx) if (last_channel and sub == n_sub - 1) else hx
            hc = _dot(lhs, wb_ref[c, 0, :, cs])
            bg = _dot(lhs, wb_ref[c, 1, :, cs])
            cg = _dot(lhs, wb_ref[c, 2, :, cs])
            u = cg * hc
            c0 = carry_ref[c, 0:1, cs]
            c1 = carry_ref[c, 1:2, cs]
            u1 = jnp.where(row == 0, c1, pltpu.roll(u, 1, 0))
            u2 = jnp.where(row == 0, c0, jnp.where(row == 1, c1, pltpu.roll(u, 2, 0)))
            conv = cw[0:1, cs] * u2 + cw[1:2, cs] * u1 + cw[2:3, cs] * u
            a_ref[:, cs] = (bg * conv).astype(a_ref.dtype)
            carry_ref[c, 0:2, cs] = u[tm - 2:tm, :]

        @pl.when(t == n_t - 1)
        def _():
            nc_ref[0, :, c * tc:(c + 1) * tc] = carry_ref[c, 0:2, :]

    phase = lax.rem(j, 2 * n_c)
    for m in range(2 * n_c):
        @pl.when((s >= 1) & (phase == m))
        def _(m=m):
            body(m % n_c, m // n_c)


def _conv_prompt(x, g, w_in, conv_w, batch, seq, d_conv, tm, tc):
    mp, d = x.shape
    n_c = d_conv // tc
    n_t = seq // tm
    rows_blocks = batch * n_t
    n_steps = n_c * rows_blocks
    conv = lambda s: jnp.maximum(s - 1, 0)
    norm = lambda s: jnp.minimum(s // n_c, rows_blocks - 1)
    col_rest = 3 * d_conv
    sw = tc * n_c
    n_col = (w_in.shape[1] - col_rest) // sw
    assert n_steps == 4 * n_col and col_rest % sw == 0
    tile = lambda s: jnp.minimum(s, n_steps - 1)
    side_in = pl.BlockSpec((d // 4, sw), lambda s: (tile(s) % 4, col_rest // sw + tile(s) // 4))
    side_out = pl.BlockSpec((d // 4, sw), lambda s: (tile(s) % 4, tile(s) // 4))
    side_shape = jax.ShapeDtypeStruct((d, w_in.shape[1] - col_rest), BF16)

    def wcol(k):
        return pl.BlockSpec((d, tc), lambda s: (0, k * n_c + jnp.minimum(conv(s), n_c - 1)),
                            pipeline_mode=pl.Buffered(1))

    return pl.pallas_call(
        functools.partial(_conv_prompt_kernel, n_slabs=n_steps, n_t=n_t, n_c=n_c),
        grid=(n_steps + 1,),
        in_specs=[pl.BlockSpec((tm, d), lambda s: (norm(s), 0)),
                  pl.BlockSpec((1, d), lambda s: (0, 0)),
                  wcol(0), wcol(1), wcol(2),
                  pl.BlockSpec((3, tc), lambda s: (0, conv(s) % n_c)),
                  side_in],
        out_specs=[pl.BlockSpec((tm, d), lambda s: (norm(s), 0)),
                   pl.BlockSpec((tm, tc), lambda s: (conv(s) // n_c, conv(s) % n_c)),
                   pl.BlockSpec((1, 2, d_conv), lambda s: (conv(s) // n_c // n_t, 0, 0)),
                   side_out],
        out_shape=[jax.ShapeDtypeStruct((mp, d), BF16),
                   jax.ShapeDtypeStruct((mp, d_conv), BF16),
                   jax.ShapeDtypeStruct((batch, 2, d_conv), F32),
                   side_shape],
        scratch_shapes=[pltpu.VMEM((2, tm, d), BF16), pltpu.VMEM((n_c, 3, d, tc), BF16),
                        pltpu.VMEM((n_c, 8, tc), F32)],
        compiler_params=_params(("arbitrary",)),
        name="conv_prompt",
    )(x, g.reshape(1, d), w_in, w_in, w_in, conv_w, w_in)


def _conv_sample_kernel(x_ref, g_ref, whc_ref, wbg_ref, wcg_ref, cw_ref, st_ref, hx_ref, a_ref, nc_ref):
    hx = _rms_scale(x_ref[...], g_ref[...]).astype(BF16)
    hx_ref[...] = hx
    hc = _dot(hx, whc_ref[...].astype(BF16))
    bg = _dot(hx, wbg_ref[...].astype(BF16))
    cg = _dot(hx, wcg_ref[...].astype(BF16))
    u = cg * hc
    cw = cw_ref[...]
    s1 = st_ref[:, 1, :]
    conv = cw[0:1, :] * st_ref[:, 0, :] + cw[1:2, :] * s1 + cw[2:3, :] * u
    a_ref[...] = (bg * conv).astype(a_ref.dtype)
    nc_ref[:, 0, :] = s1
    nc_ref[:, 1, :] = u


def _conv_sample(x, g, w_in, conv_w, state, d_conv, tc):
    n, d = x.shape
    n_c = d_conv // tc
    return pl.pallas_call(
        _conv_sample_kernel,
        grid=(n_c,),
        in_specs=[pl.BlockSpec((n, d), lambda c: (0, 0)),
                  pl.BlockSpec((1, d), lambda c: (0, 0)),
                  pl.BlockSpec((d, tc), lambda c: (0, c)),
                  pl.BlockSpec((d, tc), lambda c: (0, n_c + c)),
                  pl.BlockSpec((d, tc), lambda c: (0, 2 * n_c + c)),
                  pl.BlockSpec((3, tc), lambda c: (0, c)),
                  pl.BlockSpec((n, 2, tc), lambda c: (0, 0, c))],
        out_specs=[pl.BlockSpec((n, d), lambda c: (0, 0)),
                   pl.BlockSpec((n, tc), lambda c: (0, c)),
                   pl.BlockSpec((n, 2, tc), lambda c: (0, 0, c))],
        out_shape=[jax.ShapeDtypeStruct((n, d), BF16),
                   jax.ShapeDtypeStruct((n, d_conv), BF16),
                   jax.ShapeDtypeStruct((n, 2, d_conv), F32)],
        compiler_params=pltpu.CompilerParams(dimension_semantics=("arbitrary",), vmem_limit_bytes=VMEM_LIMIT,
                                             allow_input_fusion=[True] + [False] * 5 + [True]),
        name="conv_sample",
    )(x, g.reshape(1, d), w_in, w_in, w_in, conv_w, state)


def _proj_kernel(x_ref, w_ref, o_ref):
    res = _dot(x_ref[...], w_ref[...])
    dk = o_ref.shape[2]
    for h in range(o_ref.shape[1]):
        o_ref[:, h, :] = res[:, h * dk:(h + 1) * dk]


def _proj(x, w, col0, n, tn, dk):
    rows, d = x.shape
    return pl.pallas_call(
        _proj_kernel,
        grid=(n // tn,),
        in_specs=[pl.BlockSpec((rows, d), lambda j: (0, 0)),
                  pl.BlockSpec((d, tn), lambda j: (0, col0 // tn + j))],
        out_specs=pl.BlockSpec((rows, tn // dk, dk), lambda j: (0, j, 0)),
        out_shape=jax.ShapeDtypeStruct((rows, n // dk, dk), F32),
        compiler_params=_params(("arbitrary",)),
        name="hgrn_proj_decode",
    )(x, w)


def _lower_bound(lbl):
    e = jnp.exp(lbl - jnp.max(lbl, axis=0))
    return e[0] / jnp.sum(e, axis=0)


def _head_norm_gate(o, g, og):
    return _rms_scale(o, g) * _silu(og)


def _cumsum_rows(x, tril_bf16):
    hi = x.astype(BF16)
    r1 = x - hi.astype(F32)
    mid = r1.astype(BF16)
    lo = (r1 - mid.astype(F32)).astype(BF16)
    n = x.shape[1]
    parts = _dot(tril_bf16, jnp.concatenate([hi, mid, lo], axis=1))
    return parts[:, 0:n] + parts[:, n:2 * n] + parts[:, 2 * n:3 * n]


def _hgrn_decode_step(p_ref, lb, g, s0_ref, o_ref, row0, sn_ref):
    bb, heads, dk = s0_ref.shape[0], s0_ref.shape[1], s0_ref.shape[2]
    head_row = lax.broadcasted_iota(jnp.int32, (heads, dk), 0)

    def column(row):
        return jnp.broadcast_to(row, (dk, dk)).T

    for i in range(bb):
        p = p_ref[i]
        qs = _silu(p[0:heads])
        f = lb + (1.0 - lb) * _sigmoid(p[heads:2 * heads])
        kk = 1.0 - f
        v = p[2 * heads:3 * heads]
        og = p[3 * heads:4 * heads]
        q_in = (qs * f).astype(BF16)
        o_inter = jnp.zeros((heads, dk), F32)
        for h in range(heads):
            s = s0_ref[i, h]
            sn_ref[i, h] = column(f[h:h + 1, :]) * s + column(kk[h:h + 1, :]) * v[h:h + 1, :]
            o_inter = o_inter + jnp.where(head_row == h, _dot(q_in, s.astype(BF16)), 0.0)
        o = jnp.sum(qs * kk, axis=-1, keepdims=True) * v + o_inter
        o = _head_norm_gate(o, g, og)
        o_ref[pl.ds(row0 + i, 1), :] = jnp.concatenate([o[h:h + 1, :] for h in range(heads)], axis=1)


def _hgrn_prompt_kernel(hx_ref, wq_ref, wf_ref, wi_ref, wo_ref, lbl_ref, g_ref, dp_ref, dlbl_ref, ds0_ref, *refs,
                        n_side, n_slabs, n_t):
    side_refs, refs = refs[:n_side], refs[n_side:]
    o_ref, s_ref, do_ref, dsn_ref = refs[:4]
    side_out_refs, (pa_ref, pb_ref, st_ref) = refs[4:4 + n_side], refs[4 + n_side:]
    s = pl.program_id(0)
    for side_ref, side_out_ref in zip(side_refs, side_out_refs):
        _slab_cast(s, n_slabs, side_ref, side_out_ref)
    tb = hx_ref.shape[0]
    heads_per_step = st_ref.shape[0]
    dk = HEAD_DIM
    sub = HGRN_SUB_ROWS
    n_sub = tb // sub
    n_chunks = sub // CHUNK
    t = lax.rem(jnp.maximum(s - 1, 0), n_t)

    @pl.when(s == 0)
    def _():
        pb_ref[...] = jnp.zeros_like(pb_ref)

    @pl.when(t == 0)
    def _():
        st_ref[...] = jnp.zeros_like(st_ref)

    lb_all = _lower_bound(lbl_ref[...])
    g = g_ref[...]
    row = lax.broadcasted_iota(jnp.int32, (sub, sub), 0)
    col = lax.broadcasted_iota(jnp.int32, (sub, sub), 1)
    shift = CHUNK.bit_length() - 1
    causal = (row >= col) & (jnp.right_shift(row, shift) == jnp.right_shift(col, shift))
    tril = causal.astype(BF16)

    def sub_block(p_next_ref, p_ref, sb):
        heads = range(heads_per_step)
        cols = [slice(hh * dk, (hh + 1) * dk) for hh in heads]
        rows = slice(sb * sub, (sb + 1) * sub)
        seqs = ds0_ref.shape[0] // n_sub
        own = pl.ds(sb * seqs, seqs)

        def project(k, w_ref):
            p_next_ref[k, rows] = _dot(hx_ref[rows, :], w_ref[...])

        qs, kk, lf = [], [], []
        for hh in heads:
            lb = lb_all[:, cols[hh]]
            f = lb + (1.0 - lb) * _sigmoid(p_ref[1, rows, cols[hh]])
            qs.append(_silu(p_ref[0, rows, cols[hh]]))
            kk.append(1.0 - f)
            lf.append(jnp.log(f))
        project(0, wq_ref)
        b = [_cumsum_rows(lf[hh], tril) for hh in heads]
        q_in, k_in, k_end, decay = [], [], [], []
        for hh in heads:
            b3 = b[hh].reshape(n_chunks, CHUNK, dk)
            b_last = b3[:, CHUNK - 1:CHUNK, :]
            b_end = jnp.broadcast_to(b_last, b3.shape).reshape(sub, dk)
            q_in.append((qs[hh] * jnp.exp(b[hh])).astype(BF16))
            k_in.append((kk[hh] * jnp.exp(-b[hh])).astype(BF16))
            k_end.append((kk[hh] * jnp.exp(b_end - b[hh])).astype(BF16))
            decay.append(jnp.exp(b_last))
        project(1, wf_ref)
        v = [p_ref[2, rows, cols[hh]].astype(BF16) for hh in heads]
        raw = [_dot_nt(q_in[hh], k_in[hh]) for hh in heads]
        delta = [[_dot_tn(v[hh][c * CHUNK:(c + 1) * CHUNK], k_end[hh][c * CHUNK:(c + 1) * CHUNK])
                  for c in range(n_chunks)] for hh in heads]
        scores = [jnp.where(causal, raw[hh], 0.0).astype(BF16) for hh in heads]
        starts = []
        for hh in heads:
            st = st_ref[hh]
            per_chunk = []
            for c in range(n_chunks):
                per_chunk.append(st.astype(BF16))
                st = decay[hh][c] * st + delta[hh][c]
            st_ref[hh] = st
            starts.append(per_chunk)
        project(2, wi_ref)
        row0 = jnp.minimum(s, n_slabs - 1) * ds0_ref.shape[0] + sb * seqs
        _hgrn_decode_step(dp_ref.at[own], _lower_bound(dlbl_ref[...]), g, ds0_ref.at[own], do_ref, row0,
                          dsn_ref.at[own])
        o = []
        for hh in heads:
            inter = [_dot_nt(q_in[hh][c * CHUNK:(c + 1) * CHUNK], starts[hh][c]) for c in range(n_chunks)]
            o.append(_dot(scores[hh], v[hh]) + jnp.concatenate(inter, axis=0))
        for hh in heads:
            o_ref[rows, cols[hh]] = _head_norm_gate(o[hh], g, p_ref[3, rows, cols[hh]]).astype(o_ref.dtype)
        project(3, wo_ref)

    def body(p_next_ref, p_ref):
        for sb in range(n_sub):
            sub_block(p_next_ref, p_ref, sb)

    parity = lax.rem(s, 2)

    @pl.when(parity == 0)
    def _():
        body(pa_ref, pb_ref)

    @pl.when(parity == 1)
    def _():
        body(pb_ref, pa_ref)

    @pl.when(t == n_t - 1)
    def _():
        for hh in range(heads_per_step):
            s_ref[0, hh] = st_ref[hh].T


def _hgrn_prompt(hx, w_in, col0, lb_logits, onorm_g, batch, seq, heads, tb, hp, side_ws, dec_p, dec_state):
    d = hx.shape[1]
    n_t = seq // tb
    n_h = heads // hp
    dk = HEAD_DIM
    wc = hp * dk
    assert CHUNK & (CHUNK - 1) == 0 and col0 % wc == 0
    rows_blocks = batch * n_t
    n_steps = n_h * rows_blocks
    proj = lambda s: jnp.minimum(s, n_steps - 1)
    rec = lambda s: jnp.maximum(s - 1, 0)
    sides = [_slab_specs(w, n_steps, lambda s: s) for w in side_ws]

    n_dec, dec_rows = dec_p.shape[0], dec_p.shape[1]
    assert n_dec % n_steps == 0
    db = n_dec // n_steps
    dec = lambda s: jnp.minimum(s, n_steps - 1)

    def wcol(k):
        return pl.BlockSpec((d, wc), lambda s: (0, col0 // wc + k * n_h + proj(s) // rows_blocks))

    return pl.pallas_call(
        functools.partial(_hgrn_prompt_kernel, n_side=len(sides), n_slabs=n_steps, n_t=n_t),
        grid=(n_steps + 1,),
        in_specs=[pl.BlockSpec((tb, d), lambda s: (proj(s) % rows_blocks, 0)),
                  wcol(0), wcol(1), wcol(2), wcol(3),
                  pl.BlockSpec((2, 1, wc), lambda s: (0, 0, rec(s) // rows_blocks)),
                  pl.BlockSpec((1, dk), lambda s: (0, 0)),
                  pl.BlockSpec((db, dec_rows, dk), lambda s: (dec(s), 0, 0)),
                  pl.BlockSpec((2, heads, dk), lambda s: (0, 0, 0)),
                  pl.BlockSpec((db, heads, dk, dk), lambda s: (dec(s), 0, 0, 0))]
        + [side[0] for side in sides],
        out_specs=[pl.BlockSpec((tb, wc), lambda s: (rec(s) % rows_blocks, rec(s) // rows_blocks)),
                   pl.BlockSpec((1, hp, dk, dk),
                                lambda s: ((rec(s) % rows_blocks) // n_t, rec(s) // rows_blocks, 0, 0)),
                   pl.BlockSpec((n_dec, heads * dk), lambda s: (0, 0)),
                   pl.BlockSpec((db, heads, dk, dk), lambda s: (dec(s), 0, 0, 0))]
        + [side[1] for side in sides],
        out_shape=[jax.ShapeDtypeStruct((batch * seq, heads * dk), BF16),
                   jax.ShapeDtypeStruct((batch, heads, dk, dk), F32),
                   jax.ShapeDtypeStruct((n_dec, heads * dk), F32),
                   jax.ShapeDtypeStruct((n_dec, heads, dk, dk), F32)]
        + [side[2] for side in sides],
        scratch_shapes=[pltpu.VMEM((4, tb, wc), F32), pltpu.VMEM((4, tb, wc), F32),
                        pltpu.VMEM((hp, dk, dk), F32)],
        compiler_params=_params(("arbitrary",)),
        name="hgrn_prompt",
    )(hx, w_in, w_in, w_in, w_in, lb_logits.reshape(2, 1, heads * dk), onorm_g.reshape(1, dk),
      dec_p, lb_logits.reshape(2, heads, dk), dec_state, *side_ws)


def _merge_kernel(hx_ref, a_ref, o_ref, ht_ref, at_ref, ot_ref, wga_ref, wgb_ref, wa_ref, wb_ref, side_a_ref,
                  side_b_ref, mix_ref, side_a_out_ref, side_b_out_ref, *, n_full, n_slabs):
    j, i = pl.program_id(0), pl.program_id(1)
    _slab_cast(j * pl.num_programs(1) + i, n_slabs, side_a_ref, side_a_out_ref)
    _slab_cast(j * pl.num_programs(1) + i, n_slabs, side_b_ref, side_b_out_ref)

    def body(rows, is_tail):
        hx = ht_ref[...] if is_tail else hx_ref[...]
        a = at_ref[...] if is_tail else a_ref[...]
        o = ot_ref[...].astype(BF16) if is_tail else o_ref[...]
        for sub in range(mix_ref.shape[1] // MERGE_SUB_BLOCK):
            cs = slice(sub * MERGE_SUB_BLOCK, (sub + 1) * MERGE_SUB_BLOCK)
            ga = _sigmoid(_dot(hx, wga_ref[:, cs]))
            gb = _sigmoid(_dot(hx, wgb_ref[:, cs]))
            mix = ga * _dot(a, wa_ref[:, cs]) + gb * _dot(o, wb_ref[:, cs])
            mix_ref[0:rows, cs] = mix.astype(mix_ref.dtype)

    _row_split(i, n_full, hx_ref.shape[0], ht_ref.shape[0], body)


def _merge(hx, a, o, hx_tail, a_tail, o_tail, w_in, w_a, w_b, col_ga, tm, tn, side_a, side_b, n_slabs):
    d = hx.shape[1]
    m = a.shape[0] + hx_tail.shape[0]
    dc, dh = a.shape[1], o.shape[1]
    tail = a_tail.shape[0]
    n_full = a.shape[0] // tm
    n_n = d // tn
    clamp = lambda j, i: (jnp.minimum(i, n_full - 1), 0)
    assert n_slabs <= n_n * (n_full + 1)
    step_of = lambda j, i: j * (n_full + 1) + i
    weights = pl.Buffered(1) if n_n == 1 else None
    a_in, a_out, a_shape = _slab_specs(side_a, n_slabs, step_of)
    b_in, b_out, b_shape = _slab_specs(side_b, n_slabs, step_of)
    return pl.pallas_call(
        functools.partial(_merge_kernel, n_full=n_full, n_slabs=n_slabs),
        grid=(n_n, n_full + 1),
        in_specs=[pl.BlockSpec((tm, d), clamp),
                  pl.BlockSpec((tm, dc), clamp),
                  pl.BlockSpec((tm, dh), clamp),
                  pl.BlockSpec((tail, d), lambda j, i: (0, 0)),
                  pl.BlockSpec((tail, dc), lambda j, i: (0, 0)),
                  pl.BlockSpec((tail, dh), lambda j, i: (0, 0)),
                  pl.BlockSpec((d, tn), lambda j, i: (0, col_ga // tn + j), pipeline_mode=weights),
                  pl.BlockSpec((d, tn), lambda j, i: (0, col_ga // tn + n_n + j), pipeline_mode=weights),
                  pl.BlockSpec((dc, tn), lambda j, i: (0, j), pipeline_mode=weights),
                  pl.BlockSpec((dh, tn), lambda j, i: (0, j), pipeline_mode=weights),
                  a_in, b_in],
        out_specs=[pl.BlockSpec((tm, tn), lambda j, i: (i, j)), a_out, b_out],
        out_shape=[jax.ShapeDtypeStruct((m, d), BF16), a_shape, b_shape],
        compiler_params=_params(("arbitrary", "arbitrary"), VMEM_LIMIT_MLP),
        name="merge",
    )(hx, a, o, hx_tail, a_tail, o_tail, w_in, w_in, w_a, w_b, side_a, side_b)


def _outproj_kernel(xp_ref, xs_ref, mix_ref, w_ref, g_ref, x1_ref, h2_ref, *, n_full):
    i = pl.program_id(0)

    def body(rows, is_tail):
        x = xs_ref[...] if is_tail else xp_ref[...]
        x1 = x + _dot(mix_ref[0:rows, :], w_ref[...])
        x1_ref[0:rows, :] = x1
        h2_ref[0:rows, :] = _rms_scale(x1, g_ref[...]).astype(h2_ref.dtype)

    _row_split(i, n_full, xp_ref.shape[0], xs_ref.shape[0], body)


def _outproj(xp, xs, mix, w_out, g, tm):
    mp, d = xp.shape
    tail = xs.shape[0]
    n_full = mp // tm
    return pl.pallas_call(
        functools.partial(_outproj_kernel, n_full=n_full),
        grid=(n_full + 1,),
        in_specs=[pl.BlockSpec((tm, d), lambda i: (jnp.minimum(i, n_full - 1), 0)),
                  pl.BlockSpec((tail, d), lambda i: (0, 0)),
                  pl.BlockSpec((tm, d), lambda i: (i, 0)),
                  pl.BlockSpec((d, d), lambda i: (0, 0)),
                  pl.BlockSpec((1, d), lambda i: (0, 0))],
        out_specs=[pl.BlockSpec((tm, d), lambda i: (i, 0)),
                   pl.BlockSpec((tm, d), lambda i: (i, 0))],
        out_shape=[jax.ShapeDtypeStruct((mp + tail, d), F32),
                   jax.ShapeDtypeStruct((mp + tail, d), BF16)],
        compiler_params=pltpu.CompilerParams(dimension_semantics=("arbitrary",), vmem_limit_bytes=VMEM_LIMIT,
                                             allow_input_fusion=[False, True, False, False, False]),
        name="outproj",
    )(xp, xs, mix, w_out, g.reshape(1, d))


def _mlp_kernel(h2_ref, x1_ref, wup_ref, wdn_ref, g_ref, yp_ref, ys_ref, *, n_full):
    i, j = pl.program_id(0), pl.program_id(1)
    last = pl.num_programs(1) - 1

    def body(rows, is_tail):
        y_ref = ys_ref if is_tail else yp_ref

        def step(first, final):
            halves = 2 if (final and not is_tail) else 1
            for hf in range(halves):
                rs = slice(hf * rows // halves, (hf + 1) * rows // halves)
                h = jnp.maximum(_dot(h2_ref[rs, :], wup_ref[...]), 0.0)
                acc = (x1_ref[rs, :] if first else y_ref[rs, :]) + _dot((h * h).astype(BF16), wdn_ref[...])
                y_ref[rs, :] = _rms_scale(acc, g_ref[...]) if final else acc

        @pl.when(j == 0)
        def _():
            step(True, False)

        @pl.when((j > 0) & (j < last))
        def _():
            step(False, False)

        @pl.when(j == last)
        def _():
            step(False, True)

    _row_split(i, n_full, yp_ref.shape[0], ys_ref.shape[0], body)


def _mlp(h2, x1, w_up, w_down, g, tm, tf, tail):
    m, d = h2.shape
    dff = w_up.shape[1]
    assert dff // tf >= 2
    n_full = (m - tail) // tm
    return pl.pallas_call(
        functools.partial(_mlp_kernel, n_full=n_full),
        grid=(n_full + 1, dff // tf),
        in_specs=[pl.BlockSpec((tm, d), lambda i, j: (i, 0)),
                  pl.BlockSpec((tm, d), lambda i, j: (i, 0)),
                  pl.BlockSpec((d, tf), lambda i, j: (0, j)),
                  pl.BlockSpec((tf, d), lambda i, j: (j, 0)),
                  pl.BlockSpec((1, d), lambda i, j: (0, 0))],
        out_specs=[pl.BlockSpec((tm, d), lambda i, j: (jnp.minimum(i, n_full - 1), 0)),
                   pl.BlockSpec((tail, d), lambda i, j: (0, 0))],
        out_shape=[jax.ShapeDtypeStruct((m - tail, d), F32),
                   jax.ShapeDtypeStruct((tail, d), F32)],
        compiler_params=_params(("arbitrary", "arbitrary"), VMEM_LIMIT_MLP),
        name="mlp",
    )(h2, x1, w_up, w_down, g.reshape(1, d))


def kernel(x_prompt, x_sample, state_conv, state_hgrn, norm_mix, w_in, conv_w, lb_logits, onorm_g,
           w_branch_a, w_branch_b, w_out, norm_ffn, w_up, w_down, norm_final):
    batch, seq, d = x_prompt.shape
    n_dec = x_sample.shape[0]
    depth, _, d_conv = conv_w.shape
    heads, dk = state_hgrn.shape[2], state_hgrn.shape[3]
    d_hgrn = heads * dk
    mp = batch * seq
    assert depth == 1 and x_sample.shape[1] == 1 and dk == HEAD_DIM and state_hgrn.shape[4] == dk
    assert seq % ROW_BLOCK == 0 and mp % MLP_ROW_BLOCK == 0 and ROW_BLOCK % n_dec == 0
    col_hgrn = 0
    col_ga = col_hgrn + 4 * d_hgrn
    w_in0 = w_in.reshape(w_in.shape[1:])
    w_a0 = w_branch_a.reshape(w_branch_a.shape[1:])
    w_b0 = w_branch_b.reshape(w_branch_b.shape[1:])
    w_out0 = w_out.reshape(w_out.shape[1:])
    w_up0 = w_up.reshape(w_up.shape[1:])
    w_down0 = w_down.reshape(w_down.shape[1:])

    xp = x_prompt.reshape(mp, d)
    xs = x_sample.reshape(n_dec, d)
    hx, a_p, conv_p, w_rest_b = _conv_prompt(xp, norm_mix[0], w_in0, conv_w[0], batch, seq, d_conv,
                                             ROW_BLOCK, CONV_COL_BLOCK)
    hx_s, a_s, conv_s = _conv_sample(xs, norm_mix[0], w_in0, conv_w[0], state_conv[0], d_conv, CONV_COL_BLOCK)

    p_s = _proj(hx_s, w_rest_b, col_hgrn, 4 * d_hgrn, PROJ_COL_BLOCK, dk)
    o_p, hgrn_p, o_s, hgrn_s, w_up_b, w_a_b, w_b_b = _hgrn_prompt(
        hx, w_rest_b, col_hgrn, lb_logits, onorm_g[0], batch, seq, heads, HGRN_ROWS, HGRN_HEADS, (w_up0, w_a0, w_b0),
        p_s, state_hgrn[0])

    mix, w_out_b, w_down_b = _merge(hx, a_p, o_p, hx_s, a_s, o_s, w_rest_b, w_a_b, w_b_b,
                                    col_ga, ROW_BLOCK, MERGE_COL_BLOCK, w_out0, w_down0, SIDE_SLABS // 4)
    x1, h2 = _outproj(xp, xs, mix, w_out_b, norm_ffn[0], ROW_BLOCK)
    y_p, y_s = _mlp(h2, x1, w_up_b, w_down_b, norm_final, MLP_ROW_BLOCK, MLP_FF_BLOCK, n_dec)

    return (y_p.reshape(batch, seq, d), y_s.reshape(n_dec, 1, d),
            conv_p[None], hgrn_p[None], conv_s[None], hgrn_s[None])
```
